```python
import jax
import jax.numpy as jnp
from jax import lax
import numpy as np

D_MODEL = 1024
BATCH = 4
SEQ = 4096
DEPTH = 4

GRID_W = 64
CTX_LEN = 256
N_EVEN = (DEPTH + 1) // 2
N_ODD = DEPTH // 2
EPS = 1e-6
ROPE_BASE = 10000.0

RET_HEADS = 4
RET_DK = 128
RET_DV = 128
RET_CHUNK = 128
RET_QK = RET_HEADS * RET_DK
RET_VW = RET_HEADS * RET_DV
LRU_WIDTH = 512
LRU_BLOCKS = 8
LRU_BLOCK = LRU_WIDTH // LRU_BLOCKS
LRU_CONV = 4
LRU_C = 8.0
AB_SPLITS = (RET_QK, 2 * RET_QK, 2 * RET_QK + RET_VW, 2 * RET_QK + 2 * RET_VW,
             2 * RET_QK + 2 * RET_VW + LRU_WIDTH)
AB_IN = 2 * RET_QK + 2 * RET_VW + 2 * LRU_WIDTH
AB_MIX = RET_VW + LRU_WIDTH
MLA_HEADS = 16
MLA_NOPE = 64
MLA_ROPE = 32
MLA_V = 64
MLA_Q_RANK = 384
MLA_KV_RANK = 256
MLA_IN = MLA_Q_RANK + MLA_KV_RANK + MLA_ROPE
MLA_Q_BLOCK = 128
MLA_SCALE = (MLA_NOPE + MLA_ROPE) ** -0.5
MOE_GROUPS = 4
MOE_PER_GROUP = 8
MOE_EXPERTS = MOE_GROUPS * MOE_PER_GROUP
MOE_TOPK = 2
MOE_HIDDEN = 512
MOE_BLOCK = 128

kernel_name = 'hybrid_retention_rglru_mla_hmoe_dit'


def rmsnorm(x, g):
    xf = x.astype(jnp.float32)
    y = xf * lax.rsqrt(jnp.mean(xf * xf, axis=-1, keepdims=True) + EPS)
    return (y * g.astype(jnp.float32)).astype(x.dtype)


def modulate(h, shift, scale):
    return h * (1.0 + scale) + shift


def rotate(x, cos, sin):
    x1, x2 = jnp.split(x, 2, axis=-1)
    return jnp.concatenate([x1 * cos - x2 * sin, x1 * sin + x2 * cos], axis=-1).astype(x.dtype)


def to_heads(t, d):
    b, n, _ = t.shape
    return t.reshape(b, n, RET_HEADS, d).transpose(0, 2, 1, 3)


def from_heads(t):
    b, h, n, d = t.shape
    return t.transpose(0, 2, 1, 3).reshape(b, n, h * d)


def head_rms(y):
    return y * lax.rsqrt(jnp.mean(y * y, axis=-1, keepdims=True) + EPS)


def retention_dir(q, k, v, log_gamma, state0, want_out):
    b, h, n, dk = q.shape
    dv = v.shape[-1]
    nc = n // RET_CHUNK
    qc = q.reshape(b, h, nc, RET_CHUNK, dk)
    kc = k.reshape(b, h, nc, RET_CHUNK, dk)
    vc = v.reshape(b, h, nc, RET_CHUNK, dv)
    j = jnp.arange(RET_CHUNK, dtype=jnp.float32)
    lg = log_gamma[:, None]
    zeta = jnp.exp(lg * (RET_CHUNK - 1.0 - j))
    u = jnp.einsum('bhncd,bhnce->bhnde', kc, vc * zeta[None, :, None, :, None])
    chunk_decay = jnp.exp(log_gamma * RET_CHUNK)[None, :, None, None]

    def step(s, u_n):
        return chunk_decay * s + u_n, (s if want_out else None)

    s_final, s_in = lax.scan(step, state0, jnp.moveaxis(u, 2, 0))
    if not want_out:
        return None, s_final
    s_in = jnp.moveaxis(s_in, 0, 2)
    xi = jnp.exp(lg * (j + 1.0))
    inter = jnp.einsum('bhncd,bhnde->bhnce', qc, s_in) * xi[None, :, None, :, None]
    diff = j[:, None] - j[None, :]
    dmat = jnp.where(diff >= 0, jnp.exp(log_gamma[:, None, None] * jnp.maximum(diff, 0.0)), 0.0)
    scores = jnp.einsum('bhncd,bhnmd->bhncm', qc, kc) * dmat[None, :, None]
    intra = jnp.einsum('bhncm,bhnme->bhnce', scores, vc)
    return (intra + inter).reshape(b, h, n, dv), s_final


def retention_bidir(q_c, k_c, v_c, q_l, k_l, v_l, decay_logit, want_ctx):
    b, h, _, dk = q_l.shape
    zero = jnp.zeros((b, h, dk, v_l.shape[-1]), jnp.float32)
    out_c, out_l = None, None
    for d in range(2):
        lg = jax.nn.log_sigmoid(decay_logit[d].astype(jnp.float32))
        f = (lambda t: jnp.flip(t, axis=2)) if d == 1 else (lambda t: t)
        oc, s_ctx = retention_dir(f(q_c), f(k_c), f(v_c), lg, zero, want_ctx)
        ol, _ = retention_dir(f(q_l), f(k_l), f(v_l), lg, s_ctx, True)
        out_l = f(ol) if out_l is None else out_l + f(ol)
        if want_ctx:
            out_c = f(oc) if out_c is None else out_c + f(oc)
    return out_c, out_l


def centred_dwconv(x, w, bias):
    kw = w.shape[0]
    n = x.shape[1]
    xp = jnp.pad(x, ((0, 0), (kw // 2, (kw - 1) // 2), (0, 0)))
    y = xp[:, 0:n] * w[0]
    for t in range(1, kw):
        y = y + xp[:, t:t + n] * w[t]
    return y + bias


def rglru_coeffs(x, gate_w, gate_b, lam):
    b, n, w = x.shape
    xb = x.reshape(b, n, LRU_BLOCKS, LRU_BLOCK)
    gates = jnp.einsum('bnki,gkij->gbnkj', xb, gate_w).reshape(2, b, n, w) + gate_b[:, None, None, :]
    r = jax.nn.sigmoid(gates[0])
    i = jax.nn.sigmoid(gates[1])
    log_a = -LRU_C * r * jax.nn.softplus(-lam.astype(jnp.float32))
    a = jnp.exp(log_a)
    bterm = jnp.sqrt(-jnp.expm1(2.0 * log_a)) * (i * x)
    return a, bterm


def linear_scan(a, bterm, h0):
    def comb(left, right):
        return left[0] * right[0], right[0] * left[1] + right[1]
    a_cum, h = lax.associative_scan(comb, (a, bterm), axis=1)
    return h + a_cum * h0[:, None, :]


def rglru_bidir(xc, xl, gate_w, gate_b, lam, want_ctx):
    b, _, w = xl.shape
    zero = jnp.zeros((b, w), jnp.float32)
    out_c, out_l = None, None
    for d in range(2):
        f = (lambda t: jnp.flip(t, axis=1)) if d == 1 else (lambda t: t)
        a, bt = rglru_coeffs(f(xc), gate_w[d], gate_b[d], lam[d])
        h_ctx = linear_scan(a, bt, zero)
        a, bt = rglru_coeffs(f(xl), gate_w[d], gate_b[d], lam[d])
        h_lat = f(linear_scan(a, bt, h_ctx[:, -1]))
        out_l = h_lat if out_l is None else out_l + h_lat
        if want_ctx:
            out_c = f(h_ctx) if out_c is None else out_c + f(h_ctx)
    return out_c, out_l


def ab_mixer(hc, hl, w_in, w_out, decay_logit, conv_w, conv_b, gate_w, gate_b, lam, ret_cs, want_ctx):
    cos, sin = ret_cs
    pc = jnp.split((hc @ w_in).astype(jnp.float32), AB_SPLITS, axis=-1)
    pl = jnp.split((hl @ w_in).astype(jnp.float32), AB_SPLITS, axis=-1)
    k_scale = RET_DK ** -0.5
    q_c = to_heads(pc[0], RET_DK)
    k_c = to_heads(pc[1], RET_DK) * k_scale
    v_c = to_heads(pc[2], RET_DV)
    q_l = rotate(to_heads(pl[0], RET_DK), cos, sin)
    k_l = rotate(to_heads(pl[1], RET_DK), cos, sin) * k_scale
    v_l = to_heads(pl[2], RET_DV)
    rc, rl = retention_bidir(q_c, k_c, v_c, q_l, k_l, v_l, decay_logit, want_ctx)
    uc = centred_dwconv(pc[4], conv_w, conv_b)
    ul = centred_dwconv(pl[4], conv_w, conv_b)
    lc, ll = rglru_bidir(uc, ul, gate_w, gate_b, lam, want_ctx)

    def merge(r, g, hrec, y, dtype):
        ret = from_heads(head_rms(r)) * jax.nn.silu(g)
        rec = hrec * jax.nn.gelu(y)
        return (jnp.concatenate([ret, rec], axis=-1) @ w_out).astype(dtype)

    ol = merge(rl, pl[3], ll, pl[5], hl.dtype)
    oc = merge(rc, pc[3], lc, pc[5], hc.dtype) if want_ctx else None
    return oc, ol


def mla_project(h, w_in, q_g, kv_g, w_uq, w_ukv, cs):
    b, n, _ = h.shape
    p = h @ w_in
    cq, ckv, kr = jnp.split(p, [MLA_Q_RANK, MLA_Q_RANK + MLA_KV_RANK], axis=-1)
    q = (rmsnorm(cq, q_g) @ w_uq).reshape(b, n, MLA_HEADS, MLA_NOPE + MLA_ROPE)
    kv = (rmsnorm(ckv, kv_g) @ w_ukv).reshape(b, n, MLA_HEADS, MLA_NOPE + MLA_V)
    q_nope, q_rope = jnp.split(q, [MLA_NOPE], axis=-1)
    k_nope, v = jnp.split(kv, [MLA_NOPE], axis=-1)
    if cs is not None:
        cos, sin = cs
        q_rope = rotate(q_rope, cos[:, None, :], sin[:, None, :])
        kr = rotate(kr, cos, sin)
    k = jnp.concatenate([k_nope, jnp.broadcast_to(kr[:, :, None, :], (b, n, MLA_HEADS, MLA_ROPE))], axis=-1)
    q = jnp.concatenate([q_nope, q_rope], axis=-1)
    return q.astype(jnp.float32), k.astype(jnp.float32), v.astype(jnp.float32)


def attend(q, k, v):
    s = jnp.einsum('bqhd,bkhd->bhqk', q, k) * MLA_SCALE
    p = jax.nn.softmax(s, axis=-1)
    return jnp.einsum('bhqk,bkhd->bqhd', p, v)


def mla_mixer(hc, hl, w_in, q_g, kv_g, w_uq, w_ukv, w_out, mla_cs, want_ctx):
    qc, kc, vc = mla_project(hc, w_in, q_g, kv_g, w_uq, w_ukv, None)
    ql, kl, vl = mla_project(hl, w_in, q_g, kv_g, w_uq, w_ukv, mla_cs)
    b, n = hl.shape[:2]
    k_all = jnp.concatenate([kc, kl], axis=1)
    v_all = jnp.concatenate([vc, vl], axis=1)
    nqb = n // MLA_Q_BLOCK
    qb = ql.reshape(b, nqb, MLA_Q_BLOCK, MLA_HEADS, MLA_NOPE + MLA_ROPE).transpose(1, 0, 2, 3, 4)
    ob = lax.map(lambda qq: attend(qq, k_all, v_all), qb)
    ol = (ob.transpose(1, 0, 2, 3, 4).reshape(b, n, MLA_HEADS * MLA_V) @ w_out).astype(hl.dtype)
    if not want_ctx:
        return None, ol
    oc = (attend(qc, kc, vc).reshape(b, hc.shape[1], MLA_HEADS * MLA_V) @ w_out).astype(hc.dtype)
    return oc, ol


def hier_moe(h, group_w, group_b, expert_w, expert_b, w_gate, w_up, w_down):
    t_count, d = h.shape
    hf = h.astype(jnp.float32)
    group_p = jax.nn.softmax(hf @ group_w + group_b, axis=-1)
    g_top, g_sel = lax.top_k(group_p, 1)
    e_logits = (hf @ expert_w + expert_b).reshape(t_count, MOE_GROUPS, MOE_PER_GROUP)
    within = jnp.take_along_axis(e_logits, g_sel[:, :, None], axis=1)[:, 0]
    e_p = jax.nn.softmax(within, axis=-1)
    top_p, top_i = lax.top_k(e_p, MOE_TOPK)
    top_p = top_p / jnp.sum(top_p, axis=-1, keepdims=True)
    weights = (g_top * top_p).reshape(-1)
    e_flat = (g_sel * MOE_PER_GROUP + top_i).reshape(-1)
    tok = jnp.repeat(jnp.arange(t_count, dtype=jnp.int32), MOE_TOPK)
    n_assign = t_count * MOE_TOPK
    order = jnp.argsort(e_flat)
    e_sorted = e_flat[order]
    counts = jnp.zeros((MOE_EXPERTS,), jnp.int32).at[e_flat].add(1)
    padded = (counts + MOE_BLOCK - 1) // MOE_BLOCK * MOE_BLOCK
    start = jnp.cumsum(counts) - counts
    pend = jnp.cumsum(padded)
    pstart = pend - padded
    dest = pstart[e_sorted] + (jnp.arange(n_assign, dtype=jnp.int32) - start[e_sorted])
    n_rows = (n_assign + MOE_EXPERTS * (MOE_BLOCK - 1) + MOE_BLOCK - 1) // MOE_BLOCK * MOE_BLOCK
    n_blocks = n_rows // MOE_BLOCK
    row_tok = jnp.zeros((n_rows,), jnp.int32).at[dest].set(tok[order])
    row_w = jnp.zeros((n_rows,), jnp.float32).at[dest].set(weights[order])
    block_e = jnp.minimum(jnp.searchsorted(pend, jnp.arange(n_blocks, dtype=jnp.int32) * MOE_BLOCK,
                                           side='right'), MOE_EXPERTS - 1)
    xin = h[row_tok].reshape(n_blocks, MOE_BLOCK, d)

    def expert_block(args):
        xb, e = args
        return (jax.nn.silu(xb @ w_gate[e]) * (xb @ w_up[e])) @ w_down[e]

    yb = lax.map(expert_block, (xin, block_e)).reshape(n_rows, d)
    y = jnp.zeros((t_count, d), jnp.float32).at[row_tok].add(yb * row_w[:, None])
    return y.astype(h.dtype)


def setup_inputs(seed: int = 0) -> dict:
    key = jax.random.key(seed)
    keys = iter(jax.random.split(key, 40))

    def nrm(shape, scale):
        return jax.random.normal(next(keys), shape, jnp.float32) * scale

    dm = D_MODEL
    decay_base = jnp.asarray(np.log(2.0 ** (5 + np.arange(RET_HEADS)) - 1.0), jnp.float32)
    a_init = jax.random.uniform(next(keys), (N_EVEN, 2, LRU_WIDTH), jnp.float32, 0.9, 0.999) ** (1.0 / LRU_C)
    return {
        'x': nrm((BATCH, SEQ, dm), 1.0),
        'c': nrm((BATCH, dm), 1.0),
        'ctx': nrm((BATCH, CTX_LEN, dm), 1.0),
        'c_ctx': nrm((dm,), 1.0),
        'ada_w': nrm((DEPTH, dm, 6 * dm), 0.5 * dm ** -0.5),
        'ada_b': nrm((DEPTH, 6 * dm), 0.01),
        'norm_g': 1.0 + nrm((DEPTH, 2, dm), 0.02),
        'ab_w_in': nrm((N_EVEN, dm, AB_IN), dm ** -0.5),
        'ab_w_out': nrm((N_EVEN, AB_MIX, dm), AB_MIX ** -0.5),
        'ret_decay_logit': decay_base + nrm((N_EVEN, 2, RET_HEADS), 0.1),
        'lru_conv_w': nrm((N_EVEN, LRU_CONV, LRU_WIDTH), LRU_CONV ** -0.5),
        'lru_conv_b': nrm((N_EVEN, LRU_WIDTH), 0.01),
        'lru_gate_w': nrm((N_EVEN, 2, 2, LRU_BLOCKS, LRU_BLOCK, LRU_BLOCK), LRU_BLOCK ** -0.5),
        'lru_gate_b': nrm((N_EVEN, 2, 2, LRU_WIDTH), 0.01),
        'lru_lambda': jnp.log(a_init) - jnp.log1p(-a_init),
        'mla_w_in': nrm((N_ODD, dm, MLA_IN), dm ** -0.5),
        'mla_q_norm_g': 1.0 + nrm((N_ODD, MLA_Q_RANK), 0.02),
        'mla_kv_norm_g': 1.0 + nrm((N_ODD, MLA_KV_RANK), 0.02),
        'mla_w_uq': nrm((N_ODD, MLA_Q_RANK, MLA_HEADS * (MLA_NOPE + MLA_ROPE)), MLA_Q_RANK ** -0.5),
        'mla_w_ukv': nrm((N_ODD, MLA_KV_RANK, MLA_HEADS * (MLA_NOPE + MLA_V)), MLA_KV_RANK ** -0.5),
        'mla_w_out': nrm((N_ODD, MLA_HEADS * MLA_V, dm), (MLA_HEADS * MLA_V) ** -0.5),
        'moe_group_w': nrm((DEPTH, dm, MOE_GROUPS), dm ** -0.5),
        'moe_group_b': nrm((DEPTH, MOE_GROUPS), 0.01),
        'moe_expert_w': nrm((DEPTH, dm, MOE_EXPERTS), dm ** -0.5),
        'moe_expert_b': nrm((DEPTH, MOE_EXPERTS), 0.01),
        'moe_w_gate': nrm((DEPTH, MOE_EXPERTS, dm, MOE_HIDDEN), dm ** -0.5),
        'moe_w_up': nrm((DEPTH, MOE_EXPERTS, dm, MOE_HIDDEN), dm ** -0.5),
        'moe_w_down': nrm((DEPTH, MOE_EXPERTS, MOE_HIDDEN, dm), MOE_HIDDEN ** -0.5),
        'final_norm_g': 1.0 + nrm((dm,), 0.02),
    }


def reference(x, c, ctx, c_ctx, ada_w, ada_b, norm_g, ab_w_in, ab_w_out, ret_decay_logit,
              lru_conv_w, lru_conv_b, lru_gate_w, lru_gate_b, lru_lambda, mla_w_in, mla_q_norm_g,
              mla_kv_norm_g, mla_w_uq, mla_w_ukv, mla_w_out, moe_group_w, moe_group_b, moe_expert_w,
              moe_expert_b, moe_w_gate, moe_w_up, moe_w_down, final_norm_g):
    b, n, _ = x.shape
    ctx_len = ctx.shape[1]
    rows = n // GRID_W
    t = jnp.arange(n, dtype=jnp.float32)
    inv1 = ROPE_BASE ** (-jnp.arange(0, RET_DK, 2, dtype=jnp.float32) / RET_DK)
    ang1 = t[:, None] * inv1[None, :]
    ret_cs = (jnp.cos(ang1), jnp.sin(ang1))
    r_pos = jnp.repeat(jnp.arange(rows, dtype=jnp.float32), GRID_W)
    c_pos = jnp.tile(jnp.arange(GRID_W, dtype=jnp.float32), rows)
    ax_dim = MLA_ROPE // 2
    inv2 = ROPE_BASE ** (-jnp.arange(0, ax_dim, 2, dtype=jnp.float32) / ax_dim)
    ang2 = jnp.concatenate([r_pos[:, None] * inv2[None, :], c_pos[:, None] * inv2[None, :]], axis=-1)
    mla_cs = (jnp.cos(ang2), jnp.sin(ang2))

    xl, xc = x, ctx
    s_lat = jax.nn.silu(c)
    s_ctx = jax.nn.silu(c_ctx)
    for layer in range(DEPTH):
        want_ctx = layer != DEPTH - 1
        mod_l = jnp.split((s_lat @ ada_w[layer] + ada_b[layer])[:, None, :], 6, axis=-1)
        mod_c = jnp.split((s_ctx @ ada_w[layer] + ada_b[layer])[None, None, :], 6, axis=-1)
        hl = modulate(rmsnorm(xl, norm_g[layer, 0]), mod_l[0], mod_l[1])
        hc = modulate(rmsnorm(xc, norm_g[layer, 0]), mod_c[0], mod_c[1])
        i = layer // 2
        if layer % 2 == 0:
            oc, ol = ab_mixer(hc, hl, ab_w_in[i], ab_w_out[i], ret_decay_logit[i], lru_conv_w[i],
                              lru_conv_b[i], lru_gate_w[i], lru_gate_b[i], lru_lambda[i], ret_cs, want_ctx)
        else:
            oc, ol = mla_mixer(hc, hl, mla_w_in[i], mla_q_norm_g[i], mla_kv_norm_g[i], mla_w_uq[i],
                               mla_w_ukv[i], mla_w_out[i], mla_cs, want_ctx)
        xl = xl + mod_l[2] * ol
        hl = modulate(rmsnorm(xl, norm_g[layer, 1]), mod_l[3], mod_l[4])
        moe_params = (moe_group_w[layer], moe_group_b[layer], moe_expert_w[layer], moe_expert_b[layer],
                      moe_w_gate[layer], moe_w_up[layer], moe_w_down[layer])
        if want_ctx:
            xc = xc + mod_c[2] * oc
            hc = modulate(rmsnorm(xc, norm_g[layer, 1]), mod_c[3], mod_c[4])
            tokens = jnp.concatenate([hc.reshape(b * ctx_len, -1), hl.reshape(b * n, -1)], axis=0)
            y = hier_moe(tokens, *moe_params)
            xc = xc + mod_c[5] * y[:b * ctx_len].reshape(b, ctx_len, -1)
            xl = xl + mod_l[5] * y[b * ctx_len:].reshape(b, n, -1)
        else:
            y = hier_moe(hl.reshape(b * n, -1), *moe_params)
            xl = xl + mod_l[5] * y.reshape(b, n, -1)
    return rmsnorm(xl, final_norm_g)
```

```python
import functools

import jax
import jax.numpy as jnp
from jax import lax
from jax.experimental import pallas as pl
from jax.experimental.pallas import tpu as pltpu

F32 = jnp.float32
BF16 = jnp.bfloat16

EPS = 1e-6
ROPE_BASE = 10000.0
GRID_W = 64

RET_HEADS = 4
RET_DK = 128
RET_CHUNK = 128
LRU_WIDTH = 512
LRU_BLOCK = 64
LRU_C = 8.0
LRU_HALF = 256
LRU_TILE = 128

MLA_HEADS = 16
MLA_NOPE = 64
MLA_ROPE = 32
MLA_V = 64
MLA_Q_RANK = 384
MLA_KV_RANK = 256
MLA_SCALE = (MLA_NOPE + MLA_ROPE) ** -0.5
HEAD_PAD = 128

MOE_GROUPS = 4
MOE_PER_GROUP = 8
MOE_EXPERTS = 32
MOE_ROWS = 256

ROW_TILE = 256
ATT_TQ = 256
LANES = 128
VMEM_LIMIT = 56 * 1024 * 1024


def _cparams(*sem):
    return pltpu.CompilerParams(dimension_semantics=sem, vmem_limit_bytes=VMEM_LIMIT)


def _rms(x, g):
    return x * lax.rsqrt(jnp.mean(x * x, axis=-1, keepdims=True) + EPS) * g


def _dot(a, b):
    return jnp.dot(a, b, preferred_element_type=F32)


def _dot_nt(a, b):
    return lax.dot_general(a, b, (((1,), (1,)), ((), ())), preferred_element_type=F32)


def _dot_tn(a, b):
    return lax.dot_general(a, b, (((0,), (0,)), ((), ())), preferred_element_type=F32)


def _ada_kernel(s_ref, w_ref, b_ref, o_ref):
    s = jax.nn.silu(s_ref[...])
    o_ref[0] = _dot(s.astype(BF16), w_ref[0].astype(BF16)) + b_ref[0]


def _ada_all(cvec, ada_w, ada_b):
    depth, d, n6 = ada_w.shape
    rows = cvec.shape[0]
    tn = n6 // 4
    return pl.pallas_call(
        _ada_kernel,
        out_shape=jax.ShapeDtypeStruct((depth, rows, n6), F32),
        grid=(depth, n6 // tn),
        in_specs=[pl.BlockSpec((rows, d), lambda l, j: (0, 0)),
                  pl.BlockSpec((1, d, tn), lambda l, j: (l, 0, j)),
                  pl.BlockSpec((1, 1, tn), lambda l, j: (l, 0, j))],
        out_specs=pl.BlockSpec((1, rows, tn), lambda l, j: (l, 0, j)),
        compiler_params=_cparams("arbitrary", "arbitrary"),
        name="adaln",
    )(cvec, ada_w, ada_b.reshape(depth, 1, n6))


class _Rows:
    def __init__(self, b, s, l):
        assert s % ROW_TILE == 0 and l % ROW_TILE == 0
        self.b, self.s, self.l = b, s, l
        self.tpb = s // ROW_TILE
        self.ctx_tiles = l // ROW_TILE
        self.n_tiles = b * self.tpb
        self.rows = b * s

    def mod_idx(self, i):
        return 2 * (i // self.tpb) + jnp.where(i % self.tpb >= self.ctx_tiles, 1, 0)

    def pos_idx(self, i):
        return i % self.tpb


def _combine(x_ref, y0_ref, y1_ref, modp_ref):
    return x_ref[...] + modp_ref[0, 5:6, :] * (y0_ref[...] + y1_ref[...])


def _modulated(x, g_ref, mod_ref, base):
    h = _rms(x, g_ref[...])
    return h * (1.0 + mod_ref[0, base + 1:base + 2, :]) + mod_ref[0, base:base + 1, :]


def _pre_ab_kernel(*refs, has_moe):
    if has_moe:
        x_ref, y0_ref, y1_ref, modp_ref, mod_ref, g_ref, w_ref, xo_ref, p_ref = refs
        x = _combine(x_ref, y0_ref, y1_ref, modp_ref)
        xo_ref[...] = x
    else:
        x_ref, mod_ref, g_ref, w_ref, p_ref = refs
        x = x_ref[...]
    h = _modulated(x, g_ref, mod_ref, 0)
    p_ref[...] = _dot(h.astype(BF16), w_ref[...])


def _row_specs(rt, d, moe):
    specs = [pl.BlockSpec((ROW_TILE, d), lambda i: (i, 0))]
    if moe:
        nt = rt.n_tiles
        specs += [pl.BlockSpec((ROW_TILE, d), lambda i: (i, 0)),
                  pl.BlockSpec((ROW_TILE, d), lambda i: (i + nt, 0)),
                  pl.BlockSpec((1, 8, d), lambda i: (rt.mod_idx(i), 0, 0))]
    return specs


def _pre_ab(rt, x, moe, mods, g1, w_in):
    d = x.shape[1]
    n_out = w_in.shape[1]
    has_moe = moe is not None
    ins = [x] + ([moe[0], moe[0], moe[1]] if has_moe else []) + [mods, g1.reshape(1, d), w_in]
    in_specs = _row_specs(rt, d, has_moe) + [
        pl.BlockSpec((1, 8, d), lambda i: (rt.mod_idx(i), 0, 0)),
        pl.BlockSpec((1, d), lambda i: (0, 0)),
        pl.BlockSpec((d, n_out), lambda i: (0, 0))]
    out_shape = [jax.ShapeDtypeStruct((rt.rows, n_out), F32)]
    out_specs = [pl.BlockSpec((ROW_TILE, n_out), lambda i: (i, 0))]
    if has_moe:
        out_shape = [jax.ShapeDtypeStruct((rt.rows, d), F32)] + out_shape
        out_specs = [pl.BlockSpec((ROW_TILE, d), lambda i: (i, 0))] + out_specs
    res = pl.pallas_call(
        functools.partial(_pre_ab_kernel, has_moe=has_moe),
        out_shape=out_shape, grid=(rt.n_tiles,), in_specs=in_specs, out_specs=out_specs,
        compiler_params=_cparams("arbitrary"), name="pre_ab",
    )(*ins)
    return (res[0], res[1]) if has_moe else (x, res[0])


def _ret_kernel(q_ref, k_ref, v_ref, g_ref, cos_ref, sin_ref, lg_ref, o_ref, qs, ks, acc, *, s_len, l_len):
    c = RET_CHUNK
    nch, cch = s_len // c, l_len // c
    lgf = lg_ref[0, 0:1, :]
    lgb = lg_ref[0, 1:2, :]
    ii = lax.broadcasted_iota(jnp.int32, (c, c), 0).astype(F32)
    jj = lax.broadcasted_iota(jnp.int32, (c, c), 1).astype(F32)
    diff = ii - jj
    dmask = (jnp.where(diff > 0, jnp.exp(lgf * jnp.maximum(diff, 0.0)), 0.0)
             + jnp.where(diff < 0, jnp.exp(lgb * jnp.maximum(-diff, 0.0)), 0.0)
             + jnp.where(diff == 0, 2.0, 0.0))
    zeta_f = jnp.exp(lgf * (c - 1.0 - ii))
    xi_f = jnp.exp(lgf * (ii + 1.0))
    zeta_b = jnp.exp(lgb * ii)
    xi_b = jnp.exp(lgb * (c - ii))
    cd_f = jnp.exp(lgf * c)
    cd_b = jnp.exp(lgb * c)
    k_scale = RET_DK ** -0.5

    def fwd(n, st):
        rows = pl.ds(pl.multiple_of(n * c, c), c)
        cs, sn = cos_ref[rows, :], sin_ref[rows, :]
        q = q_ref[0, rows, :]
        k = k_ref[0, rows, :]
        v = v_ref[0, rows, :]
        qb = (q * cs + pltpu.roll(q, 64, 1) * sn).astype(BF16)
        kb = ((k * cs + pltpu.roll(k, 64, 1) * sn) * k_scale).astype(BF16)
        qs[rows, :] = qb
        ks[rows, :] = kb
        sc = _dot_nt(qb, kb) * dmask
        acc[rows, :] = _dot(sc.astype(BF16), v.astype(BF16)) + _dot(qb, st.astype(BF16)) * xi_f
        return cd_f * st + _dot_tn(kb, (v * zeta_f).astype(BF16))

    def bwd(n, st):
        rows = pl.ds(pl.multiple_of(n * c, c), c)
        qb = qs[rows, :]
        kb = ks[rows, :]
        v = v_ref[0, rows, :]
        y = acc[rows, :] + _dot(qb, st.astype(BF16)) * xi_b
        y = y * lax.rsqrt(jnp.mean(y * y, axis=-1, keepdims=True) + EPS)
        o_ref[0, rows, :] = (y * jax.nn.silu(g_ref[0, rows, :])).astype(o_ref.dtype)
        return cd_b * st + _dot_tn(kb, (v * zeta_b).astype(BF16))

    zero = jnp.zeros((c, c), F32)
    lax.fori_loop(0, nch, fwd, zero)
    st = lax.fori_loop(0, cch, lambda t, st: bwd(cch - 1 - t, st), zero)
    lax.fori_loop(0, nch - cch, lambda t, st: bwd(nch - 1 - t, st), st)


def _retention(p3, cos2, sin2, lgv, l_len):
    b, s, _ = p3.shape
    h = RET_HEADS

    def col(off):
        return pl.BlockSpec((1, s, RET_DK), lambda bi, hi: (bi, 0, off + hi))

    return pl.pallas_call(
        functools.partial(_ret_kernel, s_len=s, l_len=l_len),
        out_shape=jax.ShapeDtypeStruct((b, s, h * RET_DK), BF16),
        grid=(b, h),
        in_specs=[col(0), col(h), col(2 * h), col(3 * h),
                  pl.BlockSpec((s, RET_DK), lambda bi, hi: (0, 0)),
                  pl.BlockSpec((s, RET_DK), lambda bi, hi: (0, 0)),
                  pl.BlockSpec((1, 8, LANES), lambda bi, hi: (hi, 0, 0))],
        out_specs=pl.BlockSpec((1, s, RET_DK), lambda bi, hi: (bi, 0, hi)),
        scratch_shapes=[pltpu.VMEM((s, RET_DK), BF16), pltpu.VMEM((s, RET_DK), BF16),
                        pltpu.VMEM((s, RET_DK), F32)],
        compiler_params=_cparams("arbitrary", "arbitrary"), name="retention",
    )(p3, p3, p3, p3, cos2, sin2, lgv)


def _tile_scan(a, b, reverse):
    n = a.shape[0]
    rows = lax.broadcasted_iota(jnp.int32, a.shape, 0)
    step = 1
    while step < n:
        shift = n - step if reverse else step
        a_s = pltpu.roll(a, shift, 0)
        b_s = pltpu.roll(b, shift, 0)
        m = (rows < n - step) if reverse else (rows >= step)
        b = jnp.where(m, a * b_s + b, b)
        a = jnp.where(m, a * a_s, a)
        step *= 2
    return a, b


def _lru_kernel(x_ref, y_ref, cw_ref, wg_ref, gb_ref, sp_ref, o_ref, xpad, hf, ab, bb, *, s_len, l_len):
    tl, w = LRU_TILE, LRU_HALF
    ntl, ctl = s_len // tl, l_len // tl
    xpad[0:8, :] = jnp.zeros((8, w), F32)
    xpad[s_len + 8:s_len + 16, :] = jnp.zeros((8, w), F32)
    xpad[8:s_len + 8, :] = x_ref[0]
    w0, w1, w2, w3, cb = (cw_ref[0, t:t + 1, :] for t in range(5))
    sp_f = sp_ref[0, 0:1, :]
    sp_b = sp_ref[0, 1:2, :]
    it = lax.broadcasted_iota(jnp.int32, (tl, w), 0)

    def coeff(gr, gi, sp, xc):
        r = jax.nn.sigmoid(gr)
        i = jax.nn.sigmoid(gi)
        log_a = -LRU_C * r * sp
        th = jnp.tanh(log_a)
        return jnp.exp(log_a), jnp.sqrt(-2.0 * th / (1.0 - th)) * (i * xc)

    def fwd(n, carry):
        r0 = pl.multiple_of(n * tl, tl)
        win = xpad[pl.ds(r0, tl + 16), :]
        t = r0 + it
        seg = jnp.where(t >= l_len, 1, 0)

        def tap(d):
            v = pltpu.roll(win, (tl + 16 - d) % (tl + 16), 0)[8:8 + tl]
            return jnp.where(jnp.where(t + d >= l_len, 1, 0) == seg, v, 0.0)

        xc = tap(-2) * w0
        xc = xc + tap(-1) * w1
        xc = xc + win[8:8 + tl] * w2
        xc = xc + tap(1) * w3
        xc = xc + cb
        gts = _dot(xc.astype(BF16), wg_ref[0]) + gb_ref[0]
        a_f, b_f = coeff(gts[:, 0:w], gts[:, w:2 * w], sp_f, xc)
        a_b, b_b = coeff(gts[:, 2 * w:3 * w], gts[:, 3 * w:4 * w], sp_b, xc)
        rows = pl.ds(r0, tl)
        ab[rows, :] = a_b
        bb[rows, :] = b_b
        a_c, h_loc = _tile_scan(a_f, b_f, False)
        hh = h_loc + a_c * carry
        hf[rows, :] = hh
        return hh[tl - 1:tl, :]

    def bwd(n, carry):
        rows = pl.ds(pl.multiple_of(n * tl, tl), tl)
        a_c, h_loc = _tile_scan(ab[rows, :], bb[rows, :], True)
        hh = h_loc + a_c * carry
        o_ref[0, rows, :] = ((hf[rows, :] + hh) * jax.nn.gelu(y_ref[0, rows, :])).astype(o_ref.dtype)
        return hh[0:1, :]

    zero = jnp.zeros((1, w), F32)
    lax.fori_loop(0, ntl, fwd, zero)
    c = lax.fori_loop(0, ctl, lambda t, c: bwd(ctl - 1 - t, c), zero)
    lax.fori_loop(0, ntl - ctl, lambda t, c: bwd(ntl - 1 - t, c), c)


def _rglru(p3, conv_wb, gate_w, gate_b, sp, l_len):
    b, s, _ = p3.shape
    nh = LRU_WIDTH // LRU_HALF
    xoff = (4 * RET_HEADS * RET_DK) // LRU_HALF
    yoff = xoff + nh
    return pl.pallas_call(
        functools.partial(_lru_kernel, s_len=s, l_len=l_len),
        out_shape=jax.ShapeDtypeStruct((b, s, LRU_WIDTH), BF16),
        grid=(b, nh),
        in_specs=[pl.BlockSpec((1, s, LRU_HALF), lambda bi, j: (bi, 0, xoff + j)),
                  pl.BlockSpec((1, s, LRU_HALF), lambda bi, j: (bi, 0, yoff + j)),
                  pl.BlockSpec((1, 8, LRU_HALF), lambda bi, j: (j, 0, 0)),
                  pl.BlockSpec((1, LRU_HALF, 4 * LRU_HALF), lambda bi, j: (j, 0, 0)),
                  pl.BlockSpec((1, 1, 4 * LRU_HALF), lambda bi, j: (j, 0, 0)),
                  pl.BlockSpec((1, 8, LRU_HALF), lambda bi, j: (j, 0, 0))],
        out_specs=pl.BlockSpec((1, s, LRU_HALF), lambda bi, j: (bi, 0, j)),
        scratch_shapes=[pltpu.VMEM((s + 16, LRU_HALF), F32), pltpu.VMEM((s, LRU_HALF), F32),
                        pltpu.VMEM((s, LRU_HALF), F32), pltpu.VMEM((s, LRU_HALF), F32)],
        compiler_params=_cparams("arbitrary", "arbitrary"), name="rglru",
    )(p3, p3, conv_wb, gate_w, gate_b, sp)


def _lru_params(conv_w, conv_b, gate_w, gate_b, lam):
    nh = LRU_WIDTH // LRU_HALF
    bph = LRU_HALF // LRU_BLOCK
    cw = jnp.concatenate([conv_w, conv_b[None, :], jnp.zeros((3, LRU_WIDTH), F32)], axis=0)
    cw = cw.reshape(8, nh, LRU_HALF).transpose(1, 0, 2)
    eye = jnp.eye(bph, dtype=F32)
    gw = gate_w.reshape(2, 2, nh, bph, LRU_BLOCK, LRU_BLOCK)
    dense = jnp.einsum('dgjkio,kl->jkidglo', gw, eye)
    dense = dense.reshape(nh, LRU_HALF, 4 * LRU_HALF).astype(BF16)
    gb = gate_b.reshape(2, 2, nh, LRU_HALF).transpose(2, 0, 1, 3).reshape(nh, 1, 4 * LRU_HALF)
    sp = jax.nn.softplus(-lam.astype(F32)).reshape(2, nh, LRU_HALF).transpose(1, 0, 2)
    sp = jnp.concatenate([sp, jnp.zeros((nh, 6, LRU_HALF), F32)], axis=1)
    return cw, dense, gb, sp


def _post_kernel(x_ref, ma_ref, mb_ref, w_ref, mod_ref, g_ref, wr_ref, br_ref, xo_ref, h_ref, r_ref):
    m = jnp.concatenate([ma_ref[...], mb_ref[...]], axis=1)
    o = _dot(m, w_ref[...])
    x = x_ref[...] + mod_ref[0, 2:3, :] * o
    xo_ref[...] = x
    h = _modulated(x, g_ref, mod_ref, 3)
    h_ref[...] = h
    lg = jnp.dot(h, wr_ref[...], preferred_element_type=F32, precision=lax.Precision.HIGHEST) + br_ref[...]
    lane = lax.broadcasted_iota(jnp.int32, lg.shape, 1)
    lanef = lane.astype(F32)
    ninf = -jnp.inf
    big = float(LANES)
    gl = jnp.where(lane < MOE_GROUPS, lg, ninf)
    gmax = jnp.max(gl, axis=1, keepdims=True)
    g_top = 1.0 / jnp.sum(jnp.exp(gl - gmax), axis=1, keepdims=True)
    g_sel = jnp.min(jnp.where(gl == gmax, lanef, big), axis=1, keepdims=True)
    lo = MOE_GROUPS + MOE_PER_GROUP * g_sel
    el = jnp.where((lanef >= lo) & (lanef < lo + MOE_PER_GROUP), lg, ninf)
    emax = jnp.max(el, axis=1, keepdims=True)
    esum = jnp.sum(jnp.exp(el - emax), axis=1, keepdims=True)
    i1 = jnp.min(jnp.where(el == emax, lanef, big), axis=1, keepdims=True)
    el2 = jnp.where(lanef == i1, ninf, el)
    m2 = jnp.max(el2, axis=1, keepdims=True)
    i2 = jnp.min(jnp.where(el2 == m2, lanef, big), axis=1, keepdims=True)
    p1 = 1.0 / esum
    p2 = jnp.exp(m2 - emax) / esum
    tot = p1 + p2
    w1 = g_top * (p1 / tot)
    w2 = g_top * (p2 / tot)
    r_ref[...] = jnp.where(lane == 0, i1 - MOE_GROUPS,
                           jnp.where(lane == 1, i2 - MOE_GROUPS,
                                     jnp.where(lane == 2, w1, jnp.where(lane == 3, w2, 0.0))))


def _post(rt, x, ma, mb, cb, w_out, mods, g2, wr, br):
    d = x.shape[1]
    hd = d // 2
    return pl.pallas_call(
        _post_kernel,
        out_shape=[jax.ShapeDtypeStruct((rt.rows, d), F32), jax.ShapeDtypeStruct((rt.rows, d), F32),
                   jax.ShapeDtypeStruct((rt.rows, LANES), F32)],
        grid=(rt.n_tiles,),
        in_specs=[pl.BlockSpec((ROW_TILE, d), lambda i: (i, 0)),
                  pl.BlockSpec((ROW_TILE, hd), lambda i: (i, 0)),
                  pl.BlockSpec((ROW_TILE, hd), lambda i: (i, cb)),
                  pl.BlockSpec((d, d), lambda i: (0, 0)),
                  pl.BlockSpec((1, 8, d), lambda i: (rt.mod_idx(i), 0, 0)),
                  pl.BlockSpec((1, d), lambda i: (0, 0)),
                  pl.BlockSpec((d, LANES), lambda i: (0, 0)),
                  pl.BlockSpec((1, LANES), lambda i: (0, 0))],
        out_specs=[pl.BlockSpec((ROW_TILE, d), lambda i: (i, 0)),
                   pl.BlockSpec((ROW_TILE, d), lambda i: (i, 0)),
                   pl.BlockSpec((ROW_TILE, LANES), lambda i: (i, 0))],
        compiler_params=_cparams("arbitrary"), name="post",
    )(x, ma, mb, w_out, mods, g2.reshape(1, d), wr, br)


def _moe_kernel(be_ref, nv_ref, src_ref, dst_ref, h_hbm, rw_ref, wg_ref, wu_ref, wd_ref, out_hbm,
                xs, ys, sem_in, sem_out, *, nb):
    del be_ref
    mb = MOE_ROWS
    i = pl.program_id(0)
    slot = i % 2

    def issue_gather(blk, sl):
        def body(j, c):
            pltpu.make_async_copy(h_hbm.at[pl.ds(src_ref[blk * mb + j], 1)], xs.at[sl, pl.ds(j, 1)],
                                  sem_in.at[sl]).start()
            return c
        lax.fori_loop(0, nv_ref[blk], body, 0)

    def wait_rows(blk, sl, gather):
        def wait(n):
            if gather:
                pltpu.make_async_copy(h_hbm.at[pl.ds(0, n)], xs.at[sl, pl.ds(0, n)], sem_in.at[sl]).wait()
            else:
                pltpu.make_async_copy(ys.at[sl, pl.ds(0, n)], out_hbm.at[pl.ds(0, n)], sem_out.at[sl]).wait()

        n = nv_ref[blk]
        n8 = pl.multiple_of(lax.shift_left(lax.shift_right_logical(n, 3), 3), 8)

        @pl.when(n8 > 0)
        def _():
            wait(n8)

        def tail(j, c):
            wait(1)
            return c
        lax.fori_loop(0, n - n8, tail, 0)

    @pl.when(i == 0)
    def _():
        xs[...] = jnp.zeros(xs.shape, xs.dtype)
        issue_gather(0, 0)

    @pl.when(i + 1 < nb)
    def _():
        issue_gather(i + 1, 1 - slot)

    wait_rows(i, slot, True)

    @pl.when(i >= 2)
    def _():
        wait_rows(i - 2, slot, False)

    @pl.when(nv_ref[i] > 0)
    def _():
        x = xs[slot].astype(BF16)
        hg = _dot(x, wg_ref[0])
        hu = _dot(x, wu_ref[0])
        a = (jax.nn.silu(hg) * hu).astype(BF16)
        ys[slot] = _dot(a, wd_ref[0]) * rw_ref[...]

        def sbody(j, c):
            pltpu.make_async_copy(ys.at[slot, pl.ds(j, 1)], out_hbm.at[pl.ds(dst_ref[i * mb + j], 1)],
                                  sem_out.at[slot]).start()
            return c
        lax.fori_loop(0, nv_ref[i], sbody, 0)

    @pl.when(i == nb - 1)
    def _():
        wait_rows(i, slot, False)
        if nb > 1:
            wait_rows(i - 1, 1 - slot, False)


def _moe(h, route, wg, wu, wd):
    t_count, d = h.shape
    mb = MOE_ROWS
    n_assign = 2 * t_count
    nb = (n_assign + MOE_EXPERTS * (mb - 1) + mb - 1) // mb
    n_rows = nb * mb
    e_flat = route[:, 0:2].astype(jnp.int32).reshape(-1)
    w_flat = route[:, 2:4].reshape(-1)
    onehot = (e_flat[:, None] == jnp.arange(MOE_EXPERTS, dtype=jnp.int32)[None, :]).astype(jnp.int32)
    csum = jnp.cumsum(onehot, axis=0)
    rank = jnp.take_along_axis(csum, e_flat[:, None], axis=1)[:, 0] - 1
    counts = csum[-1]
    padded = (counts + mb - 1) // mb * mb
    pend = jnp.cumsum(padded)
    pstart = pend - padded
    dest = pstart[e_flat] + rank
    a_idx = jnp.arange(n_assign, dtype=jnp.int32)
    tok = a_idx // 2
    row_src = jnp.zeros((n_rows,), jnp.int32).at[dest].set(tok)
    row_dst = jnp.zeros((n_rows,), jnp.int32).at[dest].set((a_idx % 2) * t_count + tok)
    row_w = jnp.zeros((n_rows,), F32).at[dest].set(w_flat)
    blk0 = jnp.arange(nb, dtype=jnp.int32) * mb
    block_e = jnp.minimum(jnp.searchsorted(pend, blk0, side='right'), MOE_EXPERTS - 1).astype(jnp.int32)
    n_valid = jnp.clip(counts[block_e] - (blk0 - pstart[block_e]), 0, mb).astype(jnp.int32)
    hid = wg.shape[2]

    def wspec(shape):
        return pl.BlockSpec(shape, lambda i, be, nv, s_, d_: (be[i], 0, 0))

    grid_spec = pltpu.PrefetchScalarGridSpec(
        num_scalar_prefetch=4, grid=(nb,),
        in_specs=[pl.BlockSpec(memory_space=pl.ANY),
                  pl.BlockSpec((mb, 1), lambda i, be, nv, s_, d_: (i, 0)),
                  wspec((1, d, hid)), wspec((1, d, hid)), wspec((1, hid, d))],
        out_specs=pl.BlockSpec(memory_space=pl.ANY),
        scratch_shapes=[pltpu.VMEM((2, mb, d), F32), pltpu.VMEM((2, mb, d), F32),
                        pltpu.SemaphoreType.DMA((2,)), pltpu.SemaphoreType.DMA((2,))])
    return pl.pallas_call(
        functools.partial(_moe_kernel, nb=nb),
        out_shape=jax.ShapeDtypeStruct((n_assign, d), F32),
        grid_spec=grid_spec,
        compiler_params=_cparams("arbitrary"), name="moe",
    )(block_e, n_valid, row_src, row_dst, h, row_w.reshape(n_rows, 1), wg, wu, wd)


def _pre_mla_kernel(x_ref, y0_ref, y1_ref, modp_ref, mod_ref, g_ref, win_ref, qg_ref, kvg_ref,
                    wq_ref, wqs_ref, wk_ref, wv_ref, ct_ref, st_ref, xo_ref, q_ref, k_ref, v_ref):
    x = _combine(x_ref, y0_ref, y1_ref, modp_ref)
    xo_ref[...] = x
    h = _modulated(x, g_ref, mod_ref, 0)
    p = _dot(h.astype(BF16), win_ref[...])
    cq = _rms(p[:, 0:MLA_Q_RANK], qg_ref[...]).astype(BF16)
    ckv = _rms(p[:, MLA_Q_RANK:MLA_Q_RANK + MLA_KV_RANK], kvg_ref[...]).astype(BF16)
    off = MLA_Q_RANK + MLA_KV_RANK
    ct, st = ct_ref[...], st_ref[...]
    k_rope = p[:, off:off + HEAD_PAD] * ct + p[:, off + HEAD_PAD:off + 2 * HEAD_PAD] * st
    qa = _dot(cq, wq_ref[...])
    qb = _dot(cq, wqs_ref[...])
    kn = _dot(ckv, wk_ref[...])
    v_ref[...] = _dot(ckv, wv_ref[...]).astype(v_ref.dtype)
    for hh in range(MLA_HEADS):
        sl = slice(hh * HEAD_PAD, (hh + 1) * HEAD_PAD)
        q_ref[:, sl] = ((qa[:, sl] * ct + qb[:, sl] * st) * MLA_SCALE).astype(q_ref.dtype)
        k_ref[:, sl] = (kn[:, sl] + k_rope).astype(k_ref.dtype)


def _pre_mla(rt, x, moe, mods, g1, wts, ct, st):
    d = x.shape[1]
    w_in, qg, kvg, wq, wqs, wk, wv = wts

    def full(a):
        return pl.BlockSpec(a.shape, lambda i: (0,) * a.ndim)

    hq = MLA_HEADS * HEAD_PAD
    hv = MLA_HEADS * MLA_V
    return pl.pallas_call(
        _pre_mla_kernel,
        out_shape=[jax.ShapeDtypeStruct((rt.rows, d), F32), jax.ShapeDtypeStruct((rt.rows, hq), BF16),
                   jax.ShapeDtypeStruct((rt.rows, hq), BF16), jax.ShapeDtypeStruct((rt.rows, hv), BF16)],
        grid=(rt.n_tiles,),
        in_specs=_row_specs(rt, d, True) + [
            pl.BlockSpec((1, 8, d), lambda i: (rt.mod_idx(i), 0, 0)),
            pl.BlockSpec((1, d), lambda i: (0, 0)),
            full(w_in), full(qg), full(kvg), full(wq), full(wqs), full(wk), full(wv),
            pl.BlockSpec((ROW_TILE, HEAD_PAD), lambda i: (rt.pos_idx(i), 0)),
            pl.BlockSpec((ROW_TILE, HEAD_PAD), lambda i: (rt.pos_idx(i), 0))],
        out_specs=[pl.BlockSpec((ROW_TILE, d), lambda i: (i, 0)),
                   pl.BlockSpec((ROW_TILE, hq), lambda i: (i, 0)),
                   pl.BlockSpec((ROW_TILE, hq), lambda i: (i, 0)),
                   pl.BlockSpec((ROW_TILE, hv), lambda i: (i, 0))],
        compiler_params=_cparams("arbitrary"), name="pre_mla",
    )(x, moe[0], moe[0], moe[1], mods, g1.reshape(1, d), w_in, qg, kvg, wq, wqs, wk, wv, ct, st)


def _mla_params(w_in, q_g, kv_g, w_uq, w_ukv):
    d = w_in.shape[0]
    hp, hr = HEAD_PAD, MLA_ROPE // 2
    nq = MLA_NOPE + MLA_ROPE
    kr = w_in[:, MLA_Q_RANK + MLA_KV_RANK:]
    z = jnp.zeros((d, MLA_NOPE), F32)
    zt = jnp.zeros((d, hp - nq), F32)
    kr_a = jnp.concatenate([z, kr, zt], axis=1)
    kr_b = jnp.concatenate([z, -kr[:, hr:], kr[:, :hr], zt], axis=1)
    w_in_p = jnp.concatenate([w_in[:, :MLA_Q_RANK + MLA_KV_RANK], kr_a, kr_b], axis=1).astype(BF16)
    wq = w_uq.reshape(MLA_Q_RANK, MLA_HEADS, nq)
    zq = jnp.zeros((MLA_Q_RANK, MLA_HEADS, hp - nq), F32)
    wq_a = jnp.concatenate([wq, zq], axis=2).reshape(MLA_Q_RANK, MLA_HEADS * hp).astype(BF16)
    wq_b = jnp.concatenate([jnp.zeros_like(wq[:, :, :MLA_NOPE]), -wq[:, :, MLA_NOPE + hr:],
                            wq[:, :, MLA_NOPE:MLA_NOPE + hr], zq], axis=2)
    wq_b = wq_b.reshape(MLA_Q_RANK, MLA_HEADS * hp).astype(BF16)
    wkv = w_ukv.reshape(MLA_KV_RANK, MLA_HEADS, MLA_NOPE + MLA_V)
    wk = jnp.concatenate([wkv[:, :, :MLA_NOPE], jnp.zeros((MLA_KV_RANK, MLA_HEADS, hp - MLA_NOPE), F32)], axis=2)
    wk = wk.reshape(MLA_KV_RANK, MLA_HEADS * hp).astype(BF16)
    wv = wkv[:, :, MLA_NOPE:].reshape(MLA_KV_RANK, MLA_HEADS * MLA_V).astype(BF16)
    return (w_in_p, q_g.reshape(1, -1), kv_g.reshape(1, -1), wq_a, wq_b, wk, wv)


def _attn_kernel(q_ref, k_ref, v_ref, o_ref, *, s_len, l_len):
    t = pl.program_id(2)
    lane = lax.broadcasted_iota(jnp.int32, (ATT_TQ, 2 * MLA_V), 1)

    def attend(nk):
        outs = []
        vv = v_ref[0, 0:nk, :]
        for j in range(2):
            q = q_ref[0, :, j * HEAD_PAD:(j + 1) * HEAD_PAD]
            k = k_ref[0, 0:nk, j * HEAD_PAD:(j + 1) * HEAD_PAD]
            s = _dot_nt(q, k)
            p = jnp.exp(s - jnp.max(s, axis=1, keepdims=True))
            den = jnp.sum(p, axis=1, keepdims=True)
            outs.append(_dot(p.astype(BF16), vv) / den)
        o_ref[0] = jnp.where(lane < MLA_V, outs[0], outs[1]).astype(o_ref.dtype)

    ctx_tiles = l_len // ATT_TQ

    @pl.when(t < ctx_tiles)
    def _():
        attend(l_len)

    @pl.when(t >= ctx_tiles)
    def _():
        attend(s_len)


def _attention(q3, k3, v3, l_len):
    b, s, _ = q3.shape
    hp2 = 2 * HEAD_PAD
    return pl.pallas_call(
        functools.partial(_attn_kernel, s_len=s, l_len=l_len),
        out_shape=jax.ShapeDtypeStruct((b, s, MLA_HEADS * MLA_V), BF16),
        grid=(b, MLA_HEADS // 2, s // ATT_TQ),
        in_specs=[pl.BlockSpec((1, ATT_TQ, hp2), lambda bi, hi, ti: (bi, ti, hi)),
                  pl.BlockSpec((1, s, hp2), lambda bi, hi, ti: (bi, 0, hi)),
                  pl.BlockSpec((1, s, 2 * MLA_V), lambda bi, hi, ti: (bi, 0, hi))],
        out_specs=pl.BlockSpec((1, ATT_TQ, 2 * MLA_V), lambda bi, hi, ti: (bi, ti, hi)),
        compiler_params=_cparams("arbitrary", "arbitrary", "arbitrary"), name="attention",
    )(q3, k3, v3)


def _final_kernel(x_ref, y0_ref, y1_ref, modp_ref, g_ref, o_ref):
    o_ref[...] = _rms(_combine(x_ref, y0_ref, y1_ref, modp_ref), g_ref[...])


def _final(rt, x, moe, g, n_len):
    d = x.shape[1]
    lt = n_len // ROW_TILE
    nt = rt.n_tiles

    def row(i):
        return (i // lt) * rt.tpb + rt.ctx_tiles + i % lt

    return pl.pallas_call(
        _final_kernel,
        out_shape=jax.ShapeDtypeStruct((rt.b * n_len, d), F32),
        grid=(rt.b * lt,),
        in_specs=[pl.BlockSpec((ROW_TILE, d), lambda i: (row(i), 0)),
                  pl.BlockSpec((ROW_TILE, d), lambda i: (row(i), 0)),
                  pl.BlockSpec((ROW_TILE, d), lambda i: (row(i) + nt, 0)),
                  pl.BlockSpec((1, 8, d), lambda i: (2 * (i // lt) + 1, 0, 0)),
                  pl.BlockSpec((1, d), lambda i: (0, 0))],
        out_specs=pl.BlockSpec((ROW_TILE, d), lambda i: (i, 0)),
        compiler_params=_cparams("arbitrary"), name="final_norm",
    )(x, moe[0], moe[0], moe[1], g.reshape(1, d))


def _ret_tables(n, l):
    t = jnp.arange(n, dtype=F32)
    inv = ROPE_BASE ** (-jnp.arange(0, RET_DK, 2, dtype=F32) / RET_DK)
    ang = t[:, None] * inv[None, :]
    cos, sin = jnp.cos(ang), jnp.sin(ang)
    cos2 = jnp.concatenate([jnp.ones((l, RET_DK), F32), jnp.concatenate([cos, cos], axis=1)], axis=0)
    sin2 = jnp.concatenate([jnp.zeros((l, RET_DK), F32), jnp.concatenate([-sin, sin], axis=1)], axis=0)
    return cos2, sin2


def _mla_tables(n, l):
    rows = n // GRID_W
    r_pos = jnp.repeat(jnp.arange(rows, dtype=F32), GRID_W)
    c_pos = jnp.tile(jnp.arange(GRID_W, dtype=F32), rows)
    ax = MLA_ROPE // 2
    inv = ROPE_BASE ** (-jnp.arange(0, ax, 2, dtype=F32) / ax)
    ang = jnp.concatenate([r_pos[:, None] * inv[None, :], c_pos[:, None] * inv[None, :]], axis=-1)
    cos, sin = jnp.cos(ang), jnp.sin(ang)
    pad = HEAD_PAD - MLA_NOPE - MLA_ROPE
    ct_l = jnp.concatenate([jnp.ones((n, MLA_NOPE), F32), cos, cos, jnp.zeros((n, pad), F32)], axis=1)
    st_l = jnp.concatenate([jnp.zeros((n, MLA_NOPE), F32), sin, sin, jnp.zeros((n, pad), F32)], axis=1)
    ct_c = jnp.concatenate([jnp.ones((l, MLA_NOPE + MLA_ROPE), F32), jnp.zeros((l, pad), F32)], axis=1)
    return jnp.concatenate([ct_c, ct_l], axis=0), jnp.concatenate([jnp.zeros((l, HEAD_PAD), F32), st_l], axis=0)


def kernel(x, c, ctx, c_ctx, ada_w, ada_b, norm_g, ab_w_in, ab_w_out, ret_decay_logit, lru_conv_w, lru_conv_b, lru_gate_w, lru_gate_b, lru_lambda, mla_w_in, mla_q_norm_g, mla_kv_norm_g, mla_w_uq, mla_w_ukv, mla_w_out, moe_group_w, moe_group_b, moe_expert_w, moe_expert_b, moe_w_gate, moe_w_up, moe_w_down, final_norm_g):
    b, n, d = x.shape
    l = ctx.shape[1]
    s = l + n
    depth = ada_w.shape[0]
    rt = _Rows(b, s, l)

    nrow = (b + 1 + 7) // 8 * 8
    cvec = jnp.concatenate([c, c_ctx[None, :], jnp.zeros((nrow - b - 1, d), F32)], axis=0)
    ada = _ada_all(cvec, ada_w, ada_b)

    def layer_mods(layer):
        lat = ada[layer, :b].reshape(b, 1, 6, d)
        cx = jnp.broadcast_to(ada[layer, b].reshape(1, 1, 6, d), (b, 1, 6, d))
        m = jnp.concatenate([cx, lat], axis=1)
        m = jnp.concatenate([m, jnp.zeros((b, 2, 2, d), F32)], axis=2)
        return m.reshape(2 * b, 8, d)

    cos2, sin2 = _ret_tables(n, l)
    ct, st = _mla_tables(n, l)

    xs = jnp.concatenate([ctx, x], axis=1).reshape(b * s, d)
    moe = None
    mods = None
    for layer in range(depth):
        prev = (moe, mods) if moe is not None else None
        mods = layer_mods(layer)
        i = layer // 2
        if layer % 2 == 0:
            xs, p = _pre_ab(rt, xs, prev, mods, norm_g[layer, 0], ab_w_in[i].astype(BF16))
            p3 = p.reshape(b, s, -1)
            lg = jax.nn.log_sigmoid(ret_decay_logit[i].astype(F32))
            lgv = jnp.broadcast_to(lg.T[:, :, None], (RET_HEADS, 2, LANES))
            lgv = jnp.concatenate([lgv, jnp.zeros((RET_HEADS, 6, LANES), F32)], axis=1)
            ma = _retention(p3, cos2, sin2, lgv, l).reshape(b * s, -1)
            mb = _rglru(p3, *_lru_params(lru_conv_w[i], lru_conv_b[i], lru_gate_w[i], lru_gate_b[i],
                                         lru_lambda[i]), l).reshape(b * s, -1)
            cb = 0
            w_out = ab_w_out[i].astype(BF16)
        else:
            wts = _mla_params(mla_w_in[i], mla_q_norm_g[i], mla_kv_norm_g[i], mla_w_uq[i], mla_w_ukv[i])
            xs, q, k, v = _pre_mla(rt, xs, prev, mods, norm_g[layer, 0], wts, ct, st)
            att = _attention(q.reshape(b, s, -1), k.reshape(b, s, -1), v.reshape(b, s, -1), l)
            ma = mb = att.reshape(b * s, -1)
            cb = 1
            w_out = mla_w_out[i].astype(BF16)
        wr = jnp.concatenate([moe_group_w[layer], moe_expert_w[layer],
                              jnp.zeros((d, LANES - MOE_GROUPS - MOE_EXPERTS), F32)], axis=1)
        br = jnp.concatenate([moe_group_b[layer], moe_expert_b[layer],
                              jnp.zeros((LANES - MOE_GROUPS - MOE_EXPERTS,), F32)]).reshape(1, LANES)
        xs, h2, route = _post(rt, xs, ma, mb, cb, w_out, mods, norm_g[layer, 1], wr, br)
        moe = _moe(h2, route, moe_w_gate[layer].astype(BF16), moe_w_up[layer].astype(BF16),
                   moe_w_down[layer].astype(BF16))
    out = _final(rt, xs, (moe, mods), final_norm_g, n)
    return out.reshape(b, n, d)
```

```python
import functools

import jax
import jax.numpy as jnp
from jax import lax
from jax.experimental import pallas as pl
from jax.experimental.pallas import tpu as pltpu

F32 = jnp.float32
BF16 = jnp.bfloat16

EPS = 1e-6
ROPE_BASE = 10000.0
GRID_W = 64

RET_HEADS = 4
RET_DK = 128
RET_CHUNK = 128
LRU_WIDTH = 512
LRU_BLOCK = 64
LRU_C = 8.0
LRU_HALF = 256
LRU_TILE = 128

MLA_HEADS = 16
MLA_NOPE = 64
MLA_ROPE = 32
MLA_V = 64
MLA_Q_RANK = 384
MLA_KV_RANK = 256
MLA_SCALE = (MLA_NOPE + MLA_ROPE) ** -0.5
HEAD_PAD = 128

MOE_GROUPS = 4
MOE_PER_GROUP = 8
MOE_EXPERTS = 32
MOE_ROWS = 256

ROW_TILE = 256
ATT_TQ = 256
LANES = 128
VMEM_LIMIT = 56 * 1024 * 1024


def _cparams(*sem):
    return pltpu.CompilerParams(dimension_semantics=sem, vmem_limit_bytes=VMEM_LIMIT)


def _rms(x, g):
    return x * lax.rsqrt(jnp.mean(x * x, axis=-1, keepdims=True) + EPS) * g


def _dot(a, b):
    return jnp.dot(a, b, preferred_element_type=F32)


def _dot_nt(a, b):
    return lax.dot_general(a, b, (((1,), (1,)), ((), ())), preferred_element_type=F32)


def _dot_tn(a, b):
    return lax.dot_general(a, b, (((0,), (0,)), ((), ())), preferred_element_type=F32)


def _ada_kernel(s_ref, w_ref, b_ref, o_ref):
    s = jax.nn.silu(s_ref[...])
    o_ref[0] = _dot(s.astype(BF16), w_ref[0].astype(BF16)) + b_ref[0]


def _ada_all(cvec, ada_w, ada_b):
    depth, d, n6 = ada_w.shape
    rows = cvec.shape[0]
    tn = n6 // 4
    return pl.pallas_call(
        _ada_kernel,
        out_shape=jax.ShapeDtypeStruct((depth, rows, n6), F32),
        grid=(depth, n6 // tn),
        in_specs=[pl.BlockSpec((rows, d), lambda l, j: (0, 0)),
                  pl.BlockSpec((1, d, tn), lambda l, j: (l, 0, j)),
                  pl.BlockSpec((1, 1, tn), lambda l, j: (l, 0, j))],
        out_specs=pl.BlockSpec((1, rows, tn), lambda l, j: (l, 0, j)),
        compiler_params=_cparams("arbitrary", "arbitrary"),
        name="adaln",
    )(cvec, ada_w, ada_b.reshape(depth, 1, n6))


class _Rows:
    def __init__(self, b, s, l):
        assert s % ROW_TILE == 0 and l % ROW_TILE == 0
        self.b, self.s, self.l = b, s, l
        self.tpb = s // ROW_TILE
        self.ctx_tiles = l // ROW_TILE
        self.n_tiles = b * self.tpb
        self.rows = b * s

    def mod_idx(self, i):
        return 2 * (i // self.tpb) + jnp.where(i % self.tpb >= self.ctx_tiles, 1, 0)

    def pos_idx(self, i):
        return i % self.tpb


def _modulated(x, g_ref, mod_ref, base):
    h = _rms(x, g_ref[...])
    return h * (1.0 + mod_ref[0, base + 1:base + 2, :]) + mod_ref[0, base:base + 1, :]


def _pre_ab_kernel(x_ref, mod_ref, g_ref, w_ref, p_ref):
    h = _modulated(x_ref[...], g_ref, mod_ref, 0)
    p_ref[...] = _dot(h.astype(BF16), w_ref[...])


def _pre_ab(rt, x, mods, g1, w_in):
    d = x.shape[1]
    n_out = w_in.shape[1]
    return pl.pallas_call(
        _pre_ab_kernel,
        out_shape=jax.ShapeDtypeStruct((rt.rows, n_out), F32),
        grid=(rt.n_tiles,),
        in_specs=[pl.BlockSpec((ROW_TILE, d), lambda i: (i, 0)),
                  pl.BlockSpec((1, 8, d), lambda i: (rt.mod_idx(i), 0, 0)),
                  pl.BlockSpec((1, d), lambda i: (0, 0)),
                  pl.BlockSpec((d, n_out), lambda i: (0, 0))],
        out_specs=pl.BlockSpec((ROW_TILE, n_out), lambda i: (i, 0)),
        compiler_params=_cparams("arbitrary"), name="pre_ab",
    )(x, mods, g1.reshape(1, d), w_in)


def _ret_kernel(q_ref, k_ref, v_ref, g_ref, cos_ref, sin_ref, lg_ref, o_ref, qs, ks, acc, *, s_len, l_len):
    c = RET_CHUNK
    nch, cch = s_len // c, l_len // c
    lgf = lg_ref[0, 0:1, :]
    lgb = lg_ref[0, 1:2, :]
    ii = lax.broadcasted_iota(jnp.int32, (c, c), 0).astype(F32)
    jj = lax.broadcasted_iota(jnp.int32, (c, c), 1).astype(F32)
    diff = ii - jj
    dmask = (jnp.where(diff > 0, jnp.exp(lgf * jnp.maximum(diff, 0.0)), 0.0)
             + jnp.where(diff < 0, jnp.exp(lgb * jnp.maximum(-diff, 0.0)), 0.0)
             + jnp.where(diff == 0, 2.0, 0.0))
    zeta_f = jnp.exp(lgf * (c - 1.0 - ii))
    xi_f = jnp.exp(lgf * (ii + 1.0))
    zeta_b = jnp.exp(lgb * ii)
    xi_b = jnp.exp(lgb * (c - ii))
    cd_f = jnp.exp(lgf * c)
    cd_b = jnp.exp(lgb * c)
    k_scale = RET_DK ** -0.5

    def fwd(n, st):
        rows = pl.ds(pl.multiple_of(n * c, c), c)
        cs, sn = cos_ref[rows, :], sin_ref[rows, :]
        q = q_ref[0, rows, :]
        k = k_ref[0, rows, :]
        v = v_ref[0, rows, :]
        qb = (q * cs + pltpu.roll(q, 64, 1) * sn).astype(BF16)
        kb = ((k * cs + pltpu.roll(k, 64, 1) * sn) * k_scale).astype(BF16)
        qs[rows, :] = qb
        ks[rows, :] = kb
        sc = _dot_nt(qb, kb) * dmask
        acc[rows, :] = _dot(sc.astype(BF16), v.astype(BF16)) + _dot(qb, st.astype(BF16)) * xi_f
        return cd_f * st + _dot_tn(kb, (v * zeta_f).astype(BF16))

    def bwd(n, st):
        rows = pl.ds(pl.multiple_of(n * c, c), c)
        qb = qs[rows, :]
        kb = ks[rows, :]
        v = v_ref[0, rows, :]
        y = acc[rows, :] + _dot(qb, st.astype(BF16)) * xi_b
        y = y * lax.rsqrt(jnp.mean(y * y, axis=-1, keepdims=True) + EPS)
        o_ref[0, rows, :] = (y * jax.nn.silu(g_ref[0, rows, :])).astype(o_ref.dtype)
        return cd_b * st + _dot_tn(kb, (v * zeta_b).astype(BF16))

    zero = jnp.zeros((c, c), F32)
    lax.fori_loop(0, nch, fwd, zero)
    st = lax.fori_loop(0, cch, lambda t, st: bwd(cch - 1 - t, st), zero)
    lax.fori_loop(0, nch - cch, lambda t, st: bwd(nch - 1 - t, st), st)


def _retention(p3, cos2, sin2, lgv, l_len):
    b, s, _ = p3.shape
    h = RET_HEADS

    def col(off):
        return pl.BlockSpec((1, s, RET_DK), lambda bi, hi: (bi, 0, off + hi))

    return pl.pallas_call(
        functools.partial(_ret_kernel, s_len=s, l_len=l_len),
        out_shape=jax.ShapeDtypeStruct((b, s, h * RET_DK), BF16),
        grid=(b, h),
        in_specs=[col(0), col(h), col(2 * h), col(3 * h),
                  pl.BlockSpec((s, RET_DK), lambda bi, hi: (0, 0)),
                  pl.BlockSpec((s, RET_DK), lambda bi, hi: (0, 0)),
                  pl.BlockSpec((1, 8, LANES), lambda bi, hi: (hi, 0, 0))],
        out_specs=pl.BlockSpec((1, s, RET_DK), lambda bi, hi: (bi, 0, hi)),
        scratch_shapes=[pltpu.VMEM((s, RET_DK), BF16), pltpu.VMEM((s, RET_DK), BF16),
                        pltpu.VMEM((s, RET_DK), F32)],
        compiler_params=_cparams("arbitrary", "arbitrary"), name="retention",
    )(p3, p3, p3, p3, cos2, sin2, lgv)


def _tile_scan(a, b, reverse):
    n = a.shape[0]
    rows = lax.broadcasted_iota(jnp.int32, a.shape, 0)
    step = 1
    while step < n:
        shift = n - step if reverse else step
        a_s = pltpu.roll(a, shift, 0)
        b_s = pltpu.roll(b, shift, 0)
        m = (rows < n - step) if reverse else (rows >= step)
        b = jnp.where(m, a * b_s + b, b)
        a = jnp.where(m, a * a_s, a)
        step *= 2
    return a, b


def _lru_kernel(x_ref, y_ref, cw_ref, wg_ref, gb_ref, sp_ref, o_ref, xpad, hf, ab, bb, *, s_len, l_len):
    tl, w = LRU_TILE, LRU_HALF
    ntl, ctl = s_len // tl, l_len // tl
    xpad[0:8, :] = jnp.zeros((8, w), F32)
    xpad[s_len + 8:s_len + 16, :] = jnp.zeros((8, w), F32)
    xpad[8:s_len + 8, :] = x_ref[0]
    w0, w1, w2, w3, cb = (cw_ref[0, t:t + 1, :] for t in range(5))
    sp_f = sp_ref[0, 0:1, :]
    sp_b = sp_ref[0, 1:2, :]
    it = lax.broadcasted_iota(jnp.int32, (tl, w), 0)

    def coeff(gr, gi, sp, xc):
        r = jax.nn.sigmoid(gr)
        i = jax.nn.sigmoid(gi)
        log_a = -LRU_C * r * sp
        th = jnp.tanh(log_a)
        return jnp.exp(log_a), jnp.sqrt(-2.0 * th / (1.0 - th)) * (i * xc)

    def fwd(n, carry):
        r0 = pl.multiple_of(n * tl, tl)
        win = xpad[pl.ds(r0, tl + 16), :]
        t = r0 + it
        seg = jnp.where(t >= l_len, 1, 0)

        def tap(d):
            v = pltpu.roll(win, (tl + 16 - d) % (tl + 16), 0)[8:8 + tl]
            return jnp.where(jnp.where(t + d >= l_len, 1, 0) == seg, v, 0.0)

        xc = tap(-2) * w0
        xc = xc + tap(-1) * w1
        xc = xc + win[8:8 + tl] * w2
        xc = xc + tap(1) * w3
        xc = xc + cb
        gts = _dot(xc.astype(BF16), wg_ref[0]) + gb_ref[0]
        a_f, b_f = coeff(gts[:, 0:w], gts[:, w:2 * w], sp_f, xc)
        a_b, b_b = coeff(gts[:, 2 * w:3 * w], gts[:, 3 * w:4 * w], sp_b, xc)
        rows = pl.ds(r0, tl)
        ab[rows, :] = a_b
        bb[rows, :] = b_b
        a_c, h_loc = _tile_scan(a_f, b_f, False)
        hh = h_loc + a_c * carry
        hf[rows, :] = hh
        return hh[tl - 1:tl, :]

    def bwd(n, carry):
        rows = pl.ds(pl.multiple_of(n * tl, tl), tl)
        a_c, h_loc = _tile_scan(ab[rows, :], bb[rows, :], True)
        hh = h_loc + a_c * carry
        o_ref[0, rows, :] = ((hf[rows, :] + hh) * jax.nn.gelu(y_ref[0, rows, :])).astype(o_ref.dtype)
        return hh[0:1, :]

    zero = jnp.zeros((1, w), F32)
    lax.fori_loop(0, ntl, fwd, zero)
    c = lax.fori_loop(0, ctl, lambda t, c: bwd(ctl - 1 - t, c), zero)
    lax.fori_loop(0, ntl - ctl, lambda t, c: bwd(ntl - 1 - t, c), c)


def _rglru(p3, conv_wb, gate_w, gate_b, sp, l_len):
    b, s, _ = p3.shape
    nh = LRU_WIDTH // LRU_HALF
    xoff = (4 * RET_HEADS * RET_DK) // LRU_HALF
    yoff = xoff + nh
    return pl.pallas_call(
        functools.partial(_lru_kernel, s_len=s, l_len=l_len),
        out_shape=jax.ShapeDtypeStruct((b, s, LRU_WIDTH), BF16),
        grid=(b, nh),
        in_specs=[pl.BlockSpec((1, s, LRU_HALF), lambda bi, j: (bi, 0, xoff + j)),
                  pl.BlockSpec((1, s, LRU_HALF), lambda bi, j: (bi, 0, yoff + j)),
                  pl.BlockSpec((1, 8, LRU_HALF), lambda bi, j: (j, 0, 0)),
                  pl.BlockSpec((1, LRU_HALF, 4 * LRU_HALF), lambda bi, j: (j, 0, 0)),
                  pl.BlockSpec((1, 1, 4 * LRU_HALF), lambda bi, j: (j, 0, 0)),
                  pl.BlockSpec((1, 8, LRU_HALF), lambda bi, j: (j, 0, 0))],
        out_specs=pl.BlockSpec((1, s, LRU_HALF), lambda bi, j: (bi, 0, j)),
        scratch_shapes=[pltpu.VMEM((s + 16, LRU_HALF), F32), pltpu.VMEM((s, LRU_HALF), F32),
                        pltpu.VMEM((s, LRU_HALF), F32), pltpu.VMEM((s, LRU_HALF), F32)],
        compiler_params=_cparams("arbitrary", "arbitrary"), name="rglru",
    )(p3, p3, conv_wb, gate_w, gate_b, sp)


def _lru_params(conv_w, conv_b, gate_w, gate_b, lam):
    nh = LRU_WIDTH // LRU_HALF
    bph = LRU_HALF // LRU_BLOCK
    cw = jnp.concatenate([conv_w, conv_b[None, :], jnp.zeros((3, LRU_WIDTH), F32)], axis=0)
    cw = cw.reshape(8, nh, LRU_HALF).transpose(1, 0, 2)
    eye = jnp.eye(bph, dtype=F32)
    gw = gate_w.reshape(2, 2, nh, bph, LRU_BLOCK, LRU_BLOCK)
    dense = jnp.einsum('dgjkio,kl->jkidglo', gw, eye)
    dense = dense.reshape(nh, LRU_HALF, 4 * LRU_HALF).astype(BF16)
    gb = gate_b.reshape(2, 2, nh, LRU_HALF).transpose(2, 0, 1, 3).reshape(nh, 1, 4 * LRU_HALF)
    sp = jax.nn.softplus(-lam.astype(F32)).reshape(2, nh, LRU_HALF).transpose(1, 0, 2)
    sp = jnp.concatenate([sp, jnp.zeros((nh, 6, LRU_HALF), F32)], axis=1)
    return cw, dense, gb, sp


def _post_kernel(x_ref, ma_ref, mb_ref, w_ref, mod_ref, g_ref, wr_ref, br_ref, xo_ref, h_ref, r_ref, cnt_ref):
    m = jnp.concatenate([ma_ref[...], mb_ref[...]], axis=1)
    o = _dot(m, w_ref[...])
    x = x_ref[...] + mod_ref[0, 2:3, :] * o
    xo_ref[...] = x
    h = _modulated(x, g_ref, mod_ref, 3)
    h_ref[...] = h
    lg = jnp.dot(h, wr_ref[...], preferred_element_type=F32, precision=lax.Precision.HIGHEST) + br_ref[...]
    lane = lax.broadcasted_iota(jnp.int32, lg.shape, 1)
    lanef = lane.astype(F32)
    ninf = -jnp.inf
    big = float(LANES)
    gl = jnp.where(lane < MOE_GROUPS, lg, ninf)
    gmax = jnp.max(gl, axis=1, keepdims=True)
    g_top = 1.0 / jnp.sum(jnp.exp(gl - gmax), axis=1, keepdims=True)
    g_sel = jnp.min(jnp.where(gl == gmax, lanef, big), axis=1, keepdims=True)
    lo = MOE_GROUPS + MOE_PER_GROUP * g_sel
    el = jnp.where((lanef >= lo) & (lanef < lo + MOE_PER_GROUP), lg, ninf)
    emax = jnp.max(el, axis=1, keepdims=True)
    esum = jnp.sum(jnp.exp(el - emax), axis=1, keepdims=True)
    i1 = jnp.min(jnp.where(el == emax, lanef, big), axis=1, keepdims=True)
    el2 = jnp.where(lanef == i1, ninf, el)
    m2 = jnp.max(el2, axis=1, keepdims=True)
    i2 = jnp.min(jnp.where(el2 == m2, lanef, big), axis=1, keepdims=True)
    p1 = 1.0 / esum
    p2 = jnp.exp(m2 - emax) / esum
    tot = p1 + p2
    w1 = g_top * (p1 / tot)
    w2 = g_top * (p2 / tot)
    hit1 = lanef == i1
    hit2 = lanef == i2
    onehot = jnp.where(hit1, 1.0, 0.0) + jnp.where(hit2, 1.0, 0.0)
    ti = lax.broadcasted_iota(jnp.int32, (ROW_TILE, ROW_TILE), 0)
    tj = lax.broadcasted_iota(jnp.int32, (ROW_TILE, ROW_TILE), 1)
    earlier = jnp.where(tj < ti, 1.0, 0.0).astype(BF16)

    @pl.when(pl.program_id(0) == 0)
    def _():
        cnt_ref[...] = jnp.zeros(cnt_ref.shape, F32)

    before = _dot(earlier, onehot.astype(BF16)) + cnt_ref[0:1, :]
    k1 = jnp.sum(jnp.where(hit1, before, 0.0), axis=1, keepdims=True)
    k2 = jnp.sum(jnp.where(hit2, before, 0.0), axis=1, keepdims=True)
    cnt_ref[0:1, :] = cnt_ref[0:1, :] + jnp.sum(onehot, axis=0, keepdims=True)
    vals = (i1 - MOE_GROUPS, i2 - MOE_GROUPS, w1, w2, k1, k2)
    slab = jnp.zeros(lg.shape, F32)
    for col, v in enumerate(vals):
        slab = jnp.where(lane == col, v, slab)
    r_ref[...] = slab


def _post(rt, x, ma, mb, cb, w_out, mods, g2, wr, br):
    d = x.shape[1]
    hd = d // 2
    return pl.pallas_call(
        _post_kernel,
        out_shape=[jax.ShapeDtypeStruct((rt.rows, d), F32), jax.ShapeDtypeStruct((rt.rows, d), F32),
                   jax.ShapeDtypeStruct((rt.rows, LANES), F32), jax.ShapeDtypeStruct((8, LANES), F32)],
        grid=(rt.n_tiles,),
        in_specs=[pl.BlockSpec((ROW_TILE, d), lambda i: (i, 0)),
                  pl.BlockSpec((ROW_TILE, hd), lambda i: (i, 0)),
                  pl.BlockSpec((ROW_TILE, hd), lambda i: (i, cb)),
                  pl.BlockSpec((d, d), lambda i: (0, 0)),
                  pl.BlockSpec((1, 8, d), lambda i: (rt.mod_idx(i), 0, 0)),
                  pl.BlockSpec((1, d), lambda i: (0, 0)),
                  pl.BlockSpec((d, LANES), lambda i: (0, 0)),
                  pl.BlockSpec((1, LANES), lambda i: (0, 0))],
        out_specs=[pl.BlockSpec((ROW_TILE, d), lambda i: (i, 0)),
                   pl.BlockSpec((ROW_TILE, d), lambda i: (i, 0)),
                   pl.BlockSpec((ROW_TILE, LANES), lambda i: (i, 0)),
                   pl.BlockSpec((8, LANES), lambda i: (0, 0))],
        compiler_params=_cparams("arbitrary"), name="post",
    )(x, ma, mb, w_out, mods, g2.reshape(1, d), wr, br)


def _moe_plan(route, cnt):
    mb = MOE_ROWS
    t_count = route.shape[0]
    nb = (2 * t_count + MOE_EXPERTS * (mb - 1) + mb - 1) // mb
    counts = cnt[0, MOE_GROUPS:MOE_GROUPS + MOE_EXPERTS].astype(jnp.int32)
    padded = (counts + mb - 1) // mb * mb
    pend = jnp.cumsum(padded)
    pstart = pend - padded
    experts = jnp.arange(MOE_EXPERTS, dtype=jnp.int32)
    e = route[:, 0:2].astype(jnp.int32)
    first = jnp.sum(jnp.where(e[:, :, None] == experts[None, None, :], pstart[None, None, :], 0), axis=-1)
    dest = (first + route[:, 4:6].astype(jnp.int32)).reshape(-1)
    blk0 = jnp.arange(nb, dtype=jnp.int32) * mb
    block_e = jnp.minimum(jnp.sum((blk0[:, None] >= pend[None, :]).astype(jnp.int32), axis=1), MOE_EXPERTS - 1)
    sel = block_e[:, None] == experts[None, :]
    used = blk0 - jnp.sum(jnp.where(sel, pstart[None, :], 0), axis=1)
    n_valid = jnp.clip(jnp.sum(jnp.where(sel, counts[None, :], 0), axis=1) - used, 0, mb).astype(jnp.int32)
    return dest, block_e, n_valid, nb


def _dispatch_kernel(dest_ref, h_hbm, zero_hbm, xs_hbm, sem, *, nt):
    del zero_hbm
    i = pl.program_id(0)
    slot = i % 2

    def body(j, c):
        tok = i * ROW_TILE + j
        src = h_hbm.at[pl.ds(tok, 1)]
        pltpu.make_async_copy(src, xs_hbm.at[pl.ds(dest_ref[2 * tok], 1)], sem.at[slot]).start()
        pltpu.make_async_copy(src, xs_hbm.at[pl.ds(dest_ref[2 * tok + 1], 1)], sem.at[slot]).start()
        return c
    lax.fori_loop(0, ROW_TILE, body, 0, unroll=8)

    def wait_tile(sl):
        for _ in range(2):
            pltpu.make_async_copy(h_hbm.at[pl.ds(0, ROW_TILE)], xs_hbm.at[pl.ds(0, ROW_TILE)], sem.at[sl]).wait()

    @pl.when(i > 0)
    def _():
        wait_tile(1 - slot)

    @pl.when(i == nt - 1)
    def _():
        wait_tile(slot)


def _dispatch(h, dest, n_rows):
    t_count, d = h.shape
    nt = t_count // ROW_TILE
    grid_spec = pltpu.PrefetchScalarGridSpec(
        num_scalar_prefetch=1, grid=(nt,),
        in_specs=[pl.BlockSpec(memory_space=pl.ANY), pl.BlockSpec(memory_space=pl.ANY)],
        out_specs=pl.BlockSpec(memory_space=pl.ANY),
        scratch_shapes=[pltpu.SemaphoreType.DMA((2,))])
    return pl.pallas_call(
        functools.partial(_dispatch_kernel, nt=nt),
        out_shape=jax.ShapeDtypeStruct((n_rows, d), F32),
        grid_spec=grid_spec, input_output_aliases={2: 0},
        compiler_params=_cparams("arbitrary"), name="dispatch",
    )(dest, h, jnp.zeros((n_rows, d), F32))


def _expert_kernel(be_ref, nv_ref, x_ref, wg_ref, wu_ref, wd_ref, y_ref, wgb, wub, wdb):
    i = pl.program_id(0)

    @pl.when(nv_ref[i] > 0)
    def _():
        @pl.when((i == 0) | (be_ref[i] != be_ref[jnp.maximum(i - 1, 0)]))
        def _():
            wgb[...] = wg_ref[0].astype(BF16)
            wub[...] = wu_ref[0].astype(BF16)
            wdb[...] = wd_ref[0].astype(BF16)

        x = x_ref[...].astype(BF16)
        a = (jax.nn.silu(_dot(x, wgb[...])) * _dot(x, wub[...])).astype(BF16)
        y_ref[...] = _dot(a, wdb[...])

    @pl.when(nv_ref[i] == 0)
    def _():
        y_ref[...] = jnp.zeros(y_ref.shape, y_ref.dtype)


def _experts(xs, block_e, n_valid, wg, wu, wd):
    n_rows, d = xs.shape
    mb = MOE_ROWS
    hid = wg.shape[2]

    def wspec(shape):
        return pl.BlockSpec(shape, lambda i, be, nv: (be[i], 0, 0))

    grid_spec = pltpu.PrefetchScalarGridSpec(
        num_scalar_prefetch=2, grid=(n_rows // mb,),
        in_specs=[pl.BlockSpec((mb, d), lambda i, be, nv: (i, 0)),
                  wspec((1, d, hid)), wspec((1, d, hid)), wspec((1, hid, d))],
        out_specs=pl.BlockSpec((mb, d), lambda i, be, nv: (i, 0)),
        scratch_shapes=[pltpu.VMEM((d, hid), BF16), pltpu.VMEM((d, hid), BF16), pltpu.VMEM((hid, d), BF16)])
    return pl.pallas_call(
        _expert_kernel,
        out_shape=jax.ShapeDtypeStruct((n_rows, d), F32),
        grid_spec=grid_spec,
        compiler_params=_cparams("arbitrary"), name="experts",
    )(block_e, n_valid, xs, wg, wu, wd)


def _combine_kernel(dest_ref, x_ref, r_ref, modp_ref, *rest, final, tile_of):
    if final:
        g_ref, ys_hbm, o_ref, ybuf, sem = rest
    else:
        ys_hbm, o_ref, ybuf, sem = rest
    i = pl.program_id(0)
    n = pl.num_programs(0)
    slot = i % 2

    def issue(step, sl):
        base = tile_of(step) * ROW_TILE

        def body(j, c):
            tok = base + j
            for k in range(2):
                pltpu.make_async_copy(ys_hbm.at[pl.ds(dest_ref[2 * tok + k], 1)], ybuf.at[sl, k, pl.ds(j, 1)],
                                      sem.at[sl]).start()
            return c
        lax.fori_loop(0, ROW_TILE, body, 0, unroll=8)

    @pl.when(i == 0)
    def _():
        issue(0, 0)

    @pl.when(i + 1 < n)
    def _():
        issue(i + 1, 1 - slot)

    for k in range(2):
        pltpu.make_async_copy(ys_hbm.at[pl.ds(0, ROW_TILE)], ybuf.at[slot, k], sem.at[slot]).wait()
    r = r_ref[...]
    y = ybuf[slot, 0] * r[:, 2:3] + ybuf[slot, 1] * r[:, 3:4]
    x = x_ref[...] + modp_ref[0, 5:6, :] * y
    o_ref[...] = _rms(x, g_ref[...]) if final else x


def _combine(rt, x, route, mods, ys, dest, final_g=None, n_len=None):
    d = x.shape[1]
    final = final_g is not None
    if final:
        lt = n_len // ROW_TILE
        steps, out_rows = rt.b * lt, rt.b * n_len

        def tile_of(i):
            return (i // lt) * rt.tpb + rt.ctx_tiles + i % lt
    else:
        steps, out_rows = rt.n_tiles, rt.rows

        def tile_of(i):
            return i

    in_specs = [pl.BlockSpec((ROW_TILE, d), lambda i, de: (tile_of(i), 0)),
                pl.BlockSpec((ROW_TILE, LANES), lambda i, de: (tile_of(i), 0)),
                pl.BlockSpec((1, 8, d), lambda i, de: (rt.mod_idx(tile_of(i)), 0, 0))]
    args = [x, route, mods]
    if final:
        in_specs.append(pl.BlockSpec((1, d), lambda i, de: (0, 0)))
        args.append(final_g.reshape(1, d))
    in_specs.append(pl.BlockSpec(memory_space=pl.ANY))
    args.append(ys)
    grid_spec = pltpu.PrefetchScalarGridSpec(
        num_scalar_prefetch=1, grid=(steps,), in_specs=in_specs,
        out_specs=pl.BlockSpec((ROW_TILE, d), lambda i, de: (i, 0)),
        scratch_shapes=[pltpu.VMEM((2, 2, ROW_TILE, d), F32), pltpu.SemaphoreType.DMA((2,))])
    return pl.pallas_call(
        functools.partial(_combine_kernel, final=final, tile_of=tile_of),
        out_shape=jax.ShapeDtypeStruct((out_rows, d), F32),
        grid_spec=grid_spec,
        compiler_params=_cparams("arbitrary"), name="combine_final" if final else "combine",
    )(dest, *args)


def _pre_mla_kernel(x_ref, mod_ref, g_ref, win_ref, qg_ref, kvg_ref,
                    wq_ref, wqs_ref, wk_ref, wv_ref, ct_ref, st_ref, q_ref, k_ref, v_ref):
    h = _modulated(x_ref[...], g_ref, mod_ref, 0)
    p = _dot(h.astype(BF16), win_ref[...])
    cq = _rms(p[:, 0:MLA_Q_RANK], qg_ref[...]).astype(BF16)
    ckv = _rms(p[:, MLA_Q_RANK:MLA_Q_RANK + MLA_KV_RANK], kvg_ref[...]).astype(BF16)
    off = MLA_Q_RANK + MLA_KV_RANK
    ct, st = ct_ref[...], st_ref[...]
    k_rope = p[:, off:off + HEAD_PAD] * ct + p[:, off + HEAD_PAD:off + 2 * HEAD_PAD] * st
    qa = _dot(cq, wq_ref[...])
    qb = _dot(cq, wqs_ref[...])
    kn = _dot(ckv, wk_ref[...])
    v_ref[...] = _dot(ckv, wv_ref[...]).astype(v_ref.dtype)
    for hh in range(MLA_HEADS):
        sl = slice(hh * HEAD_PAD, (hh + 1) * HEAD_PAD)
        q_ref[:, sl] = ((qa[:, sl] * ct + qb[:, sl] * st) * MLA_SCALE).astype(q_ref.dtype)
        k_ref[:, sl] = (kn[:, sl] + k_rope).astype(k_ref.dtype)


def _pre_mla(rt, x, mods, g1, wts, ct, st):
    d = x.shape[1]
    w_in, qg, kvg, wq, wqs, wk, wv = wts

    def full(a):
        return pl.BlockSpec(a.shape, lambda i: (0,) * a.ndim)

    hq = MLA_HEADS * HEAD_PAD
    hv = MLA_HEADS * MLA_V
    return pl.pallas_call(
        _pre_mla_kernel,
        out_shape=[jax.ShapeDtypeStruct((rt.rows, hq), BF16),
                   jax.ShapeDtypeStruct((rt.rows, hq), BF16), jax.ShapeDtypeStruct((rt.rows, hv), BF16)],
        grid=(rt.n_tiles,),
        in_specs=[
            pl.BlockSpec((ROW_TILE, d), lambda i: (i, 0)),
            pl.BlockSpec((1, 8, d), lambda i: (rt.mod_idx(i), 0, 0)),
            pl.BlockSpec((1, d), lambda i: (0, 0)),
            full(w_in), full(qg), full(kvg), full(wq), full(wqs), full(wk), full(wv),
            pl.BlockSpec((ROW_TILE, HEAD_PAD), lambda i: (rt.pos_idx(i), 0)),
            pl.BlockSpec((ROW_TILE, HEAD_PAD), lambda i: (rt.pos_idx(i), 0))],
        out_specs=[pl.BlockSpec((ROW_TILE, hq), lambda i: (i, 0)),
                   pl.BlockSpec((ROW_TILE, hq), lambda i: (i, 0)),
                   pl.BlockSpec((ROW_TILE, hv), lambda i: (i, 0))],
        compiler_params=_cparams("arbitrary"), name="pre_mla",
    )(x, mods, g1.reshape(1, d), w_in, qg, kvg, wq, wqs, wk, wv, ct, st)


def _mla_params(w_in, q_g, kv_g, w_uq, w_ukv):
    d = w_in.shape[0]
    hp, hr = HEAD_PAD, MLA_ROPE // 2
    nq = MLA_NOPE + MLA_ROPE
    kr = w_in[:, MLA_Q_RANK + MLA_KV_RANK:]
    z = jnp.zeros((d, MLA_NOPE), F32)
    zt = jnp.zeros((d, hp - nq), F32)
    kr_a = jnp.concatenate([z, kr, zt], axis=1)
    kr_b = jnp.concatenate([z, -kr[:, hr:], kr[:, :hr], zt], axis=1)
    w_in_p = jnp.concatenate([w_in[:, :MLA_Q_RANK + MLA_KV_RANK], kr_a, kr_b], axis=1).astype(BF16)
    wq = w_uq.reshape(MLA_Q_RANK, MLA_HEADS, nq)
    zq = jnp.zeros((MLA_Q_RANK, MLA_HEADS, hp - nq), F32)
    wq_a = jnp.concatenate([wq, zq], axis=2).reshape(MLA_Q_RANK, MLA_HEADS * hp).astype(BF16)
    wq_b = jnp.concatenate([jnp.zeros_like(wq[:, :, :MLA_NOPE]), -wq[:, :, MLA_NOPE + hr:],
                            wq[:, :, MLA_NOPE:MLA_NOPE + hr], zq], axis=2)
    wq_b = wq_b.reshape(MLA_Q_RANK, MLA_HEADS * hp).astype(BF16)
    wkv = w_ukv.reshape(MLA_KV_RANK, MLA_HEADS, MLA_NOPE + MLA_V)
    wk = jnp.concatenate([wkv[:, :, :MLA_NOPE], jnp.zeros((MLA_KV_RANK, MLA_HEADS, hp - MLA_NOPE), F32)], axis=2)
    wk = wk.reshape(MLA_KV_RANK, MLA_HEADS * hp).astype(BF16)
    wv = wkv[:, :, MLA_NOPE:].reshape(MLA_KV_RANK, MLA_HEADS * MLA_V).astype(BF16)
    return (w_in_p, q_g.reshape(1, -1), kv_g.reshape(1, -1), wq_a, wq_b, wk, wv)


def _attn_kernel(q_ref, k_ref, v_ref, o_ref, *, s_len, l_len):
    t = pl.program_id(2)
    lane = lax.broadcasted_iota(jnp.int32, (ATT_TQ, 2 * MLA_V), 1)

    def attend(nk):
        outs = []
        vv = v_ref[0, 0:nk, :]
        for j in range(2):
            q = q_ref[0, :, j * HEAD_PAD:(j + 1) * HEAD_PAD]
            k = k_ref[0, 0:nk, j * HEAD_PAD:(j + 1) * HEAD_PAD]
            s = _dot_nt(q, k)
            p = jnp.exp(s - jnp.max(s, axis=1, keepdims=True))
            den = jnp.sum(p, axis=1, keepdims=True)
            outs.append(_dot(p.astype(BF16), vv) / den)
        o_ref[0] = jnp.where(lane < MLA_V, outs[0], outs[1]).astype(o_ref.dtype)

    ctx_tiles = l_len // ATT_TQ

    @pl.when(t < ctx_tiles)
    def _():
        attend(l_len)

    @pl.when(t >= ctx_tiles)
    def _():
        attend(s_len)


def _attention(q3, k3, v3, l_len):
    b, s, _ = q3.shape
    hp2 = 2 * HEAD_PAD
    return pl.pallas_call(
        functools.partial(_attn_kernel, s_len=s, l_len=l_len),
        out_shape=jax.ShapeDtypeStruct((b, s, MLA_HEADS * MLA_V), BF16),
        grid=(b, MLA_HEADS // 2, s // ATT_TQ),
        in_specs=[pl.BlockSpec((1, ATT_TQ, hp2), lambda bi, hi, ti: (bi, ti, hi)),
                  pl.BlockSpec((1, s, hp2), lambda bi, hi, ti: (bi, 0, hi)),
                  pl.BlockSpec((1, s, 2 * MLA_V), lambda bi, hi, ti: (bi, 0, hi))],
        out_specs=pl.BlockSpec((1, ATT_TQ, 2 * MLA_V), lambda bi, hi, ti: (bi, ti, hi)),
        compiler_params=_cparams("arbitrary", "arbitrary", "arbitrary"), name="attention",
    )(q3, k3, v3)


def _ret_tables(n, l):
    t = jnp.arange(n, dtype=F32)
    inv = ROPE_BASE ** (-jnp.arange(0, RET_DK, 2, dtype=F32) / RET_DK)
    ang = t[:, None] * inv[None, :]
    cos, sin = jnp.cos(ang), jnp.sin(ang)
    cos2 = jnp.concatenate([jnp.ones((l, RET_DK), F32), jnp.concatenate([cos, cos], axis=1)], axis=0)
    sin2 = jnp.concatenate([jnp.zeros((l, RET_DK), F32), jnp.concatenate([-sin, sin], axis=1)], axis=0)
    return cos2, sin2


def _mla_tables(n, l):
    rows = n // GRID_W
    r_pos = jnp.repeat(jnp.arange(rows, dtype=F32), GRID_W)
    c_pos = jnp.tile(jnp.arange(GRID_W, dtype=F32), rows)
    ax = MLA_ROPE // 2
    inv = ROPE_BASE ** (-jnp.arange(0, ax, 2, dtype=F32) / ax)
    ang = jnp.concatenate([r_pos[:, None] * inv[None, :], c_pos[:, None] * inv[None, :]], axis=-1)
    cos, sin = jnp.cos(ang), jnp.sin(ang)
    pad = HEAD_PAD - MLA_NOPE - MLA_ROPE
    ct_l = jnp.concatenate([jnp.ones((n, MLA_NOPE), F32), cos, cos, jnp.zeros((n, pad), F32)], axis=1)
    st_l = jnp.concatenate([jnp.zeros((n, MLA_NOPE), F32), sin, sin, jnp.zeros((n, pad), F32)], axis=1)
    ct_c = jnp.concatenate([jnp.ones((l, MLA_NOPE + MLA_ROPE), F32), jnp.zeros((l, pad), F32)], axis=1)
    return jnp.concatenate([ct_c, ct_l], axis=0), jnp.concatenate([jnp.zeros((l, HEAD_PAD), F32), st_l], axis=0)


def kernel(x, c, ctx, c_ctx, ada_w, ada_b, norm_g, ab_w_in, ab_w_out, ret_decay_logit, lru_conv_w, lru_conv_b, lru_gate_w, lru_gate_b, lru_lambda, mla_w_in, mla_q_norm_g, mla_kv_norm_g, mla_w_uq, mla_w_ukv, mla_w_out, moe_group_w, moe_group_b, moe_expert_w, moe_expert_b, moe_w_gate, moe_w_up, moe_w_down, final_norm_g):
    b, n, d = x.shape
    l = ctx.shape[1]
    s = l + n
    depth = ada_w.shape[0]
    rt = _Rows(b, s, l)

    nrow = (b + 1 + 7) // 8 * 8
    cvec = jnp.concatenate([c, c_ctx[None, :], jnp.zeros((nrow - b - 1, d), F32)], axis=0)
    ada = _ada_all(cvec, ada_w, ada_b)

    def layer_mods(layer):
        lat = ada[layer, :b].reshape(b, 1, 6, d)
        cx = jnp.broadcast_to(ada[layer, b].reshape(1, 1, 6, d), (b, 1, 6, d))
        m = jnp.concatenate([cx, lat], axis=1)
        m = jnp.concatenate([m, jnp.zeros((b, 2, 2, d), F32)], axis=2)
        return m.reshape(2 * b, 8, d)

    cos2, sin2 = _ret_tables(n, l)
    ct, st = _mla_tables(n, l)

    xs = jnp.concatenate([ctx, x], axis=1).reshape(b * s, d)
    out = None
    for layer in range(depth):
        mods = layer_mods(layer)
        i = layer // 2
        if layer % 2 == 0:
            p3 = _pre_ab(rt, xs, mods, norm_g[layer, 0], ab_w_in[i].astype(BF16)).reshape(b, s, -1)
            lg = jax.nn.log_sigmoid(ret_decay_logit[i].astype(F32))
            lgv = jnp.broadcast_to(lg.T[:, :, None], (RET_HEADS, 2, LANES))
            lgv = jnp.concatenate([lgv, jnp.zeros((RET_HEADS, 6, LANES), F32)], axis=1)
            ma = _retention(p3, cos2, sin2, lgv, l).reshape(b * s, -1)
            mb = _rglru(p3, *_lru_params(lru_conv_w[i], lru_conv_b[i], lru_gate_w[i], lru_gate_b[i],
                                         lru_lambda[i]), l).reshape(b * s, -1)
            cb = 0
            w_out = ab_w_out[i].astype(BF16)
        else:
            wts = _mla_params(mla_w_in[i], mla_q_norm_g[i], mla_kv_norm_g[i], mla_w_uq[i], mla_w_ukv[i])
            q, k, v = _pre_mla(rt, xs, mods, norm_g[layer, 0], wts, ct, st)
            att = _attention(q.reshape(b, s, -1), k.reshape(b, s, -1), v.reshape(b, s, -1), l)
            ma = mb = att.reshape(b * s, -1)
            cb = 1
            w_out = mla_w_out[i].astype(BF16)
        wr = jnp.concatenate([moe_group_w[layer], moe_expert_w[layer],
                              jnp.zeros((d, LANES - MOE_GROUPS - MOE_EXPERTS), F32)], axis=1)
        br = jnp.concatenate([moe_group_b[layer], moe_expert_b[layer],
                              jnp.zeros((LANES - MOE_GROUPS - MOE_EXPERTS,), F32)]).reshape(1, LANES)
        xs, h2, route, cnt = _post(rt, xs, ma, mb, cb, w_out, mods, norm_g[layer, 1], wr, br)
        dest, block_e, n_valid, nb = _moe_plan(route, cnt)
        xsort = _dispatch(h2, dest, nb * MOE_ROWS)
        ys = _experts(xsort, block_e, n_valid, moe_w_gate[layer], moe_w_up[layer], moe_w_down[layer])
        if layer + 1 < depth:
            xs = _combine(rt, xs, route, mods, ys, dest)
        else:
            out = _combine(rt, xs, route, mods, ys, dest, final_norm_g, n)
    return out.reshape(b, n, d)
```

```python
import functools

import jax
import jax.numpy as jnp
from jax import lax
from jax.experimental import pallas as pl
from jax.experimental.pallas import tpu as pltpu

F32 = jnp.float32
BF16 = jnp.bfloat16

EPS = 1e-6
ROPE_BASE = 10000.0
GRID_W = 64

RET_HEADS = 4
RET_DK = 128
RET_CHUNK = 128
LRU_WIDTH = 512
LRU_BLOCK = 64
LRU_C = 8.0
LRU_HALF = 256
LRU_TILE = 128

MLA_HEADS = 16
MLA_NOPE = 64
MLA_ROPE = 32
MLA_V = 64
MLA_Q_RANK = 384
MLA_KV_RANK = 256
MLA_SCALE = (MLA_NOPE + MLA_ROPE) ** -0.5
HEAD_PAD = 128

MOE_GROUPS = 4
MOE_PER_GROUP = 8
MOE_EXPERTS = 32
MOE_ROWS = 256

ROW_TILE = 256
ATT_TQ = 256
ATT_HEADS = 4
LANES = 128
VMEM_LIMIT = 56 * 1024 * 1024


def _cparams(*sem):
    return pltpu.CompilerParams(dimension_semantics=sem, vmem_limit_bytes=VMEM_LIMIT)


def _rms(x, g):
    return x * lax.rsqrt(jnp.mean(x * x, axis=-1, keepdims=True) + EPS) * g


def _dot(a, b):
    return jnp.dot(a, b, preferred_element_type=F32)


def _dot_nt(a, b):
    return lax.dot_general(a, b, (((1,), (1,)), ((), ())), preferred_element_type=F32)


def _dot_tn(a, b):
    return lax.dot_general(a, b, (((0,), (0,)), ((), ())), preferred_element_type=F32)


def _ada_kernel(s_ref, w_ref, b_ref, o_ref):
    s = jax.nn.silu(s_ref[...])
    o_ref[0] = _dot(s.astype(BF16), w_ref[0].astype(BF16)) + b_ref[0]


def _ada_all(cvec, ada_w, ada_b):
    depth, d, n6 = ada_w.shape
    rows = cvec.shape[0]
    tn = n6 // 4
    return pl.pallas_call(
        _ada_kernel,
        out_shape=jax.ShapeDtypeStruct((depth, rows, n6), F32),
        grid=(depth, n6 // tn),
        in_specs=[pl.BlockSpec((rows, d), lambda l, j: (0, 0)),
                  pl.BlockSpec((1, d, tn), lambda l, j: (l, 0, j)),
                  pl.BlockSpec((1, 1, tn), lambda l, j: (l, 0, j))],
        out_specs=pl.BlockSpec((1, rows, tn), lambda l, j: (l, 0, j)),
        compiler_params=_cparams("arbitrary", "arbitrary"),
        name="adaln",
    )(cvec, ada_w, ada_b.reshape(depth, 1, n6))


class _Rows:
    def __init__(self, b, s, l):
        assert s % ROW_TILE == 0 and l % ROW_TILE == 0
        self.b, self.s, self.l = b, s, l
        self.tpb = s // ROW_TILE
        self.ctx_tiles = l // ROW_TILE
        self.n_tiles = b * self.tpb
        self.rows = b * s

    def mod_idx(self, i):
        return 2 * (i // self.tpb) + jnp.where(i % self.tpb >= self.ctx_tiles, 1, 0)

    def pos_idx(self, i):
        return i % self.tpb


def _modulated(x, g_ref, mod_ref, base):
    h = _rms(x, g_ref[...])
    return h * (1.0 + mod_ref[0, base + 1:base + 2, :]) + mod_ref[0, base:base + 1, :]


def _pre_ab_kernel(x_ref, mod_ref, g_ref, w_ref, p_ref):
    h = _modulated(x_ref[...], g_ref, mod_ref, 0)
    p_ref[...] = _dot(h.astype(BF16), w_ref[...])


def _pre_ab(rt, x, mods, g1, w_in):
    d = x.shape[1]
    n_out = w_in.shape[1]
    return pl.pallas_call(
        _pre_ab_kernel,
        out_shape=jax.ShapeDtypeStruct((rt.rows, n_out), F32),
        grid=(rt.n_tiles,),
        in_specs=[pl.BlockSpec((ROW_TILE, d), lambda i: (i, 0)),
                  pl.BlockSpec((1, 8, d), lambda i: (rt.mod_idx(i), 0, 0)),
                  pl.BlockSpec((1, d), lambda i: (0, 0)),
                  pl.BlockSpec((d, n_out), lambda i: (0, 0))],
        out_specs=pl.BlockSpec((ROW_TILE, n_out), lambda i: (i, 0)),
        compiler_params=_cparams("arbitrary"), name="pre_ab",
    )(x, mods, g1.reshape(1, d), w_in)


def _ret_kernel(q_ref, k_ref, v_ref, g_ref, cos_ref, sin_ref, lg_ref, o_ref, qs, ks, acc, *, s_len, l_len):
    c = RET_CHUNK
    nch, cch = s_len // c, l_len // c
    lgf = lg_ref[0, 0:1, :]
    lgb = lg_ref[0, 1:2, :]
    ii = lax.broadcasted_iota(jnp.int32, (c, c), 0).astype(F32)
    jj = lax.broadcasted_iota(jnp.int32, (c, c), 1).astype(F32)
    diff = ii - jj
    dmask = (jnp.where(diff > 0, jnp.exp(lgf * jnp.maximum(diff, 0.0)), 0.0)
             + jnp.where(diff < 0, jnp.exp(lgb * jnp.maximum(-diff, 0.0)), 0.0)
             + jnp.where(diff == 0, 2.0, 0.0))
    zeta_f = jnp.exp(lgf * (c - 1.0 - ii))
    xi_f = jnp.exp(lgf * (ii + 1.0))
    zeta_b = jnp.exp(lgb * ii)
    xi_b = jnp.exp(lgb * (c - ii))
    cd_f = jnp.exp(lgf * c)
    cd_b = jnp.exp(lgb * c)
    k_scale = RET_DK ** -0.5

    def fwd(n, st):
        rows = pl.ds(pl.multiple_of(n * c, c), c)
        cs, sn = cos_ref[rows, :], sin_ref[rows, :]
        q = q_ref[0, rows, :]
        k = k_ref[0, rows, :]
        v = v_ref[0, rows, :]
        qb = (q * cs + pltpu.roll(q, 64, 1) * sn).astype(BF16)
        kb = ((k * cs + pltpu.roll(k, 64, 1) * sn) * k_scale).astype(BF16)
        qs[rows, :] = qb
        ks[rows, :] = kb
        sc = _dot_nt(qb, kb) * dmask
        acc[rows, :] = _dot(sc.astype(BF16), v.astype(BF16)) + _dot(qb, st.astype(BF16)) * xi_f
        return cd_f * st + _dot_tn(kb, (v * zeta_f).astype(BF16))

    def bwd(n, st):
        rows = pl.ds(pl.multiple_of(n * c, c), c)
        qb = qs[rows, :]
        kb = ks[rows, :]
        v = v_ref[0, rows, :]
        y = acc[rows, :] + _dot(qb, st.astype(BF16)) * xi_b
        y = y * lax.rsqrt(jnp.mean(y * y, axis=-1, keepdims=True) + EPS)
        o_ref[0, rows, :] = (y * jax.nn.silu(g_ref[0, rows, :])).astype(o_ref.dtype)
        return cd_b * st + _dot_tn(kb, (v * zeta_b).astype(BF16))

    zero = jnp.zeros((c, c), F32)
    lax.fori_loop(0, nch, fwd, zero)
    st = lax.fori_loop(0, cch, lambda t, st: bwd(cch - 1 - t, st), zero)
    lax.fori_loop(0, nch - cch, lambda t, st: bwd(nch - 1 - t, st), st)


def _retention(p3, cos2, sin2, lgv, l_len):
    b, s, _ = p3.shape
    h = RET_HEADS

    def col(off):
        return pl.BlockSpec((1, s, RET_DK), lambda bi, hi: (bi, 0, off + hi))

    return pl.pallas_call(
        functools.partial(_ret_kernel, s_len=s, l_len=l_len),
        out_shape=jax.ShapeDtypeStruct((b, s, h * RET_DK), BF16),
        grid=(b, h),
        in_specs=[col(0), col(h), col(2 * h), col(3 * h),
                  pl.BlockSpec((s, RET_DK), lambda bi, hi: (0, 0)),
                  pl.BlockSpec((s, RET_DK), lambda bi, hi: (0, 0)),
                  pl.BlockSpec((1, 8, LANES), lambda bi, hi: (hi, 0, 0))],
        out_specs=pl.BlockSpec((1, s, RET_DK), lambda bi, hi: (bi, 0, hi)),
        scratch_shapes=[pltpu.VMEM((s, RET_DK), BF16), pltpu.VMEM((s, RET_DK), BF16),
                        pltpu.VMEM((s, RET_DK), F32)],
        compiler_params=_cparams("arbitrary", "arbitrary"), name="retention",
    )(p3, p3, p3, p3, cos2, sin2, lgv)


def _tile_scan(a, b, reverse):
    n = a.shape[0]
    rows = lax.broadcasted_iota(jnp.int32, a.shape, 0)
    step = 1
    while step < n:
        shift = n - step if reverse else step
        a_s = pltpu.roll(a, shift, 0)
        b_s = pltpu.roll(b, shift, 0)
        m = (rows < n - step) if reverse else (rows >= step)
        b = jnp.where(m, a * b_s + b, b)
        a = jnp.where(m, a * a_s, a)
        step *= 2
    return a, b


def _lru_kernel(x_ref, y_ref, cw_ref, wg_ref, gb_ref, sp_ref, o_ref, xpad, hf, ab, bb, *, s_len, l_len):
    tl, w = LRU_TILE, LRU_HALF
    ntl, ctl = s_len // tl, l_len // tl
    xpad[0:8, :] = jnp.zeros((8, w), F32)
    xpad[s_len + 8:s_len + 16, :] = jnp.zeros((8, w), F32)
    xpad[8:s_len + 8, :] = x_ref[0]
    w0, w1, w2, w3, cb = (cw_ref[0, t:t + 1, :] for t in range(5))
    sp_f = sp_ref[0, 0:1, :]
    sp_b = sp_ref[0, 1:2, :]
    it = lax.broadcasted_iota(jnp.int32, (tl, w), 0)

    def coeff(gr, gi, sp, xc):
        r = jax.nn.sigmoid(gr)
        i = jax.nn.sigmoid(gi)
        log_a = -LRU_C * r * sp
        th = jnp.tanh(log_a)
        return jnp.exp(log_a), jnp.sqrt(-2.0 * th / (1.0 - th)) * (i * xc)

    def fwd(n, carry):
        r0 = pl.multiple_of(n * tl, tl)
        win = xpad[pl.ds(r0, tl + 16), :]
        t = r0 + it
        seg = jnp.where(t >= l_len, 1, 0)

        def tap(d):
            v = pltpu.roll(win, (tl + 16 - d) % (tl + 16), 0)[8:8 + tl]
            return jnp.where(jnp.where(t + d >= l_len, 1, 0) == seg, v, 0.0)

        xc = tap(-2) * w0
        xc = xc + tap(-1) * w1
        xc = xc + win[8:8 + tl] * w2
        xc = xc + tap(1) * w3
        xc = xc + cb
        gts = _dot(xc.astype(BF16), wg_ref[0]) + gb_ref[0]
        a_f, b_f = coeff(gts[:, 0:w], gts[:, w:2 * w], sp_f, xc)
        a_b, b_b = coeff(gts[:, 2 * w:3 * w], gts[:, 3 * w:4 * w], sp_b, xc)
        rows = pl.ds(r0, tl)
        ab[rows, :] = a_b
        bb[rows, :] = b_b
        a_c, h_loc = _tile_scan(a_f, b_f, False)
        hh = h_loc + a_c * carry
        hf[rows, :] = hh
        return hh[tl - 1:tl, :]

    def bwd(n, carry):
        rows = pl.ds(pl.multiple_of(n * tl, tl), tl)
        a_c, h_loc = _tile_scan(ab[rows, :], bb[rows, :], True)
        hh = h_loc + a_c * carry
        o_ref[0, rows, :] = ((hf[rows, :] + hh) * jax.nn.gelu(y_ref[0, rows, :])).astype(o_ref.dtype)
        return hh[0:1, :]

    zero = jnp.zeros((1, w), F32)
    lax.fori_loop(0, ntl, fwd, zero)
    c = lax.fori_loop(0, ctl, lambda t, c: bwd(ctl - 1 - t, c), zero)
    lax.fori_loop(0, ntl - ctl, lambda t, c: bwd(ntl - 1 - t, c), c)


def _rglru(p3, conv_wb, gate_w, gate_b, sp, l_len):
    b, s, _ = p3.shape
    nh = LRU_WIDTH // LRU_HALF
    xoff = (4 * RET_HEADS * RET_DK) // LRU_HALF
    yoff = xoff + nh
    return pl.pallas_call(
        functools.partial(_lru_kernel, s_len=s, l_len=l_len),
        out_shape=jax.ShapeDtypeStruct((b, s, LRU_WIDTH), BF16),
        grid=(b, nh),
        in_specs=[pl.BlockSpec((1, s, LRU_HALF), lambda bi, j: (bi, 0, xoff + j)),
                  pl.BlockSpec((1, s, LRU_HALF), lambda bi, j: (bi, 0, yoff + j)),
                  pl.BlockSpec((1, 8, LRU_HALF), lambda bi, j: (j, 0, 0)),
                  pl.BlockSpec((1, LRU_HALF, 4 * LRU_HALF), lambda bi, j: (j, 0, 0)),
                  pl.BlockSpec((1, 1, 4 * LRU_HALF), lambda bi, j: (j, 0, 0)),
                  pl.BlockSpec((1, 8, LRU_HALF), lambda bi, j: (j, 0, 0))],
        out_specs=pl.BlockSpec((1, s, LRU_HALF), lambda bi, j: (bi, 0, j)),
        scratch_shapes=[pltpu.VMEM((s + 16, LRU_HALF), F32), pltpu.VMEM((s, LRU_HALF), F32),
                        pltpu.VMEM((s, LRU_HALF), F32), pltpu.VMEM((s, LRU_HALF), F32)],
        compiler_params=_cparams("arbitrary", "arbitrary"), name="rglru",
    )(p3, p3, conv_wb, gate_w, gate_b, sp)


def _lru_params(conv_w, conv_b, gate_w, gate_b, lam):
    nh = LRU_WIDTH // LRU_HALF
    bph = LRU_HALF // LRU_BLOCK
    cw = jnp.concatenate([conv_w, conv_b[None, :], jnp.zeros((3, LRU_WIDTH), F32)], axis=0)
    cw = cw.reshape(8, nh, LRU_HALF).transpose(1, 0, 2)
    eye = jnp.eye(bph, dtype=F32)
    gw = gate_w.reshape(2, 2, nh, bph, LRU_BLOCK, LRU_BLOCK)
    dense = jnp.einsum('dgjkio,kl->jkidglo', gw, eye)
    dense = dense.reshape(nh, LRU_HALF, 4 * LRU_HALF).astype(BF16)
    gb = gate_b.reshape(2, 2, nh, LRU_HALF).transpose(2, 0, 1, 3).reshape(nh, 1, 4 * LRU_HALF)
    sp = jax.nn.softplus(-lam.astype(F32)).reshape(2, nh, LRU_HALF).transpose(1, 0, 2)
    sp = jnp.concatenate([sp, jnp.zeros((nh, 6, LRU_HALF), F32)], axis=1)
    return cw, dense, gb, sp


def _post_kernel(x_ref, ma_ref, mb_ref, w_ref, mod_ref, g_ref, wr_ref, br_ref, xo_ref, h_ref, r_ref, cnt_ref):
    m = jnp.concatenate([ma_ref[...], mb_ref[...]], axis=1)
    o = _dot(m, w_ref[...])
    x = x_ref[...] + mod_ref[0, 2:3, :] * o
    xo_ref[...] = x
    h = _modulated(x, g_ref, mod_ref, 3)
    h_ref[...] = h
    lg = jnp.dot(h, wr_ref[...], preferred_element_type=F32, precision=lax.Precision.HIGHEST) + br_ref[...]
    lane = lax.broadcasted_iota(jnp.int32, lg.shape, 1)
    lanef = lane.astype(F32)
    ninf = -jnp.inf
    big = float(LANES)
    gl = jnp.where(lane < MOE_GROUPS, lg, ninf)
    gmax = jnp.max(gl, axis=1, keepdims=True)
    g_top = 1.0 / jnp.sum(jnp.exp(gl - gmax), axis=1, keepdims=True)
    g_sel = jnp.min(jnp.where(gl == gmax, lanef, big), axis=1, keepdims=True)
    lo = MOE_GROUPS + MOE_PER_GROUP * g_sel
    el = jnp.where((lanef >= lo) & (lanef < lo + MOE_PER_GROUP), lg, ninf)
    emax = jnp.max(el, axis=1, keepdims=True)
    esum = jnp.sum(jnp.exp(el - emax), axis=1, keepdims=True)
    i1 = jnp.min(jnp.where(el == emax, lanef, big), axis=1, keepdims=True)
    el2 = jnp.where(lanef == i1, ninf, el)
    m2 = jnp.max(el2, axis=1, keepdims=True)
    i2 = jnp.min(jnp.where(el2 == m2, lanef, big), axis=1, keepdims=True)
    p1 = 1.0 / esum
    p2 = jnp.exp(m2 - emax) / esum
    tot = p1 + p2
    w1 = g_top * (p1 / tot)
    w2 = g_top * (p2 / tot)
    hit1 = lanef == i1
    hit2 = lanef == i2
    onehot = jnp.where(hit1, 1.0, 0.0) + jnp.where(hit2, 1.0, 0.0)
    ti = lax.broadcasted_iota(jnp.int32, (ROW_TILE, ROW_TILE), 0)
    tj = lax.broadcasted_iota(jnp.int32, (ROW_TILE, ROW_TILE), 1)
    earlier = jnp.where(tj < ti, 1.0, 0.0).astype(BF16)

    @pl.when(pl.program_id(0) == 0)
    def _():
        cnt_ref[...] = jnp.zeros(cnt_ref.shape, F32)

    before = _dot(earlier, onehot.astype(BF16)) + cnt_ref[0:1, :]
    k1 = jnp.sum(jnp.where(hit1, before, 0.0), axis=1, keepdims=True)
    k2 = jnp.sum(jnp.where(hit2, before, 0.0), axis=1, keepdims=True)
    cnt_ref[0:1, :] = cnt_ref[0:1, :] + jnp.sum(onehot, axis=0, keepdims=True)
    vals = (i1 - MOE_GROUPS, i2 - MOE_GROUPS, w1, w2, k1, k2)
    slab = jnp.zeros(lg.shape, F32)
    for col, v in enumerate(vals):
        slab = jnp.where(lane == col, v, slab)
    r_ref[...] = slab


def _post(rt, x, ma, mb, cb, w_out, mods, g2, wr, br):
    d = x.shape[1]
    hd = d // 2
    return pl.pallas_call(
        _post_kernel,
        out_shape=[jax.ShapeDtypeStruct((rt.rows, d), F32), jax.ShapeDtypeStruct((rt.rows, d), F32),
                   jax.ShapeDtypeStruct((rt.rows, LANES), F32), jax.ShapeDtypeStruct((8, LANES), F32)],
        grid=(rt.n_tiles,),
        in_specs=[pl.BlockSpec((ROW_TILE, d), lambda i: (i, 0)),
                  pl.BlockSpec((ROW_TILE, hd), lambda i: (i, 0)),
                  pl.BlockSpec((ROW_TILE, hd), lambda i: (i, cb)),
                  pl.BlockSpec((d, d), lambda i: (0, 0)),
                  pl.BlockSpec((1, 8, d), lambda i: (rt.mod_idx(i), 0, 0)),
                  pl.BlockSpec((1, d), lambda i: (0, 0)),
                  pl.BlockSpec((d, LANES), lambda i: (0, 0)),
                  pl.BlockSpec((1, LANES), lambda i: (0, 0))],
        out_specs=[pl.BlockSpec((ROW_TILE, d), lambda i: (i, 0)),
                   pl.BlockSpec((ROW_TILE, d), lambda i: (i, 0)),
                   pl.BlockSpec((ROW_TILE, LANES), lambda i: (i, 0)),
                   pl.BlockSpec((8, LANES), lambda i: (0, 0))],
        compiler_params=_cparams("arbitrary"), name="post",
    )(x, ma, mb, w_out, mods, g2.reshape(1, d), wr, br)


def _moe_plan(route, cnt):
    mb = MOE_ROWS
    t_count = route.shape[0]
    nb = (2 * t_count + MOE_EXPERTS * (mb - 1) + mb - 1) // mb
    counts = cnt[0, MOE_GROUPS:MOE_GROUPS + MOE_EXPERTS].astype(jnp.int32)
    padded = (counts + mb - 1) // mb * mb
    pend = jnp.cumsum(padded)
    pstart = pend - padded
    experts = jnp.arange(MOE_EXPERTS, dtype=jnp.int32)
    e = route[:, 0:2].astype(jnp.int32)
    first = jnp.sum(jnp.where(e[:, :, None] == experts[None, None, :], pstart[None, None, :], 0), axis=-1)
    dest = (first + route[:, 4:6].astype(jnp.int32)).reshape(-1)
    blk0 = jnp.arange(nb, dtype=jnp.int32) * mb
    block_e = jnp.minimum(jnp.sum((blk0[:, None] >= pend[None, :]).astype(jnp.int32), axis=1), MOE_EXPERTS - 1)
    sel = block_e[:, None] == experts[None, :]
    used = blk0 - jnp.sum(jnp.where(sel, pstart[None, :], 0), axis=1)
    n_valid = jnp.clip(jnp.sum(jnp.where(sel, counts[None, :], 0), axis=1) - used, 0, mb).astype(jnp.int32)
    return dest, block_e, n_valid, nb


def _dispatch_kernel(dest_ref, h_ref, zero_hbm, xs_hbm, stage, sem, *, nt):
    del zero_hbm
    i = pl.program_id(0)
    slot = i % 2

    def wait_tile(sl):
        for _ in range(2):
            pltpu.make_async_copy(stage.at[sl], xs_hbm.at[pl.ds(0, ROW_TILE)], sem.at[sl]).wait()

    @pl.when(i >= 2)
    def _():
        wait_tile(slot)

    stage[slot] = h_ref[...]

    def body(j, c):
        tok = i * ROW_TILE + j
        src = stage.at[slot, pl.ds(j, 1)]
        pltpu.make_async_copy(src, xs_hbm.at[pl.ds(dest_ref[2 * tok], 1)], sem.at[slot]).start()
        pltpu.make_async_copy(src, xs_hbm.at[pl.ds(dest_ref[2 * tok + 1], 1)], sem.at[slot]).start()
        return c
    lax.fori_loop(0, ROW_TILE, body, 0, unroll=8)

    @pl.when(i == nt - 1)
    def _():
        wait_tile(slot)
        if nt > 1:
            wait_tile(1 - slot)


def _dispatch(h, dest, n_rows):
    t_count, d = h.shape
    nt = t_count // ROW_TILE
    grid_spec = pltpu.PrefetchScalarGridSpec(
        num_scalar_prefetch=1, grid=(nt,),
        in_specs=[pl.BlockSpec((ROW_TILE, d), lambda i, de: (i, 0)), pl.BlockSpec(memory_space=pl.ANY)],
        out_specs=pl.BlockSpec(memory_space=pl.ANY),
        scratch_shapes=[pltpu.VMEM((2, ROW_TILE, d), F32), pltpu.SemaphoreType.DMA((2,))])
    return pl.pallas_call(
        functools.partial(_dispatch_kernel, nt=nt),
        out_shape=jax.ShapeDtypeStruct((n_rows, d), F32),
        grid_spec=grid_spec, input_output_aliases={2: 0},
        compiler_params=_cparams("arbitrary"), name="dispatch",
    )(dest, h, jnp.zeros((n_rows, d), F32))


def _expert_kernel(be_ref, nv_ref, x_ref, wg_ref, wu_ref, wd_ref, y_ref, wgb, wub, wdb):
    i = pl.program_id(0)

    @pl.when(nv_ref[i] > 0)
    def _():
        @pl.when((i == 0) | (be_ref[i] != be_ref[jnp.maximum(i - 1, 0)]))
        def _():
            wgb[...] = wg_ref[0, 0].astype(BF16)
            wub[...] = wu_ref[0, 0].astype(BF16)
            wdb[...] = wd_ref[0, 0].astype(BF16)

        x = x_ref[...].astype(BF16)
        a = (jax.nn.silu(_dot(x, wgb[...])) * _dot(x, wub[...])).astype(BF16)
        y_ref[...] = _dot(a, wdb[...])

    @pl.when(nv_ref[i] == 0)
    def _():
        y_ref[...] = jnp.zeros(y_ref.shape, y_ref.dtype)


def _experts(xs, block_e, n_valid, layer, wg, wu, wd):
    n_rows, d = xs.shape
    mb = MOE_ROWS
    hid = wg.shape[3]

    def wspec(shape):
        return pl.BlockSpec(shape, lambda i, be, nv: (layer, be[i], 0, 0))

    grid_spec = pltpu.PrefetchScalarGridSpec(
        num_scalar_prefetch=2, grid=(n_rows // mb,),
        in_specs=[pl.BlockSpec((mb, d), lambda i, be, nv: (i, 0)),
                  wspec((1, 1, d, hid)), wspec((1, 1, d, hid)), wspec((1, 1, hid, d))],
        out_specs=pl.BlockSpec((mb, d), lambda i, be, nv: (i, 0)),
        scratch_shapes=[pltpu.VMEM((d, hid), BF16), pltpu.VMEM((d, hid), BF16), pltpu.VMEM((hid, d), BF16)])
    return pl.pallas_call(
        _expert_kernel,
        out_shape=jax.ShapeDtypeStruct((n_rows, d), F32),
        grid_spec=grid_spec,
        compiler_params=_cparams("arbitrary"), name="experts",
    )(block_e, n_valid, xs, wg, wu, wd)


def _combine_kernel(dest_ref, x_ref, r_ref, modp_ref, *rest, final, tile_of):
    if final:
        g_ref, ys_hbm, o_ref, ybuf, sem = rest
    else:
        ys_hbm, o_ref, ybuf, sem = rest
    i = pl.program_id(0)
    n = pl.num_programs(0)
    slot = i % 2

    def issue(step, sl):
        base = tile_of(step) * ROW_TILE

        def body(j, c):
            tok = base + j
            for k in range(2):
                pltpu.make_async_copy(ys_hbm.at[pl.ds(dest_ref[2 * tok + k], 1)], ybuf.at[sl, k, pl.ds(j, 1)],
                                      sem.at[sl]).start()
            return c
        lax.fori_loop(0, ROW_TILE, body, 0, unroll=8)

    @pl.when(i == 0)
    def _():
        issue(0, 0)

    @pl.when(i + 1 < n)
    def _():
        issue(i + 1, 1 - slot)

    for k in range(2):
        pltpu.make_async_copy(ys_hbm.at[pl.ds(0, ROW_TILE)], ybuf.at[slot, k], sem.at[slot]).wait()
    r = r_ref[...]
    y = ybuf[slot, 0] * r[:, 2:3] + ybuf[slot, 1] * r[:, 3:4]
    x = x_ref[...] + modp_ref[0, 5:6, :] * y
    o_ref[...] = _rms(x, g_ref[...]) if final else x


def _combine(rt, x, route, mods, ys, dest, final_g=None, n_len=None):
    d = x.shape[1]
    final = final_g is not None
    if final:
        lt = n_len // ROW_TILE
        steps, out_rows = rt.b * lt, rt.b * n_len

        def tile_of(i):
            return (i // lt) * rt.tpb + rt.ctx_tiles + i % lt
    else:
        steps, out_rows = rt.n_tiles, rt.rows

        def tile_of(i):
            return i

    in_specs = [pl.BlockSpec((ROW_TILE, d), lambda i, de: (tile_of(i), 0)),
                pl.BlockSpec((ROW_TILE, LANES), lambda i, de: (tile_of(i), 0)),
                pl.BlockSpec((1, 8, d), lambda i, de: (rt.mod_idx(tile_of(i)), 0, 0))]
    args = [x, route, mods]
    if final:
        in_specs.append(pl.BlockSpec((1, d), lambda i, de: (0, 0)))
        args.append(final_g.reshape(1, d))
    in_specs.append(pl.BlockSpec(memory_space=pl.ANY))
    args.append(ys)
    grid_spec = pltpu.PrefetchScalarGridSpec(
        num_scalar_prefetch=1, grid=(steps,), in_specs=in_specs,
        out_specs=pl.BlockSpec((ROW_TILE, d), lambda i, de: (i, 0)),
        scratch_shapes=[pltpu.VMEM((2, 2, ROW_TILE, d), F32), pltpu.SemaphoreType.DMA((2,))])
    return pl.pallas_call(
        functools.partial(_combine_kernel, final=final, tile_of=tile_of),
        out_shape=jax.ShapeDtypeStruct((out_rows, d), F32),
        grid_spec=grid_spec,
        compiler_params=_cparams("arbitrary"), name="combine_final" if final else "combine",
    )(dest, *args)


def _pre_mla_kernel(x_ref, mod_ref, g_ref, win_ref, qg_ref, kvg_ref,
                    wq_ref, wqs_ref, wk_ref, wv_ref, ct_ref, st_ref, q_ref, k_ref, v_ref):
    h = _modulated(x_ref[...], g_ref, mod_ref, 0)
    p = _dot(h.astype(BF16), win_ref[...])
    cq = _rms(p[:, 0:MLA_Q_RANK], qg_ref[...]).astype(BF16)
    ckv = _rms(p[:, MLA_Q_RANK:MLA_Q_RANK + MLA_KV_RANK], kvg_ref[...]).astype(BF16)
    off = MLA_Q_RANK + MLA_KV_RANK
    ct, st = ct_ref[...], st_ref[...]
    k_rope = p[:, off:off + HEAD_PAD] * ct + p[:, off + HEAD_PAD:off + 2 * HEAD_PAD] * st
    qa = _dot(cq, wq_ref[...])
    qb = _dot(cq, wqs_ref[...])
    kn = _dot(ckv, wk_ref[...])
    v_ref[...] = _dot(ckv, wv_ref[...]).astype(v_ref.dtype)
    for hh in range(MLA_HEADS):
        sl = slice(hh * HEAD_PAD, (hh + 1) * HEAD_PAD)
        q_ref[:, sl] = ((qa[:, sl] * ct + qb[:, sl] * st) * MLA_SCALE).astype(q_ref.dtype)
        k_ref[:, sl] = (kn[:, sl] + k_rope).astype(k_ref.dtype)


def _pre_mla(rt, x, mods, g1, wts, ct, st):
    d = x.shape[1]
    w_in, qg, kvg, wq, wqs, wk, wv = wts

    def full(a):
        return pl.BlockSpec(a.shape, lambda i: (0,) * a.ndim)

    hq = MLA_HEADS * HEAD_PAD
    hv = MLA_HEADS * MLA_V
    return pl.pallas_call(
        _pre_mla_kernel,
        out_shape=[jax.ShapeDtypeStruct((rt.rows, hq), BF16),
                   jax.ShapeDtypeStruct((rt.rows, hq), BF16), jax.ShapeDtypeStruct((rt.rows, hv), BF16)],
        grid=(rt.n_tiles,),
        in_specs=[
            pl.BlockSpec((ROW_TILE, d), lambda i: (i, 0)),
            pl.BlockSpec((1, 8, d), lambda i: (rt.mod_idx(i), 0, 0)),
            pl.BlockSpec((1, d), lambda i: (0, 0)),
            full(w_in), full(qg), full(kvg), full(wq), full(wqs), full(wk), full(wv),
            pl.BlockSpec((ROW_TILE, HEAD_PAD), lambda i: (rt.pos_idx(i), 0)),
            pl.BlockSpec((ROW_TILE, HEAD_PAD), lambda i: (rt.pos_idx(i), 0))],
        out_specs=[pl.BlockSpec((ROW_TILE, hq), lambda i: (i, 0)),
                   pl.BlockSpec((ROW_TILE, hq), lambda i: (i, 0)),
                   pl.BlockSpec((ROW_TILE, hv), lambda i: (i, 0))],
        compiler_params=_cparams("arbitrary"), name="pre_mla",
    )(x, mods, g1.reshape(1, d), w_in, qg, kvg, wq, wqs, wk, wv, ct, st)


def _mla_params(w_in, q_g, kv_g, w_uq, w_ukv):
    d = w_in.shape[0]
    hp, hr = HEAD_PAD, MLA_ROPE // 2
    nq = MLA_NOPE + MLA_ROPE
    kr = w_in[:, MLA_Q_RANK + MLA_KV_RANK:]
    z = jnp.zeros((d, MLA_NOPE), F32)
    zt = jnp.zeros((d, hp - nq), F32)
    kr_a = jnp.concatenate([z, kr, zt], axis=1)
    kr_b = jnp.concatenate([z, -kr[:, hr:], kr[:, :hr], zt], axis=1)
    w_in_p = jnp.concatenate([w_in[:, :MLA_Q_RANK + MLA_KV_RANK], kr_a, kr_b], axis=1).astype(BF16)
    wq = w_uq.reshape(MLA_Q_RANK, MLA_HEADS, nq)
    zq = jnp.zeros((MLA_Q_RANK, MLA_HEADS, hp - nq), F32)
    wq_a = jnp.concatenate([wq, zq], axis=2).reshape(MLA_Q_RANK, MLA_HEADS * hp).astype(BF16)
    wq_b = jnp.concatenate([jnp.zeros_like(wq[:, :, :MLA_NOPE]), -wq[:, :, MLA_NOPE + hr:],
                            wq[:, :, MLA_NOPE:MLA_NOPE + hr], zq], axis=2)
    wq_b = wq_b.reshape(MLA_Q_RANK, MLA_HEADS * hp).astype(BF16)
    wkv = w_ukv.reshape(MLA_KV_RANK, MLA_HEADS, MLA_NOPE + MLA_V)
    wk = jnp.concatenate([wkv[:, :, :MLA_NOPE], jnp.zeros((MLA_KV_RANK, MLA_HEADS, hp - MLA_NOPE), F32)], axis=2)
    wk = wk.reshape(MLA_KV_RANK, MLA_HEADS * hp).astype(BF16)
    wv = wkv[:, :, MLA_NOPE:].reshape(MLA_KV_RANK, MLA_HEADS * MLA_V).astype(BF16)
    return (w_in_p, q_g.reshape(1, -1), kv_g.reshape(1, -1), wq_a, wq_b, wk, wv)


def _attn_kernel(q_ref, k_ref, v_ref, o_ref, *, s_len, l_len):
    t = pl.program_id(2)
    lane = lax.broadcasted_iota(jnp.int32, (ATT_TQ, 2 * MLA_V), 1)

    def attend(nk):
        for pair in range(ATT_HEADS // 2):
            vv = v_ref[0, 0:nk, pair * 2 * MLA_V:(pair + 1) * 2 * MLA_V]
            outs = []
            for j in (2 * pair, 2 * pair + 1):
                q = q_ref[0, :, j * HEAD_PAD:(j + 1) * HEAD_PAD]
                k = k_ref[0, 0:nk, j * HEAD_PAD:(j + 1) * HEAD_PAD]
                s = _dot_nt(q, k)
                p = jnp.exp(s - jnp.max(s, axis=1, keepdims=True))
                den = jnp.sum(p, axis=1, keepdims=True)
                outs.append(_dot(p.astype(BF16), vv) / den)
            o_ref[0, :, pair * 2 * MLA_V:(pair + 1) * 2 * MLA_V] = (
                jnp.where(lane < MLA_V, outs[0], outs[1]).astype(o_ref.dtype))

    ctx_tiles = l_len // ATT_TQ

    @pl.when(t < ctx_tiles)
    def _():
        attend(l_len)

    @pl.when(t >= ctx_tiles)
    def _():
        attend(s_len)


def _attention(q3, k3, v3, l_len):
    b, s, _ = q3.shape
    hq = ATT_HEADS * HEAD_PAD
    hv = ATT_HEADS * MLA_V
    return pl.pallas_call(
        functools.partial(_attn_kernel, s_len=s, l_len=l_len),
        out_shape=jax.ShapeDtypeStruct((b, s, MLA_HEADS * MLA_V), BF16),
        grid=(b, MLA_HEADS // ATT_HEADS, s // ATT_TQ),
        in_specs=[pl.BlockSpec((1, ATT_TQ, hq), lambda bi, hi, ti: (bi, ti, hi)),
                  pl.BlockSpec((1, s, hq), lambda bi, hi, ti: (bi, 0, hi)),
                  pl.BlockSpec((1, s, hv), lambda bi, hi, ti: (bi, 0, hi))],
        out_specs=pl.BlockSpec((1, ATT_TQ, hv), lambda bi, hi, ti: (bi, ti, hi)),
        compiler_params=_cparams("arbitrary", "arbitrary", "arbitrary"), name="attention",
    )(q3, k3, v3)


def _ret_tables(n, l):
    t = jnp.arange(n, dtype=F32)
    inv = ROPE_BASE ** (-jnp.arange(0, RET_DK, 2, dtype=F32) / RET_DK)
    ang = t[:, None] * inv[None, :]
    cos, sin = jnp.cos(ang), jnp.sin(ang)
    cos2 = jnp.concatenate([jnp.ones((l, RET_DK), F32), jnp.concatenate([cos, cos], axis=1)], axis=0)
    sin2 = jnp.concatenate([jnp.zeros((l, RET_DK), F32), jnp.concatenate([-sin, sin], axis=1)], axis=0)
    return cos2, sin2


def _mla_tables(n, l):
    rows = n // GRID_W
    r_pos = jnp.repeat(jnp.arange(rows, dtype=F32), GRID_W)
    c_pos = jnp.tile(jnp.arange(GRID_W, dtype=F32), rows)
    ax = MLA_ROPE // 2
    inv = ROPE_BASE ** (-jnp.arange(0, ax, 2, dtype=F32) / ax)
    ang = jnp.concatenate([r_pos[:, None] * inv[None, :], c_pos[:, None] * inv[None, :]], axis=-1)
    cos, sin = jnp.cos(ang), jnp.sin(ang)
    pad = HEAD_PAD - MLA_NOPE - MLA_ROPE
    ct_l = jnp.concatenate([jnp.ones((n, MLA_NOPE), F32), cos, cos, jnp.zeros((n, pad), F32)], axis=1)
    st_l = jnp.concatenate([jnp.zeros((n, MLA_NOPE), F32), sin, sin, jnp.zeros((n, pad), F32)], axis=1)
    ct_c = jnp.concatenate([jnp.ones((l, MLA_NOPE + MLA_ROPE), F32), jnp.zeros((l, pad), F32)], axis=1)
    return jnp.concatenate([ct_c, ct_l], axis=0), jnp.concatenate([jnp.zeros((l, HEAD_PAD), F32), st_l], axis=0)


def kernel(x, c, ctx, c_ctx, ada_w, ada_b, norm_g, ab_w_in, ab_w_out, ret_decay_logit, lru_conv_w, lru_conv_b, lru_gate_w, lru_gate_b, lru_lambda, mla_w_in, mla_q_norm_g, mla_kv_norm_g, mla_w_uq, mla_w_ukv, mla_w_out, moe_group_w, moe_group_b, moe_expert_w, moe_expert_b, moe_w_gate, moe_w_up, moe_w_down, final_norm_g):
    b, n, d = x.shape
    l = ctx.shape[1]
    s = l + n
    depth = ada_w.shape[0]
    rt = _Rows(b, s, l)

    nrow = (b + 1 + 7) // 8 * 8
    cvec = jnp.concatenate([c, c_ctx[None, :], jnp.zeros((nrow - b - 1, d), F32)], axis=0)
    ada = _ada_all(cvec, ada_w, ada_b)

    def layer_mods(layer):
        lat = ada[layer, :b].reshape(b, 1, 6, d)
        cx = jnp.broadcast_to(ada[layer, b].reshape(1, 1, 6, d), (b, 1, 6, d))
        m = jnp.concatenate([cx, lat], axis=1)
        m = jnp.concatenate([m, jnp.zeros((b, 2, 2, d), F32)], axis=2)
        return m.reshape(2 * b, 8, d)

    cos2, sin2 = _ret_tables(n, l)
    ct, st = _mla_tables(n, l)

    xs = jnp.concatenate([ctx, x], axis=1).reshape(b * s, d)
    out = None
    for layer in range(depth):
        mods = layer_mods(layer)
        i = layer // 2
        if layer % 2 == 0:
            p3 = _pre_ab(rt, xs, mods, norm_g[layer, 0], ab_w_in[i].astype(BF16)).reshape(b, s, -1)
            lg = jax.nn.log_sigmoid(ret_decay_logit[i].astype(F32))
            lgv = jnp.broadcast_to(lg.T[:, :, None], (RET_HEADS, 2, LANES))
            lgv = jnp.concatenate([lgv, jnp.zeros((RET_HEADS, 6, LANES), F32)], axis=1)
            ma = _retention(p3, cos2, sin2, lgv, l).reshape(b * s, -1)
            mb = _rglru(p3, *_lru_params(lru_conv_w[i], lru_conv_b[i], lru_gate_w[i], lru_gate_b[i],
                                         lru_lambda[i]), l).reshape(b * s, -1)
            cb = 0
            w_out = ab_w_out[i].astype(BF16)
        else:
            wts = _mla_params(mla_w_in[i], mla_q_norm_g[i], mla_kv_norm_g[i], mla_w_uq[i], mla_w_ukv[i])
            q, k, v = _pre_mla(rt, xs, mods, norm_g[layer, 0], wts, ct, st)
            att = _attention(q.reshape(b, s, -1), k.reshape(b, s, -1), v.reshape(b, s, -1), l)
            ma = mb = att.reshape(b * s, -1)
            cb = 1
            w_out = mla_w_out[i].astype(BF16)
        wr = jnp.concatenate([moe_group_w[layer], moe_expert_w[layer],
                              jnp.zeros((d, LANES - MOE_GROUPS - MOE_EXPERTS), F32)], axis=1)
        br = jnp.concatenate([moe_group_b[layer], moe_expert_b[layer],
                              jnp.zeros((LANES - MOE_GROUPS - MOE_EXPERTS,), F32)]).reshape(1, LANES)
        xs, h2, route, cnt = _post(rt, xs, ma, mb, cb, w_out, mods, norm_g[layer, 1], wr, br)
        dest, block_e, n_valid, nb = _moe_plan(route, cnt)
        xsort = _dispatch(h2, dest, nb * MOE_ROWS)
        ys = _experts(xsort, block_e, n_valid, layer, moe_w_gate, moe_w_up, moe_w_down)
        if layer + 1 < depth:
            xs = _combine(rt, xs, route, mods, ys, dest)
        else:
            out = _combine(rt, xs, route, mods, ys, dest, final_norm_g, n)
    return out.reshape(b, n, d)
```

```python
import functools

import jax
import jax.numpy as jnp
from jax import lax
from jax.experimental import pallas as pl
from jax.experimental.pallas import tpu as pltpu

F32 = jnp.float32
BF16 = jnp.bfloat16

EPS = 1e-6
ROPE_BASE = 10000.0
GRID_W = 64

RET_HEADS = 4
RET_DK = 128
RET_CHUNK = 128
RET_HPS = 2
LRU_WIDTH = 512
LRU_BLOCK = 64
LRU_C = 8.0
LRU_HALF = 256
LRU_TILE = 128

MLA_HEADS = 16
MLA_NOPE = 64
MLA_ROPE = 32
MLA_V = 64
MLA_Q_RANK = 384
MLA_KV_RANK = 256
MLA_SCALE = (MLA_NOPE + MLA_ROPE) ** -0.5
LOG2_E = 1.4426950408889634
HEAD_PAD = 128

MOE_GROUPS = 4
MOE_PER_GROUP = 8
MOE_EXPERTS = 32
MOE_ROWS = 256

ROW_TILE = 256
ATT_TQ = 256
ATT_HEADS = 4
LANES = 128
VMEM_LIMIT = 56 * 1024 * 1024


def _cparams(*sem):
    return pltpu.CompilerParams(dimension_semantics=sem, vmem_limit_bytes=VMEM_LIMIT)


def _rms(x, g):
    return x * lax.rsqrt(jnp.mean(x * x, axis=-1, keepdims=True) + EPS) * g


def _dot(a, b):
    return jnp.dot(a, b, preferred_element_type=F32)


def _dot_nt(a, b):
    return lax.dot_general(a, b, (((1,), (1,)), ((), ())), preferred_element_type=F32)


def _dot_tn(a, b):
    return lax.dot_general(a, b, (((0,), (0,)), ((), ())), preferred_element_type=F32)


def _ada_kernel(s_ref, w_ref, b_ref, o_ref):
    s = jax.nn.silu(s_ref[...])
    o_ref[0] = _dot(s.astype(BF16), w_ref[0].astype(BF16)) + b_ref[0]


def _ada_all(cvec, ada_w, ada_b):
    depth, d, n6 = ada_w.shape
    rows = cvec.shape[0]
    tn = n6 // 4
    return pl.pallas_call(
        _ada_kernel,
        out_shape=jax.ShapeDtypeStruct((depth, rows, n6), F32),
        grid=(depth, n6 // tn),
        in_specs=[pl.BlockSpec((rows, d), lambda l, j: (0, 0)),
                  pl.BlockSpec((1, d, tn), lambda l, j: (l, 0, j)),
                  pl.BlockSpec((1, 1, tn), lambda l, j: (l, 0, j))],
        out_specs=pl.BlockSpec((1, rows, tn), lambda l, j: (l, 0, j)),
        compiler_params=_cparams("arbitrary", "arbitrary"),
        name="adaln",
    )(cvec, ada_w, ada_b.reshape(depth, 1, n6))


class _Rows:
    def __init__(self, b, s, l):
        assert s % ROW_TILE == 0 and l % ROW_TILE == 0
        self.b, self.s, self.l = b, s, l
        self.tpb = s // ROW_TILE
        self.ctx_tiles = l // ROW_TILE
        self.n_tiles = b * self.tpb
        self.rows = b * s

    def mod_idx(self, i):
        return 2 * (i // self.tpb) + jnp.where(i % self.tpb >= self.ctx_tiles, 1, 0)

    def pos_idx(self, i):
        return i % self.tpb


def _modulated(x, g_ref, mod_ref, base):
    h = _rms(x, g_ref[...])
    return h * (1.0 + mod_ref[0, base + 1:base + 2, :]) + mod_ref[0, base:base + 1, :]


def _pre_ab_kernel(x_ref, mod_ref, g_ref, w_ref, p_ref):
    h = _modulated(x_ref[...], g_ref, mod_ref, 0)
    p_ref[...] = _dot(h.astype(BF16), w_ref[...])


def _pre_ab(rt, x, mods, g1, w_in):
    d = x.shape[1]
    n_out = w_in.shape[1]
    return pl.pallas_call(
        _pre_ab_kernel,
        out_shape=jax.ShapeDtypeStruct((rt.rows, n_out), F32),
        grid=(rt.n_tiles,),
        in_specs=[pl.BlockSpec((ROW_TILE, d), lambda i: (i, 0)),
                  pl.BlockSpec((1, 8, d), lambda i: (rt.mod_idx(i), 0, 0)),
                  pl.BlockSpec((1, d), lambda i: (0, 0)),
                  pl.BlockSpec((d, n_out), lambda i: (0, 0))],
        out_specs=pl.BlockSpec((ROW_TILE, n_out), lambda i: (i, 0)),
        compiler_params=_cparams("arbitrary"), name="pre_ab",
    )(x, mods, g1.reshape(1, d), w_in)


def _ret_kernel(q_ref, k_ref, v_ref, g_ref, cos_ref, sin_ref, lg_ref, o_ref, qs, ks, acc, *, s_len, l_len):
    c, dk = RET_CHUNK, RET_DK
    nch, cch = s_len // c, l_len // c
    ii = lax.broadcasted_iota(jnp.int32, (c, c), 0).astype(F32)
    jj = lax.broadcasted_iota(jnp.int32, (c, c), 1).astype(F32)
    diff = ii - jj
    k_scale = RET_DK ** -0.5

    def head_consts(j):
        lgf = lg_ref[j, 0:1, :]
        lgb = lg_ref[j, 1:2, :]
        dmask = (jnp.where(diff > 0, jnp.exp(lgf * jnp.maximum(diff, 0.0)), 0.0)
                 + jnp.where(diff < 0, jnp.exp(lgb * jnp.maximum(-diff, 0.0)), 0.0)
                 + jnp.where(diff == 0, 2.0, 0.0))
        return dict(dmask=dmask,
                    zeta_f=jnp.exp(lgf * (c - 1.0 - ii)), xi_f=jnp.exp(lgf * (ii + 1.0)),
                    zeta_b=jnp.exp(lgb * ii), xi_b=jnp.exp(lgb * (c - ii)),
                    cd_f=jnp.exp(lgf * c), cd_b=jnp.exp(lgb * c))

    hc = [head_consts(j) for j in range(RET_HPS)]

    def fwd(n, sts):
        rows = pl.ds(pl.multiple_of(n * c, c), c)
        cs, sn = cos_ref[rows, :], sin_ref[rows, :]
        out = []
        for j, st in enumerate(sts):
            cols = slice(j * dk, (j + 1) * dk)
            q = q_ref[0, rows, cols]
            k = k_ref[0, rows, cols]
            v = v_ref[0, rows, cols]
            qb = (q * cs + pltpu.roll(q, 64, 1) * sn).astype(BF16)
            kb = ((k * cs + pltpu.roll(k, 64, 1) * sn) * k_scale).astype(BF16)
            qs[rows, cols] = qb
            ks[rows, cols] = kb
            sc = _dot_nt(qb, kb) * hc[j]["dmask"]
            acc[rows, cols] = _dot(sc.astype(BF16), v.astype(BF16)) + _dot(qb, st.astype(BF16)) * hc[j]["xi_f"]
            out.append(hc[j]["cd_f"] * st + _dot_tn(kb, (v * hc[j]["zeta_f"]).astype(BF16)))
        return tuple(out)

    def bwd(n, sts):
        rows = pl.ds(pl.multiple_of(n * c, c), c)
        out = []
        for j, st in enumerate(sts):
            cols = slice(j * dk, (j + 1) * dk)
            qb = qs[rows, cols]
            kb = ks[rows, cols]
            v = v_ref[0, rows, cols]
            y = acc[rows, cols] + _dot(qb, st.astype(BF16)) * hc[j]["xi_b"]
            y = y * lax.rsqrt(jnp.mean(y * y, axis=-1, keepdims=True) + EPS)
            o_ref[0, rows, cols] = (y * jax.nn.silu(g_ref[0, rows, cols])).astype(o_ref.dtype)
            out.append(hc[j]["cd_b"] * st + _dot_tn(kb, (v * hc[j]["zeta_b"]).astype(BF16)))
        return tuple(out)

    zero = tuple(jnp.zeros((c, c), F32) for _ in range(RET_HPS))
    lax.fori_loop(0, nch, fwd, zero)
    sts = lax.fori_loop(0, cch, lambda t, sts: bwd(cch - 1 - t, sts), zero)
    lax.fori_loop(0, nch - cch, lambda t, sts: bwd(nch - 1 - t, sts), sts)


def _retention(p3, cos2, sin2, lgv, l_len):
    b, s, _ = p3.shape
    groups = RET_HEADS // RET_HPS
    w = RET_HPS * RET_DK

    def col(off, **kw):
        return pl.BlockSpec((1, s, w), lambda bi, hi: (bi, 0, off + hi), **kw)

    once = dict(pipeline_mode=pl.Buffered(1))
    return pl.pallas_call(
        functools.partial(_ret_kernel, s_len=s, l_len=l_len),
        out_shape=jax.ShapeDtypeStruct((b, s, RET_HEADS * RET_DK), BF16),
        grid=(b, groups),
        in_specs=[col(0), col(groups), col(2 * groups), col(3 * groups, **once),
                  pl.BlockSpec((s, RET_DK), lambda bi, hi: (0, 0), **once),
                  pl.BlockSpec((s, RET_DK), lambda bi, hi: (0, 0), **once),
                  pl.BlockSpec((RET_HPS, 8, LANES), lambda bi, hi: (hi, 0, 0))],
        out_specs=pl.BlockSpec((1, s, w), lambda bi, hi: (bi, 0, hi)),
        scratch_shapes=[pltpu.VMEM((s, w), BF16), pltpu.VMEM((s, w), BF16), pltpu.VMEM((s, w), F32)],
        compiler_params=_cparams("arbitrary", "arbitrary"), name="retention",
    )(p3, p3, p3, p3, cos2, sin2, lgv)


def _tile_scan(a, b, reverse):
    n = a.shape[0]
    rows = lax.broadcasted_iota(jnp.int32, a.shape, 0)
    step = 1
    while step < n:
        shift = n - step if reverse else step
        a_s = pltpu.roll(a, shift, 0)
        b_s = pltpu.roll(b, shift, 0)
        m = (rows < n - step) if reverse else (rows >= step)
        b = jnp.where(m, a * b_s + b, b)
        a = jnp.where(m, a * a_s, a)
        step *= 2
    return a, b


def _lru_kernel(x_ref, y_ref, cw_ref, wg_ref, gb_ref, sp_ref, o_ref, xpad, hf, ab, bb, *, s_len, l_len):
    tl, w = LRU_TILE, LRU_HALF
    ntl, ctl = s_len // tl, l_len // tl
    xpad[0:8, :] = jnp.zeros((8, w), F32)
    xpad[s_len + 8:s_len + 16, :] = jnp.zeros((8, w), F32)
    xpad[8:s_len + 8, :] = x_ref[0]
    w0, w1, w2, w3, cb = (cw_ref[0, t:t + 1, :] for t in range(5))
    sp_f = sp_ref[0, 0:1, :]
    sp_b = sp_ref[0, 1:2, :]
    it = lax.broadcasted_iota(jnp.int32, (tl, w), 0)

    def coeff(gr, gi, sp, xc):
        r = jax.nn.sigmoid(gr)
        i = jax.nn.sigmoid(gi)
        log_a = -LRU_C * r * sp
        th = jnp.tanh(log_a)
        return jnp.exp(log_a), jnp.sqrt(-2.0 * th / (1.0 - th)) * (i * xc)

    def fwd(n, carry):
        r0 = pl.multiple_of(n * tl, tl)
        win = xpad[pl.ds(r0, tl + 16), :]
        t = r0 + it
        seg = jnp.where(t >= l_len, 1, 0)

        def tap(d):
            v = pltpu.roll(win, (tl + 16 - d) % (tl + 16), 0)[8:8 + tl]
            return jnp.where(jnp.where(t + d >= l_len, 1, 0) == seg, v, 0.0)

        xc = tap(-2) * w0
        xc = xc + tap(-1) * w1
        xc = xc + win[8:8 + tl] * w2
        xc = xc + tap(1) * w3
        xc = xc + cb
        gts = _dot(xc.astype(BF16), wg_ref[0]) + gb_ref[0]
        a_f, b_f = coeff(gts[:, 0:w], gts[:, w:2 * w], sp_f, xc)
        a_b, b_b = coeff(gts[:, 2 * w:3 * w], gts[:, 3 * w:4 * w], sp_b, xc)
        rows = pl.ds(r0, tl)
        ab[rows, :] = a_b
        bb[rows, :] = b_b
        a_c, h_loc = _tile_scan(a_f, b_f, False)
        hh = h_loc + a_c * carry
        hf[rows, :] = hh
        return hh[tl - 1:tl, :]

    def bwd(n, carry):
        rows = pl.ds(pl.multiple_of(n * tl, tl), tl)
        a_c, h_loc = _tile_scan(ab[rows, :], bb[rows, :], True)
        hh = h_loc + a_c * carry
        o_ref[0, rows, :] = ((hf[rows, :] + hh) * jax.nn.gelu(y_ref[0, rows, :])).astype(o_ref.dtype)
        return hh[0:1, :]

    zero = jnp.zeros((1, w), F32)
    lax.fori_loop(0, ntl, fwd, zero)
    c = lax.fori_loop(0, ctl, lambda t, c: bwd(ctl - 1 - t, c), zero)
    lax.fori_loop(0, ntl - ctl, lambda t, c: bwd(ntl - 1 - t, c), c)


def _rglru(p3, conv_wb, gate_w, gate_b, sp, l_len):
    b, s, _ = p3.shape
    nh = LRU_WIDTH // LRU_HALF
    xoff = (4 * RET_HEADS * RET_DK) // LRU_HALF
    yoff = xoff + nh
    return pl.pallas_call(
        functools.partial(_lru_kernel, s_len=s, l_len=l_len),
        out_shape=jax.ShapeDtypeStruct((b, s, LRU_WIDTH), BF16),
        grid=(b, nh),
        in_specs=[pl.BlockSpec((1, s, LRU_HALF), lambda bi, j: (bi, 0, xoff + j)),
                  pl.BlockSpec((1, s, LRU_HALF), lambda bi, j: (bi, 0, yoff + j)),
                  pl.BlockSpec((1, 8, LRU_HALF), lambda bi, j: (j, 0, 0)),
                  pl.BlockSpec((1, LRU_HALF, 4 * LRU_HALF), lambda bi, j: (j, 0, 0)),
                  pl.BlockSpec((1, 1, 4 * LRU_HALF), lambda bi, j: (j, 0, 0)),
                  pl.BlockSpec((1, 8, LRU_HALF), lambda bi, j: (j, 0, 0))],
        out_specs=pl.BlockSpec((1, s, LRU_HALF), lambda bi, j: (bi, 0, j)),
        scratch_shapes=[pltpu.VMEM((s + 16, LRU_HALF), F32), pltpu.VMEM((s, LRU_HALF), F32),
                        pltpu.VMEM((s, LRU_HALF), F32), pltpu.VMEM((s, LRU_HALF), F32)],
        compiler_params=_cparams("arbitrary", "arbitrary"), name="rglru",
    )(p3, p3, conv_wb, gate_w, gate_b, sp)


def _lru_params(conv_w, conv_b, gate_w, gate_b, lam):
    nh = LRU_WIDTH // LRU_HALF
    bph = LRU_HALF // LRU_BLOCK
    cw = jnp.concatenate([conv_w, conv_b[None, :], jnp.zeros((3, LRU_WIDTH), F32)], axis=0)
    cw = cw.reshape(8, nh, LRU_HALF).transpose(1, 0, 2)
    eye = jnp.eye(bph, dtype=F32)
    gw = gate_w.reshape(2, 2, nh, bph, LRU_BLOCK, LRU_BLOCK)
    dense = jnp.einsum('dgjkio,kl->jkidglo', gw, eye)
    dense = dense.reshape(nh, LRU_HALF, 4 * LRU_HALF).astype(BF16)
    gb = gate_b.reshape(2, 2, nh, LRU_HALF).transpose(2, 0, 1, 3).reshape(nh, 1, 4 * LRU_HALF)
    sp = jax.nn.softplus(-lam.astype(F32)).reshape(2, nh, LRU_HALF).transpose(1, 0, 2)
    sp = jnp.concatenate([sp, jnp.zeros((nh, 6, LRU_HALF), F32)], axis=1)
    return cw, dense, gb, sp


def _post_kernel(x_ref, ma_ref, mb_ref, w_ref, mod_ref, g_ref, wr_ref, br_ref, xo_ref, h_ref, r_ref, cnt_ref):
    m = jnp.concatenate([ma_ref[...], mb_ref[...]], axis=1)
    o = _dot(m, w_ref[...])
    x = x_ref[...] + mod_ref[0, 2:3, :] * o
    xo_ref[...] = x
    h = _modulated(x, g_ref, mod_ref, 3)
    h_ref[...] = h
    h_hi = h.astype(BF16)
    h_lo = (h - h_hi.astype(F32)).astype(BF16)
    part = _dot(h_hi, wr_ref[...])
    lg = part[:, 0:LANES] + part[:, LANES:2 * LANES] + _dot(h_lo, wr_ref[:, 0:LANES]) + br_ref[...]
    lane = lax.broadcasted_iota(jnp.int32, lg.shape, 1)
    lanef = lane.astype(F32)
    ninf = -jnp.inf
    big = float(LANES)
    gl = jnp.where(lane < MOE_GROUPS, lg, ninf)
    gmax = jnp.max(gl, axis=1, keepdims=True)
    g_top = 1.0 / jnp.sum(jnp.exp(gl - gmax), axis=1, keepdims=True)
    g_sel = jnp.min(jnp.where(gl == gmax, lanef, big), axis=1, keepdims=True)
    lo = MOE_GROUPS + MOE_PER_GROUP * g_sel
    el = jnp.where((lanef >= lo) & (lanef < lo + MOE_PER_GROUP), lg, ninf)
    emax = jnp.max(el, axis=1, keepdims=True)
    esum = jnp.sum(jnp.exp(el - emax), axis=1, keepdims=True)
    i1 = jnp.min(jnp.where(el == emax, lanef, big), axis=1, keepdims=True)
    el2 = jnp.where(lanef == i1, ninf, el)
    m2 = jnp.max(el2, axis=1, keepdims=True)
    i2 = jnp.min(jnp.where(el2 == m2, lanef, big), axis=1, keepdims=True)
    p1 = 1.0 / esum
    p2 = jnp.exp(m2 - emax) / esum
    tot = p1 + p2
    w1 = g_top * (p1 / tot)
    w2 = g_top * (p2 / tot)
    hit1 = lanef == i1
    hit2 = lanef == i2
    onehot = jnp.where(hit1, 1.0, 0.0) + jnp.where(hit2, 1.0, 0.0)
    ti = lax.broadcasted_iota(jnp.int32, (ROW_TILE, ROW_TILE), 0)
    tj = lax.broadcasted_iota(jnp.int32, (ROW_TILE, ROW_TILE), 1)
    earlier = jnp.where(tj < ti, 1.0, 0.0).astype(BF16)

    @pl.when(pl.program_id(0) == 0)
    def _():
        cnt_ref[...] = jnp.zeros(cnt_ref.shape, F32)

    before = _dot(earlier, onehot.astype(BF16)) + cnt_ref[0:1, :]
    k1 = jnp.sum(jnp.where(hit1, before, 0.0), axis=1, keepdims=True)
    k2 = jnp.sum(jnp.where(hit2, before, 0.0), axis=1, keepdims=True)
    cnt_ref[0:1, :] = cnt_ref[0:1, :] + jnp.sum(onehot, axis=0, keepdims=True)
    vals = (i1 - MOE_GROUPS, i2 - MOE_GROUPS, w1, w2, k1, k2)
    slab = jnp.zeros(lg.shape, F32)
    for col, v in enumerate(vals):
        slab = jnp.where(lane == col, v, slab)
    r_ref[...] = slab


def _post(rt, x, ma, mb, cb, w_out, mods, g2, wr, br):
    d = x.shape[1]
    hd = d // 2
    return pl.pallas_call(
        _post_kernel,
        out_shape=[jax.ShapeDtypeStruct((rt.rows, d), F32), jax.ShapeDtypeStruct((rt.rows, d), F32),
                   jax.ShapeDtypeStruct((rt.rows, LANES), F32), jax.ShapeDtypeStruct((8, LANES), F32)],
        grid=(rt.n_tiles,),
        in_specs=[pl.BlockSpec((ROW_TILE, d), lambda i: (i, 0)),
                  pl.BlockSpec((ROW_TILE, hd), lambda i: (i, 0)),
                  pl.BlockSpec((ROW_TILE, hd), lambda i: (i, cb)),
                  pl.BlockSpec((d, d), lambda i: (0, 0)),
                  pl.BlockSpec((1, 8, d), lambda i: (rt.mod_idx(i), 0, 0)),
                  pl.BlockSpec((1, d), lambda i: (0, 0)),
                  pl.BlockSpec((d, 2 * LANES), lambda i: (0, 0)),
                  pl.BlockSpec((1, LANES), lambda i: (0, 0))],
        out_specs=[pl.BlockSpec((ROW_TILE, d), lambda i: (i, 0)),
                   pl.BlockSpec((ROW_TILE, d), lambda i: (i, 0)),
                   pl.BlockSpec((ROW_TILE, LANES), lambda i: (i, 0)),
                   pl.BlockSpec((8, LANES), lambda i: (0, 0))],
        compiler_params=_cparams("arbitrary"), name="post",
    )(x, ma, mb, w_out, mods, g2.reshape(1, d), wr, br)


def _moe_plan(route, cnt):
    mb = MOE_ROWS
    t_count = route.shape[0]
    nb = (2 * t_count + MOE_EXPERTS * (mb - 1) + mb - 1) // mb
    counts = cnt[0, MOE_GROUPS:MOE_GROUPS + MOE_EXPERTS].astype(jnp.int32)
    padded = (counts + mb - 1) // mb * mb
    pend = jnp.cumsum(padded)
    pstart = pend - padded
    experts = jnp.arange(MOE_EXPERTS, dtype=jnp.int32)
    e = route[:, 0:2].astype(jnp.int32)
    first = jnp.sum(jnp.where(e[:, :, None] == experts[None, None, :], pstart[None, None, :], 0), axis=-1)
    dest = (first + route[:, 4:6].astype(jnp.int32)).reshape(-1)
    blk0 = jnp.arange(nb, dtype=jnp.int32) * mb
    block_e = jnp.minimum(jnp.sum((blk0[:, None] >= pend[None, :]).astype(jnp.int32), axis=1), MOE_EXPERTS - 1)
    sel = block_e[:, None] == experts[None, :]
    used = blk0 - jnp.sum(jnp.where(sel, pstart[None, :], 0), axis=1)
    n_valid = jnp.clip(jnp.sum(jnp.where(sel, counts[None, :], 0), axis=1) - used, 0, mb).astype(jnp.int32)
    fill = jnp.concatenate([pstart + counts, padded - counts, pend[-1:], nb - pend[-1:] // mb])
    return dest, block_e, n_valid, fill.astype(jnp.int32), nb


def _dispatch_kernel(dest_ref, fill_ref, h_ref, xs_hbm, stage, zbuf, sem, zsems, *, nt):
    i = pl.program_id(0)
    slot = i % 2
    zsem = zsems.at[0]

    def wait_tile(sl):
        for _ in range(2):
            pltpu.make_async_copy(stage.at[sl], xs_hbm.at[pl.ds(0, ROW_TILE)], sem.at[sl]).wait()

    def zero_padding(wait):
        def go(cp):
            cp.wait() if wait else cp.start()

        def one_row(r):
            go(pltpu.make_async_copy(zbuf.at[pl.ds(0, 1)], xs_hbm.at[pl.ds(r, 1)], zsem))

        def per_expert(e, c):
            start = fill_ref[e]
            n = fill_ref[MOE_EXPERTS + e]
            head = jnp.minimum((8 - (start & 7)) & 7, n)
            mid = pl.multiple_of(lax.shift_left(lax.shift_right_logical(n - head, 3), 3), 8)
            lax.fori_loop(0, head, lambda j, c2: (one_row(start + j), c2)[1], 0)

            @pl.when(mid > 0)
            def _():
                at = pl.multiple_of(start + head, 8)
                go(pltpu.make_async_copy(zbuf.at[pl.ds(0, mid)], xs_hbm.at[pl.ds(at, mid)], zsem))

            lax.fori_loop(0, n - head - mid, lambda j, c2: (one_row(start + head + mid + j), c2)[1], 0)
            return c
        lax.fori_loop(0, MOE_EXPERTS, per_expert, 0)

        def per_block(j, c):
            at = pl.multiple_of(fill_ref[2 * MOE_EXPERTS] + j * MOE_ROWS, MOE_ROWS)
            go(pltpu.make_async_copy(zbuf, xs_hbm.at[pl.ds(at, MOE_ROWS)], zsem))
            return c
        lax.fori_loop(0, fill_ref[2 * MOE_EXPERTS + 1], per_block, 0)

    @pl.when(i == 0)
    def _():
        zbuf[...] = jnp.zeros(zbuf.shape, zbuf.dtype)
        zero_padding(False)

    @pl.when(i >= 2)
    def _():
        wait_tile(slot)

    stage[slot] = h_ref[...]

    def body(j, c):
        tok = i * ROW_TILE + j
        src = stage.at[slot, pl.ds(j, 1)]
        pltpu.make_async_copy(src, xs_hbm.at[pl.ds(dest_ref[2 * tok], 1)], sem.at[slot]).start()
        pltpu.make_async_copy(src, xs_hbm.at[pl.ds(dest_ref[2 * tok + 1], 1)], sem.at[slot]).start()
        return c
    lax.fori_loop(0, ROW_TILE, body, 0, unroll=8)

    @pl.when(i == nt - 1)
    def _():
        wait_tile(slot)
        if nt > 1:
            wait_tile(1 - slot)
        zero_padding(True)


def _dispatch(h, dest, fill, n_rows):
    t_count, d = h.shape
    nt = t_count // ROW_TILE
    assert MOE_ROWS <= ROW_TILE
    grid_spec = pltpu.PrefetchScalarGridSpec(
        num_scalar_prefetch=2, grid=(nt,),
        in_specs=[pl.BlockSpec((ROW_TILE, d), lambda i, de, fi: (i, 0))],
        out_specs=pl.BlockSpec(memory_space=pl.ANY),
        scratch_shapes=[pltpu.VMEM((2, ROW_TILE, d), F32), pltpu.VMEM((MOE_ROWS, d), F32),
                        pltpu.SemaphoreType.DMA((2,)), pltpu.SemaphoreType.DMA((1,))])
    return pl.pallas_call(
        functools.partial(_dispatch_kernel, nt=nt),
        out_shape=jax.ShapeDtypeStruct((n_rows, d), F32),
        grid_spec=grid_spec,
        compiler_params=_cparams("arbitrary"), name="dispatch",
    )(dest, fill, h)


def _expert_kernel(be_ref, nv_ref, x_ref, wg_ref, wu_ref, wd_ref, y_ref, wgb, wub, wdb):
    i = pl.program_id(0)

    @pl.when(nv_ref[i] > 0)
    def _():
        @pl.when((i == 0) | (be_ref[i] != be_ref[jnp.maximum(i - 1, 0)]))
        def _():
            wgb[...] = wg_ref[0, 0].astype(BF16)
            wub[...] = wu_ref[0, 0].astype(BF16)
            wdb[...] = wd_ref[0, 0].astype(BF16)

        x = x_ref[...].astype(BF16)
        a = (jax.nn.silu(_dot(x, wgb[...])) * _dot(x, wub[...])).astype(BF16)
        y_ref[...] = _dot(a, wdb[...])

    @pl.when(nv_ref[i] == 0)
    def _():
        y_ref[...] = jnp.zeros(y_ref.shape, y_ref.dtype)


def _experts(xs, block_e, n_valid, layer, wg, wu, wd):
    n_rows, d = xs.shape
    mb = MOE_ROWS
    hid = wg.shape[3]

    def wspec(shape):
        return pl.BlockSpec(shape, lambda i, be, nv: (layer, be[i], 0, 0))

    grid_spec = pltpu.PrefetchScalarGridSpec(
        num_scalar_prefetch=2, grid=(n_rows // mb,),
        in_specs=[pl.BlockSpec((mb, d), lambda i, be, nv: (i, 0)),
                  wspec((1, 1, d, hid)), wspec((1, 1, d, hid)), wspec((1, 1, hid, d))],
        out_specs=pl.BlockSpec((mb, d), lambda i, be, nv: (i, 0)),
        scratch_shapes=[pltpu.VMEM((d, hid), BF16), pltpu.VMEM((d, hid), BF16), pltpu.VMEM((hid, d), BF16)])
    return pl.pallas_call(
        _expert_kernel,
        out_shape=jax.ShapeDtypeStruct((n_rows, d), F32),
        grid_spec=grid_spec,
        compiler_params=_cparams("arbitrary"), name="experts",
    )(block_e, n_valid, xs, wg, wu, wd)


def _combine_kernel(dest_ref, x_ref, r_ref, modp_ref, *rest, final, tile_of):
    if final:
        g_ref, ys_hbm, o_ref, ybuf, sem = rest
    else:
        ys_hbm, o_ref, ybuf, sem = rest
    i = pl.program_id(0)
    n = pl.num_programs(0)
    slot = i % 2

    def issue(step, sl):
        base = tile_of(step) * ROW_TILE

        def body(j, c):
            tok = base + j
            for k in range(2):
                pltpu.make_async_copy(ys_hbm.at[pl.ds(dest_ref[2 * tok + k], 1)], ybuf.at[sl, k, pl.ds(j, 1)],
                                      sem.at[sl]).start()
            return c
        lax.fori_loop(0, ROW_TILE, body, 0, unroll=8)

    @pl.when(i == 0)
    def _():
        issue(0, 0)

    @pl.when(i + 1 < n)
    def _():
        issue(i + 1, 1 - slot)

    for k in range(2):
        pltpu.make_async_copy(ys_hbm.at[pl.ds(0, ROW_TILE)], ybuf.at[slot, k], sem.at[slot]).wait()
    r = r_ref[...]
    y = ybuf[slot, 0] * r[:, 2:3] + ybuf[slot, 1] * r[:, 3:4]
    x = x_ref[...] + modp_ref[0, 5:6, :] * y
    o_ref[...] = _rms(x, g_ref[...]) if final else x


def _combine(rt, x, route, mods, ys, dest, final_g=None, n_len=None):
    d = x.shape[1]
    final = final_g is not None
    if final:
        lt = n_len // ROW_TILE
        steps, out_rows = rt.b * lt, rt.b * n_len

        def tile_of(i):
            return (i // lt) * rt.tpb + rt.ctx_tiles + i % lt
    else:
        steps, out_rows = rt.n_tiles, rt.rows

        def tile_of(i):
            return i

    in_specs = [pl.BlockSpec((ROW_TILE, d), lambda i, de: (tile_of(i), 0)),
                pl.BlockSpec((ROW_TILE, LANES), lambda i, de: (tile_of(i), 0)),
                pl.BlockSpec((1, 8, d), lambda i, de: (rt.mod_idx(tile_of(i)), 0, 0))]
    args = [x, route, mods]
    if final:
        in_specs.append(pl.BlockSpec((1, d), lambda i, de: (0, 0)))
        args.append(final_g.reshape(1, d))
    in_specs.append(pl.BlockSpec(memory_space=pl.ANY))
    args.append(ys)
    grid_spec = pltpu.PrefetchScalarGridSpec(
        num_scalar_prefetch=1, grid=(steps,), in_specs=in_specs,
        out_specs=pl.BlockSpec((ROW_TILE, d), lambda i, de: (i, 0)),
        scratch_shapes=[pltpu.VMEM((2, 2, ROW_TILE, d), F32), pltpu.SemaphoreType.DMA((2,))])
    return pl.pallas_call(
        functools.partial(_combine_kernel, final=final, tile_of=tile_of),
        out_shape=jax.ShapeDtypeStruct((out_rows, d), F32),
        grid_spec=grid_spec,
        compiler_params=_cparams("arbitrary"), name="combine_final" if final else "combine",
    )(dest, *args)


def _pre_mla_kernel(x_ref, mod_ref, g_ref, win_ref, qg_ref, kvg_ref,
                    wq_ref, wqs_ref, wk_ref, wv_ref, ct_ref, st_ref, q_ref, k_ref, v_ref):
    h = _modulated(x_ref[...], g_ref, mod_ref, 0)
    p = _dot(h.astype(BF16), win_ref[...])
    cq = _rms(p[:, 0:MLA_Q_RANK], qg_ref[...]).astype(BF16)
    ckv = _rms(p[:, MLA_Q_RANK:MLA_Q_RANK + MLA_KV_RANK], kvg_ref[...]).astype(BF16)
    off = MLA_Q_RANK + MLA_KV_RANK
    ct, st = ct_ref[...], st_ref[...]
    k_rope = p[:, off:off + HEAD_PAD] * ct + p[:, off + HEAD_PAD:off + 2 * HEAD_PAD] * st
    qa = _dot(cq, wq_ref[...])
    qb = _dot(cq, wqs_ref[...])
    kn = _dot(ckv, wk_ref[...])
    v_ref[...] = _dot(ckv, wv_ref[...]).astype(v_ref.dtype)
    for hh in range(MLA_HEADS):
        sl = slice(hh * HEAD_PAD, (hh + 1) * HEAD_PAD)
        q_ref[:, sl] = ((qa[:, sl] * ct + qb[:, sl] * st) * (MLA_SCALE * LOG2_E)).astype(q_ref.dtype)
        k_ref[:, sl] = (kn[:, sl] + k_rope).astype(k_ref.dtype)


def _pre_mla(rt, x, mods, g1, wts, ct, st):
    d = x.shape[1]
    w_in, qg, kvg, wq, wqs, wk, wv = wts

    def full(a):
        return pl.BlockSpec(a.shape, lambda i: (0,) * a.ndim)

    hq = MLA_HEADS * HEAD_PAD
    hv = MLA_HEADS * MLA_V
    return pl.pallas_call(
        _pre_mla_kernel,
        out_shape=[jax.ShapeDtypeStruct((rt.rows, hq), BF16),
                   jax.ShapeDtypeStruct((rt.rows, hq), BF16), jax.ShapeDtypeStruct((rt.rows, hv), BF16)],
        grid=(rt.n_tiles,),
        in_specs=[
            pl.BlockSpec((ROW_TILE, d), lambda i: (i, 0)),
            pl.BlockSpec((1, 8, d), lambda i: (rt.mod_idx(i), 0, 0)),
            pl.BlockSpec((1, d), lambda i: (0, 0)),
            full(w_in), full(qg), full(kvg), full(wq), full(wqs), full(wk), full(wv),
            pl.BlockSpec((ROW_TILE, HEAD_PAD), lambda i: (rt.pos_idx(i), 0)),
            pl.BlockSpec((ROW_TILE, HEAD_PAD), lambda i: (rt.pos_idx(i), 0))],
        out_specs=[pl.BlockSpec((ROW_TILE, hq), lambda i: (i, 0)),
                   pl.BlockSpec((ROW_TILE, hq), lambda i: (i, 0)),
                   pl.BlockSpec((ROW_TILE, hv), lambda i: (i, 0))],
        compiler_params=_cparams("arbitrary"), name="pre_mla",
    )(x, mods, g1.reshape(1, d), w_in, qg, kvg, wq, wqs, wk, wv, ct, st)


def _mla_params(w_in, q_g, kv_g, w_uq, w_ukv):
    d = w_in.shape[0]
    hp, hr = HEAD_PAD, MLA_ROPE // 2
    nq = MLA_NOPE + MLA_ROPE
    kr = w_in[:, MLA_Q_RANK + MLA_KV_RANK:]
    z = jnp.zeros((d, MLA_NOPE), F32)
    zt = jnp.zeros((d, hp - nq), F32)
    kr_a = jnp.concatenate([z, kr, zt], axis=1)
    kr_b = jnp.concatenate([z, -kr[:, hr:], kr[:, :hr], zt], axis=1)
    w_in_p = jnp.concatenate([w_in[:, :MLA_Q_RANK + MLA_KV_RANK], kr_a, kr_b], axis=1).astype(BF16)
    wq = w_uq.reshape(MLA_Q_RANK, MLA_HEADS, nq)
    zq = jnp.zeros((MLA_Q_RANK, MLA_HEADS, hp - nq), F32)
    wq_a = jnp.concatenate([wq, zq], axis=2).reshape(MLA_Q_RANK, MLA_HEADS * hp).astype(BF16)
    wq_b = jnp.concatenate([jnp.zeros_like(wq[:, :, :MLA_NOPE]), -wq[:, :, MLA_NOPE + hr:],
                            wq[:, :, MLA_NOPE:MLA_NOPE + hr], zq], axis=2)
    wq_b = wq_b.reshape(MLA_Q_RANK, MLA_HEADS * hp).astype(BF16)
    wkv = w_ukv.reshape(MLA_KV_RANK, MLA_HEADS, MLA_NOPE + MLA_V)
    wk = jnp.concatenate([wkv[:, :, :MLA_NOPE], jnp.zeros((MLA_KV_RANK, MLA_HEADS, hp - MLA_NOPE), F32)], axis=2)
    wk = wk.reshape(MLA_KV_RANK, MLA_HEADS * hp).astype(BF16)
    wv = wkv[:, :, MLA_NOPE:].reshape(MLA_KV_RANK, MLA_HEADS * MLA_V).astype(BF16)
    return (w_in_p, q_g.reshape(1, -1), kv_g.reshape(1, -1), wq_a, wq_b, wk, wv)


def _attn_kernel(q_ref, k_ref, v_ref, o_ref, *, s_len, l_len):
    t = pl.program_id(2)
    lane = lax.broadcasted_iota(jnp.int32, (ATT_TQ, 2 * MLA_V), 1)

    def attend(nk):
        for pair in range(ATT_HEADS // 2):
            vv = v_ref[0, 0:nk, pair * 2 * MLA_V:(pair + 1) * 2 * MLA_V]
            outs = []
            for j in (2 * pair, 2 * pair + 1):
                q = q_ref[0, :, j * HEAD_PAD:(j + 1) * HEAD_PAD]
                k = k_ref[0, 0:nk, j * HEAD_PAD:(j + 1) * HEAD_PAD]
                s = _dot_nt(q, k)
                p = jnp.exp2(s - jnp.max(s, axis=1, keepdims=True))
                den = jnp.sum(p, axis=1, keepdims=True)
                outs.append(_dot(p.astype(BF16), vv) / den)
            o_ref[0, :, pair * 2 * MLA_V:(pair + 1) * 2 * MLA_V] = (
                jnp.where(lane < MLA_V, outs[0], outs[1]).astype(o_ref.dtype))

    ctx_tiles = l_len // ATT_TQ

    @pl.when(t < ctx_tiles)
    def _():
        attend(l_len)

    @pl.when(t >= ctx_tiles)
    def _():
        attend(s_len)


def _attention(q3, k3, v3, l_len):
    b, s, _ = q3.shape
    hq = ATT_HEADS * HEAD_PAD
    hv = ATT_HEADS * MLA_V
    return pl.pallas_call(
        functools.partial(_attn_kernel, s_len=s, l_len=l_len),
        out_shape=jax.ShapeDtypeStruct((b, s, MLA_HEADS * MLA_V), BF16),
        grid=(b, MLA_HEADS // ATT_HEADS, s // ATT_TQ),
        in_specs=[pl.BlockSpec((1, ATT_TQ, hq), lambda bi, hi, ti: (bi, ti, hi)),
                  pl.BlockSpec((1, s, hq), lambda bi, hi, ti: (bi, 0, hi)),
                  pl.BlockSpec((1, s, hv), lambda bi, hi, ti: (bi, 0, hi))],
        out_specs=pl.BlockSpec((1, ATT_TQ, hv), lambda bi, hi, ti: (bi, ti, hi)),
        compiler_params=_cparams("arbitrary", "arbitrary", "arbitrary"), name="attention",
    )(q3, k3, v3)


def _ret_tables(n, l):
    t = jnp.arange(n, dtype=F32)
    inv = ROPE_BASE ** (-jnp.arange(0, RET_DK, 2, dtype=F32) / RET_DK)
    ang = t[:, None] * inv[None, :]
    cos, sin = jnp.cos(ang), jnp.sin(ang)
    cos2 = jnp.concatenate([jnp.ones((l, RET_DK), F32), jnp.concatenate([cos, cos], axis=1)], axis=0)
    sin2 = jnp.concatenate([jnp.zeros((l, RET_DK), F32), jnp.concatenate([-sin, sin], axis=1)], axis=0)
    return cos2, sin2


def _mla_tables(n, l):
    rows = n // GRID_W
    r_pos = jnp.repeat(jnp.arange(rows, dtype=F32), GRID_W)
    c_pos = jnp.tile(jnp.arange(GRID_W, dtype=F32), rows)
    ax = MLA_ROPE // 2
    inv = ROPE_BASE ** (-jnp.arange(0, ax, 2, dtype=F32) / ax)
    ang = jnp.concatenate([r_pos[:, None] * inv[None, :], c_pos[:, None] * inv[None, :]], axis=-1)
    cos, sin = jnp.cos(ang), jnp.sin(ang)
    pad = HEAD_PAD - MLA_NOPE - MLA_ROPE
    ct_l = jnp.concatenate([jnp.ones((n, MLA_NOPE), F32), cos, cos, jnp.zeros((n, pad), F32)], axis=1)
    st_l = jnp.concatenate([jnp.zeros((n, MLA_NOPE), F32), sin, sin, jnp.zeros((n, pad), F32)], axis=1)
    ct_c = jnp.concatenate([jnp.ones((l, MLA_NOPE + MLA_ROPE), F32), jnp.zeros((l, pad), F32)], axis=1)
    return jnp.concatenate([ct_c, ct_l], axis=0), jnp.concatenate([jnp.zeros((l, HEAD_PAD), F32), st_l], axis=0)


def kernel(x, c, ctx, c_ctx, ada_w, ada_b, norm_g, ab_w_in, ab_w_out, ret_decay_logit, lru_conv_w, lru_conv_b, lru_gate_w, lru_gate_b, lru_lambda, mla_w_in, mla_q_norm_g, mla_kv_norm_g, mla_w_uq, mla_w_ukv, mla_w_out, moe_group_w, moe_group_b, moe_expert_w, moe_expert_b, moe_w_gate, moe_w_up, moe_w_down, final_norm_g):
    b, n, d = x.shape
    l = ctx.shape[1]
    s = l + n
    depth = ada_w.shape[0]
    rt = _Rows(b, s, l)

    nrow = (b + 1 + 7) // 8 * 8
    cvec = jnp.concatenate([c, c_ctx[None, :], jnp.zeros((nrow - b - 1, d), F32)], axis=0)
    ada = _ada_all(cvec, ada_w, ada_b)

    def layer_mods(layer):
        lat = ada[layer, :b].reshape(b, 1, 6, d)
        cx = jnp.broadcast_to(ada[layer, b].reshape(1, 1, 6, d), (b, 1, 6, d))
        m = jnp.concatenate([cx, lat], axis=1)
        m = jnp.concatenate([m, jnp.zeros((b, 2, 2, d), F32)], axis=2)
        return m.reshape(2 * b, 8, d)

    cos2, sin2 = _ret_tables(n, l)
    ct, st = _mla_tables(n, l)

    xs = jnp.concatenate([ctx, x], axis=1).reshape(b * s, d)
    out = None
    for layer in range(depth):
        mods = layer_mods(layer)
        i = layer // 2
        if layer % 2 == 0:
            p3 = _pre_ab(rt, xs, mods, norm_g[layer, 0], ab_w_in[i].astype(BF16)).reshape(b, s, -1)
            lg = jax.nn.log_sigmoid(ret_decay_logit[i].astype(F32))
            lgv = jnp.broadcast_to(lg.T[:, :, None], (RET_HEADS, 2, LANES))
            lgv = jnp.concatenate([lgv, jnp.zeros((RET_HEADS, 6, LANES), F32)], axis=1)
            ma = _retention(p3, cos2, sin2, lgv, l).reshape(b * s, -1)
            mb = _rglru(p3, *_lru_params(lru_conv_w[i], lru_conv_b[i], lru_gate_w[i], lru_gate_b[i],
                                         lru_lambda[i]), l).reshape(b * s, -1)
            cb = 0
            w_out = ab_w_out[i].astype(BF16)
        else:
            wts = _mla_params(mla_w_in[i], mla_q_norm_g[i], mla_kv_norm_g[i], mla_w_uq[i], mla_w_ukv[i])
            q, k, v = _pre_mla(rt, xs, mods, norm_g[layer, 0], wts, ct, st)
            att = _attention(q.reshape(b, s, -1), k.reshape(b, s, -1), v.reshape(b, s, -1), l)
            ma = mb = att.reshape(b * s, -1)
            cb = 1
            w_out = mla_w_out[i].astype(BF16)
        wr = jnp.concatenate([moe_group_w[layer], moe_expert_w[layer],
                              jnp.zeros((d, LANES - MOE_GROUPS - MOE_EXPERTS), F32)], axis=1)
        wr_hi = wr.astype(BF16)
        wr = jnp.concatenate([wr_hi, (wr - wr_hi.astype(F32)).astype(BF16)], axis=1)
        br = jnp.concatenate([moe_group_b[layer], moe_expert_b[layer],
                              jnp.zeros((LANES - MOE_GROUPS - MOE_EXPERTS,), F32)]).reshape(1, LANES)
        xs, h2, route, cnt = _post(rt, xs, ma, mb, cb, w_out, mods, norm_g[layer, 1], wr, br)
        dest, block_e, n_valid, fill, nb = _moe_plan(route, cnt)
        xsort = _dispatch(h2, dest, fill, nb * MOE_ROWS)
        ys = _experts(xsort, block_e, n_valid, layer, moe_w_gate, moe_w_up, moe_w_down)
        if layer + 1 < depth:
            xs = _combine(rt, xs, route, mods, ys, dest)
        else:
            out = _combine(rt, xs, route, mods, ys, dest, final_norm_g, n)
    return out.reshape(b, n, d)
```

```python
import functools

import jax
import jax.numpy as jnp
from jax import lax
from jax.experimental import pallas as pl
from jax.experimental.pallas import tpu as pltpu

F32 = jnp.float32
BF16 = jnp.bfloat16

EPS = 1e-6
ROPE_BASE = 10000.0
GRID_W = 64

RET_HEADS = 4
RET_DK = 128
RET_CHUNK = 128
RET_HPS = 2
LRU_WIDTH = 512
LRU_BLOCK = 64
LRU_C = 8.0
LRU_HALF = 256
LRU_TILE = 128
LRU_SUB = 8

MLA_HEADS = 16
MLA_NOPE = 64
MLA_ROPE = 32
MLA_V = 64
MLA_Q_RANK = 384
MLA_KV_RANK = 256
MLA_SCALE = (MLA_NOPE + MLA_ROPE) ** -0.5
LOG2_E = 1.4426950408889634
HEAD_PAD = 128

MOE_GROUPS = 4
MOE_PER_GROUP = 8
MOE_EXPERTS = 32
MOE_ROWS = 256

ROW_TILE = 256
ATT_TQ = 256
ATT_HEADS = 4
ATT_DEN_EVEN = 64
ATT_DEN_ODD = 0
LANES = 128
VMEM_LIMIT = 56 * 1024 * 1024


def _cparams(*sem):
    return pltpu.CompilerParams(dimension_semantics=sem, vmem_limit_bytes=VMEM_LIMIT)


def _rms(x, g):
    return x * lax.rsqrt(jnp.mean(x * x, axis=-1, keepdims=True) + EPS) * g


def _dot(a, b):
    return jnp.dot(a, b, preferred_element_type=F32)


def _dot_nt(a, b):
    return lax.dot_general(a, b, (((1,), (1,)), ((), ())), preferred_element_type=F32)


def _dot_tn(a, b):
    return lax.dot_general(a, b, (((0,), (0,)), ((), ())), preferred_element_type=F32)


def _ada_kernel(s_ref, w_ref, b_ref, o_ref):
    s = jax.nn.silu(s_ref[...])
    o_ref[0] = _dot(s.astype(BF16), w_ref[0].astype(BF16)) + b_ref[0]


def _ada_all(cvec, ada_w, ada_b):
    depth, d, n6 = ada_w.shape
    rows = cvec.shape[0]
    tn = n6 // 4
    return pl.pallas_call(
        _ada_kernel,
        out_shape=jax.ShapeDtypeStruct((depth, rows, n6), F32),
        grid=(depth, n6 // tn),
        in_specs=[pl.BlockSpec((rows, d), lambda l, j: (0, 0)),
                  pl.BlockSpec((1, d, tn), lambda l, j: (l, 0, j)),
                  pl.BlockSpec((1, 1, tn), lambda l, j: (l, 0, j))],
        out_specs=pl.BlockSpec((1, rows, tn), lambda l, j: (l, 0, j)),
        compiler_params=_cparams("arbitrary", "arbitrary"),
        name="adaln",
    )(cvec, ada_w, ada_b.reshape(depth, 1, n6))


class _Rows:
    def __init__(self, b, s, l):
        assert s % ROW_TILE == 0 and l % ROW_TILE == 0
        self.b, self.s, self.l = b, s, l
        self.tpb = s // ROW_TILE
        self.ctx_tiles = l // ROW_TILE
        self.n_tiles = b * self.tpb
        self.rows = b * s

    def mod_idx(self, i):
        return 2 * (i // self.tpb) + jnp.where(i % self.tpb >= self.ctx_tiles, 1, 0)

    def pos_idx(self, i):
        return i % self.tpb


def _modulated(x, g_ref, mod_ref, base):
    h = _rms(x, g_ref[...])
    return h * (1.0 + mod_ref[0, base + 1:base + 2, :]) + mod_ref[0, base:base + 1, :]


def _pre_ab_kernel(x_ref, mod_ref, g_ref, w_ref, p_ref):
    h = _modulated(x_ref[...], g_ref, mod_ref, 0)
    p_ref[...] = _dot(h.astype(BF16), w_ref[...])


def _pre_ab(rt, x, mods, g1, w_in):
    d = x.shape[1]
    n_out = w_in.shape[1]
    return pl.pallas_call(
        _pre_ab_kernel,
        out_shape=jax.ShapeDtypeStruct((rt.rows, n_out), F32),
        grid=(rt.n_tiles,),
        in_specs=[pl.BlockSpec((ROW_TILE, d), lambda i: (i, 0)),
                  pl.BlockSpec((1, 8, d), lambda i: (rt.mod_idx(i), 0, 0)),
                  pl.BlockSpec((1, d), lambda i: (0, 0)),
                  pl.BlockSpec((d, n_out), lambda i: (0, 0))],
        out_specs=pl.BlockSpec((ROW_TILE, n_out), lambda i: (i, 0)),
        compiler_params=_cparams("arbitrary"), name="pre_ab",
    )(x, mods, g1.reshape(1, d), w_in)


def _ret_kernel(q_ref, k_ref, v_ref, g_ref, cos_ref, sin_ref, lg_ref, o_ref, qs, ks, acc, *, s_len, l_len):
    c, dk = RET_CHUNK, RET_DK
    nch, cch = s_len // c, l_len // c
    ii = lax.broadcasted_iota(jnp.int32, (c, c), 0).astype(F32)
    jj = lax.broadcasted_iota(jnp.int32, (c, c), 1).astype(F32)
    diff = ii - jj
    k_scale = RET_DK ** -0.5

    def head_consts(j):
        lgf = lg_ref[j, 0:1, :]
        lgb = lg_ref[j, 1:2, :]
        dmask = (jnp.where(diff > 0, jnp.exp(lgf * jnp.maximum(diff, 0.0)), 0.0)
                 + jnp.where(diff < 0, jnp.exp(lgb * jnp.maximum(-diff, 0.0)), 0.0)
                 + jnp.where(diff == 0, 2.0, 0.0))
        return dict(dmask=dmask,
                    zeta_f=jnp.exp(lgf * (c - 1.0 - ii)), xi_f=jnp.exp(lgf * (ii + 1.0)),
                    zeta_b=jnp.exp(lgb * ii), xi_b=jnp.exp(lgb * (c - ii)),
                    cd_f=jnp.exp(lgf * c), cd_b=jnp.exp(lgb * c))

    hc = [head_consts(j) for j in range(RET_HPS)]

    def fwd(n, sts):
        rows = pl.ds(pl.multiple_of(n * c, c), c)
        cs, sn = cos_ref[rows, :], sin_ref[rows, :]
        out = []
        for j, st in enumerate(sts):
            cols = slice(j * dk, (j + 1) * dk)
            q = q_ref[0, rows, cols]
            k = k_ref[0, rows, cols]
            v = v_ref[0, rows, cols]
            qb = (q * cs + pltpu.roll(q, 64, 1) * sn).astype(BF16)
            kb = ((k * cs + pltpu.roll(k, 64, 1) * sn) * k_scale).astype(BF16)
            qs[rows, cols] = qb
            ks[rows, cols] = kb
            sc = _dot_nt(qb, kb) * hc[j]["dmask"]
            acc[rows, cols] = _dot(sc.astype(BF16), v.astype(BF16)) + _dot(qb, st.astype(BF16)) * hc[j]["xi_f"]
            out.append(hc[j]["cd_f"] * st + _dot_tn(kb, (v * hc[j]["zeta_f"]).astype(BF16)))
        return tuple(out)

    def bwd(n, sts):
        rows = pl.ds(pl.multiple_of(n * c, c), c)
        out = []
        for j, st in enumerate(sts):
            cols = slice(j * dk, (j + 1) * dk)
            qb = qs[rows, cols]
            kb = ks[rows, cols]
            v = v_ref[0, rows, cols]
            y = acc[rows, cols] + _dot(qb, st.astype(BF16)) * hc[j]["xi_b"]
            y = y * lax.rsqrt(jnp.mean(y * y, axis=-1, keepdims=True) + EPS)
            o_ref[0, rows, cols] = (y * jax.nn.silu(g_ref[0, rows, cols])).astype(o_ref.dtype)
            out.append(hc[j]["cd_b"] * st + _dot_tn(kb, (v * hc[j]["zeta_b"]).astype(BF16)))
        return tuple(out)

    zero = tuple(jnp.zeros((c, c), F32) for _ in range(RET_HPS))
    lax.fori_loop(0, nch, fwd, zero)
    sts = lax.fori_loop(0, cch, lambda t, sts: bwd(cch - 1 - t, sts), zero)
    lax.fori_loop(0, nch - cch, lambda t, sts: bwd(nch - 1 - t, sts), sts)


def _retention(p3, cos2, sin2, lgv, l_len):
    b, s, _ = p3.shape
    groups = RET_HEADS // RET_HPS
    w = RET_HPS * RET_DK

    def col(off, **kw):
        return pl.BlockSpec((1, s, w), lambda bi, hi: (bi, 0, off + hi), **kw)

    once = dict(pipeline_mode=pl.Buffered(1))
    return pl.pallas_call(
        functools.partial(_ret_kernel, s_len=s, l_len=l_len),
        out_shape=jax.ShapeDtypeStruct((b, s, RET_HEADS * RET_DK), BF16),
        grid=(b, groups),
        in_specs=[col(0), col(groups), col(2 * groups), col(3 * groups, **once),
                  pl.BlockSpec((s, RET_DK), lambda bi, hi: (0, 0), **once),
                  pl.BlockSpec((s, RET_DK), lambda bi, hi: (0, 0), **once),
                  pl.BlockSpec((RET_HPS, 8, LANES), lambda bi, hi: (hi, 0, 0))],
        out_specs=pl.BlockSpec((1, s, w), lambda bi, hi: (bi, 0, hi)),
        scratch_shapes=[pltpu.VMEM((s, w), BF16), pltpu.VMEM((s, w), BF16), pltpu.VMEM((s, w), F32)],
        compiler_params=_cparams("arbitrary", "arbitrary"), name="retention",
    )(p3, p3, p3, p3, cos2, sin2, lgv)


def _tile_scan(a, b, reverse):
    n = a.shape[0]
    rows = lax.broadcasted_iota(jnp.int32, a.shape, 0)
    step = 1
    while step < n:
        shift = n - step if reverse else step
        a_s = pltpu.roll(a, shift, 0)
        b_s = pltpu.roll(b, shift, 0)
        m = (rows < n - step) if reverse else (rows >= step)
        b = jnp.where(m, a * b_s + b, b)
        a = jnp.where(m, a * a_s, a)
        step *= 2
    return a, b


def _scan_rows(a, b, carry, reverse):
    n = a.shape[0]
    pieces = [None] * (n // LRU_SUB)
    for i in (reversed(range(len(pieces))) if reverse else range(len(pieces))):
        rows = slice(i * LRU_SUB, (i + 1) * LRU_SUB)
        a_c, h_loc = _tile_scan(a[rows], b[rows], reverse)
        pieces[i] = h_loc + a_c * carry
        carry = pieces[i][0:1] if reverse else pieces[i][LRU_SUB - 1:LRU_SUB]
    return jnp.concatenate(pieces, axis=0), carry


def _lru_kernel(x_ref, y_ref, cw_ref, wg_ref, gb_ref, sp_ref, o_ref, xpad, hf, ab, bb, *, s_len, l_len):
    tl, w = LRU_TILE, LRU_HALF
    ntl, ctl = s_len // tl, l_len // tl
    xpad[0:8, :] = jnp.zeros((8, w), F32)
    xpad[s_len + 8:s_len + 16, :] = jnp.zeros((8, w), F32)
    xpad[8:s_len + 8, :] = x_ref[0]
    w0, w1, w2, w3, cb = (cw_ref[0, t:t + 1, :] for t in range(5))
    sp_f = sp_ref[0, 0:1, :]
    sp_b = sp_ref[0, 1:2, :]
    it = lax.broadcasted_iota(jnp.int32, (tl, w), 0)

    def coeff(gr, gi, sp, xc):
        r = jax.nn.sigmoid(gr)
        i = jax.nn.sigmoid(gi)
        log_a = -LRU_C * r * sp
        th = jnp.tanh(log_a)
        return jnp.exp(log_a), jnp.sqrt(-2.0 * th / (1.0 - th)) * (i * xc)

    def fwd(n, carry):
        r0 = pl.multiple_of(n * tl, tl)
        win = xpad[pl.ds(r0, tl + 16), :]
        t = r0 + it
        seg = jnp.where(t >= l_len, 1, 0)

        def tap(d):
            v = pltpu.roll(win, (tl + 16 - d) % (tl + 16), 0)[8:8 + tl]
            return jnp.where(jnp.where(t + d >= l_len, 1, 0) == seg, v, 0.0)

        xc = tap(-2) * w0
        xc = xc + tap(-1) * w1
        xc = xc + win[8:8 + tl] * w2
        xc = xc + tap(1) * w3
        xc = xc + cb
        gts = _dot(xc.astype(BF16), wg_ref[0]) + gb_ref[0]
        a_f, b_f = coeff(gts[:, 0:w], gts[:, w:2 * w], sp_f, xc)
        a_b, b_b = coeff(gts[:, 2 * w:3 * w], gts[:, 3 * w:4 * w], sp_b, xc)
        rows = pl.ds(r0, tl)
        ab[rows, :] = a_b
        bb[rows, :] = b_b
        hh, carry = _scan_rows(a_f, b_f, carry, False)
        hf[rows, :] = hh
        return carry

    def bwd(n, carry):
        rows = pl.ds(pl.multiple_of(n * tl, tl), tl)
        hh, carry = _scan_rows(ab[rows, :], bb[rows, :], carry, True)
        o_ref[0, rows, :] = ((hf[rows, :] + hh) * jax.nn.gelu(y_ref[0, rows, :])).astype(o_ref.dtype)
        return carry

    zero = jnp.zeros((1, w), F32)
    lax.fori_loop(0, ntl, fwd, zero)
    c = lax.fori_loop(0, ctl, lambda t, c: bwd(ctl - 1 - t, c), zero)
    lax.fori_loop(0, ntl - ctl, lambda t, c: bwd(ntl - 1 - t, c), c)


def _rglru(p3, conv_wb, gate_w, gate_b, sp, l_len):
    b, s, _ = p3.shape
    nh = LRU_WIDTH // LRU_HALF
    xoff = (4 * RET_HEADS * RET_DK) // LRU_HALF
    yoff = xoff + nh
    return pl.pallas_call(
        functools.partial(_lru_kernel, s_len=s, l_len=l_len),
        out_shape=jax.ShapeDtypeStruct((b, s, LRU_WIDTH), BF16),
        grid=(b, nh),
        in_specs=[pl.BlockSpec((1, s, LRU_HALF), lambda bi, j: (bi, 0, xoff + j)),
                  pl.BlockSpec((1, s, LRU_HALF), lambda bi, j: (bi, 0, yoff + j)),
                  pl.BlockSpec((1, 8, LRU_HALF), lambda bi, j: (j, 0, 0)),
                  pl.BlockSpec((1, LRU_HALF, 4 * LRU_HALF), lambda bi, j: (j, 0, 0)),
                  pl.BlockSpec((1, 1, 4 * LRU_HALF), lambda bi, j: (j, 0, 0)),
                  pl.BlockSpec((1, 8, LRU_HALF), lambda bi, j: (j, 0, 0))],
        out_specs=pl.BlockSpec((1, s, LRU_HALF), lambda bi, j: (bi, 0, j)),
        scratch_shapes=[pltpu.VMEM((s + 16, LRU_HALF), F32), pltpu.VMEM((s, LRU_HALF), F32),
                        pltpu.VMEM((s, LRU_HALF), F32), pltpu.VMEM((s, LRU_HALF), F32)],
        compiler_params=_cparams("arbitrary", "arbitrary"), name="rglru",
    )(p3, p3, conv_wb, gate_w, gate_b, sp)


def _lru_params(conv_w, conv_b, gate_w, gate_b, lam):
    nh = LRU_WIDTH // LRU_HALF
    bph = LRU_HALF // LRU_BLOCK
    cw = jnp.concatenate([conv_w, conv_b[None, :], jnp.zeros((3, LRU_WIDTH), F32)], axis=0)
    cw = cw.reshape(8, nh, LRU_HALF).transpose(1, 0, 2)
    eye = jnp.eye(bph, dtype=F32)
    gw = gate_w.reshape(2, 2, nh, bph, LRU_BLOCK, LRU_BLOCK)
    dense = jnp.einsum('dgjkio,kl->jkidglo', gw, eye)
    dense = dense.reshape(nh, LRU_HALF, 4 * LRU_HALF).astype(BF16)
    gb = gate_b.reshape(2, 2, nh, LRU_HALF).transpose(2, 0, 1, 3).reshape(nh, 1, 4 * LRU_HALF)
    sp = jax.nn.softplus(-lam.astype(F32)).reshape(2, nh, LRU_HALF).transpose(1, 0, 2)
    sp = jnp.concatenate([sp, jnp.zeros((nh, 6, LRU_HALF), F32)], axis=1)
    return cw, dense, gb, sp


def _post_kernel(x_ref, ma_ref, mb_ref, w_ref, mod_ref, g_ref, wr_ref, br_ref, xo_ref, h_ref, r_ref, cnt_ref):
    m = jnp.concatenate([ma_ref[...], mb_ref[...]], axis=1)
    o = _dot(m, w_ref[...])
    x = x_ref[...] + mod_ref[0, 2:3, :] * o
    xo_ref[...] = x
    h = _modulated(x, g_ref, mod_ref, 3)
    h_ref[...] = h
    h_hi = h.astype(BF16)
    h_lo = (h - h_hi.astype(F32)).astype(BF16)
    part = _dot(h_hi, wr_ref[...])
    lg = part[:, 0:LANES] + part[:, LANES:2 * LANES] + _dot(h_lo, wr_ref[:, 0:LANES]) + br_ref[...]
    lane = lax.broadcasted_iota(jnp.int32, lg.shape, 1)
    lanef = lane.astype(F32)
    ninf = -jnp.inf
    big = float(LANES)
    gl = jnp.where(lane < MOE_GROUPS, lg, ninf)
    gmax = jnp.max(gl, axis=1, keepdims=True)
    g_top = 1.0 / jnp.sum(jnp.exp(gl - gmax), axis=1, keepdims=True)
    g_sel = jnp.min(jnp.where(gl == gmax, lanef, big), axis=1, keepdims=True)
    lo = MOE_GROUPS + MOE_PER_GROUP * g_sel
    el = jnp.where((lanef >= lo) & (lanef < lo + MOE_PER_GROUP), lg, ninf)
    emax = jnp.max(el, axis=1, keepdims=True)
    esum = jnp.sum(jnp.exp(el - emax), axis=1, keepdims=True)
    i1 = jnp.min(jnp.where(el == emax, lanef, big), axis=1, keepdims=True)
    el2 = jnp.where(lanef == i1, ninf, el)
    m2 = jnp.max(el2, axis=1, keepdims=True)
    i2 = jnp.min(jnp.where(el2 == m2, lanef, big), axis=1, keepdims=True)
    p1 = 1.0 / esum
    p2 = jnp.exp(m2 - emax) / esum
    tot = p1 + p2
    w1 = g_top * (p1 / tot)
    w2 = g_top * (p2 / tot)
    hit1 = lanef == i1
    hit2 = lanef == i2
    onehot = jnp.where(hit1, 1.0, 0.0) + jnp.where(hit2, 1.0, 0.0)
    ti = lax.broadcasted_iota(jnp.int32, (ROW_TILE, ROW_TILE), 0)
    tj = lax.broadcasted_iota(jnp.int32, (ROW_TILE, ROW_TILE), 1)
    earlier = jnp.where(tj < ti, 1.0, 0.0).astype(BF16)

    @pl.when(pl.program_id(0) == 0)
    def _():
        cnt_ref[...] = jnp.zeros(cnt_ref.shape, F32)

    before = _dot(earlier, onehot.astype(BF16)) + cnt_ref[0:1, :]
    k1 = jnp.sum(jnp.where(hit1, before, 0.0), axis=1, keepdims=True)
    k2 = jnp.sum(jnp.where(hit2, before, 0.0), axis=1, keepdims=True)
    cnt_ref[0:1, :] = cnt_ref[0:1, :] + jnp.sum(onehot, axis=0, keepdims=True)
    vals = (i1 - MOE_GROUPS, i2 - MOE_GROUPS, w1, w2, k1, k2)
    slab = jnp.zeros(lg.shape, F32)
    for col, v in enumerate(vals):
        slab = jnp.where(lane == col, v, slab)
    r_ref[...] = slab


def _post(rt, x, ma, mb, cb, w_out, mods, g2, wr, br):
    d = x.shape[1]
    hd = d // 2
    return pl.pallas_call(
        _post_kernel,
        out_shape=[jax.ShapeDtypeStruct((rt.rows, d), F32), jax.ShapeDtypeStruct((rt.rows, d), F32),
                   jax.ShapeDtypeStruct((rt.rows, LANES), F32), jax.ShapeDtypeStruct((8, LANES), F32)],
        grid=(rt.n_tiles,),
        in_specs=[pl.BlockSpec((ROW_TILE, d), lambda i: (i, 0)),
                  pl.BlockSpec((ROW_TILE, hd), lambda i: (i, 0)),
                  pl.BlockSpec((ROW_TILE, hd), lambda i: (i, cb)),
                  pl.BlockSpec((d, d), lambda i: (0, 0)),
                  pl.BlockSpec((1, 8, d), lambda i: (rt.mod_idx(i), 0, 0)),
                  pl.BlockSpec((1, d), lambda i: (0, 0)),
                  pl.BlockSpec((d, 2 * LANES), lambda i: (0, 0)),
                  pl.BlockSpec((1, LANES), lambda i: (0, 0))],
        out_specs=[pl.BlockSpec((ROW_TILE, d), lambda i: (i, 0)),
                   pl.BlockSpec((ROW_TILE, d), lambda i: (i, 0)),
                   pl.BlockSpec((ROW_TILE, LANES), lambda i: (i, 0)),
                   pl.BlockSpec((8, LANES), lambda i: (0, 0))],
        compiler_params=_cparams("arbitrary"), name="post",
    )(x, ma, mb, w_out, mods, g2.reshape(1, d), wr, br)


def _moe_plan(route, cnt):
    mb = MOE_ROWS
    t_count = route.shape[0]
    nb = (2 * t_count + MOE_EXPERTS * (mb - 1) + mb - 1) // mb
    counts = cnt[0, MOE_GROUPS:MOE_GROUPS + MOE_EXPERTS].astype(jnp.int32)
    padded = (counts + mb - 1) // mb * mb
    pend = jnp.cumsum(padded)
    pstart = pend - padded
    experts = jnp.arange(MOE_EXPERTS, dtype=jnp.int32)
    e = route[:, 0:2].astype(jnp.int32)
    first = jnp.sum(jnp.where(e[:, :, None] == experts[None, None, :], pstart[None, None, :], 0), axis=-1)
    dest = (first + route[:, 4:6].astype(jnp.int32)).reshape(-1)
    blk0 = jnp.arange(nb, dtype=jnp.int32) * mb
    block_e = jnp.minimum(jnp.sum((blk0[:, None] >= pend[None, :]).astype(jnp.int32), axis=1), MOE_EXPERTS - 1)
    sel = block_e[:, None] == experts[None, :]
    used = blk0 - jnp.sum(jnp.where(sel, pstart[None, :], 0), axis=1)
    n_valid = jnp.clip(jnp.sum(jnp.where(sel, counts[None, :], 0), axis=1) - used, 0, mb).astype(jnp.int32)
    fill = jnp.concatenate([pstart + counts, padded - counts, pend[-1:], nb - pend[-1:] // mb])
    return dest, block_e, n_valid, fill.astype(jnp.int32), nb


def _dispatch_kernel(dest_ref, fill_ref, h_ref, xs_hbm, stage, zbuf, sem, zsems, *, nt):
    i = pl.program_id(0)
    slot = i % 2
    zsem = zsems.at[0]

    def wait_tile(sl):
        for _ in range(2):
            pltpu.make_async_copy(stage.at[sl], xs_hbm.at[pl.ds(0, ROW_TILE)], sem.at[sl]).wait()

    def zero_padding(wait):
        def go(cp):
            cp.wait() if wait else cp.start()

        def one_row(r):
            go(pltpu.make_async_copy(zbuf.at[pl.ds(0, 1)], xs_hbm.at[pl.ds(r, 1)], zsem))

        def per_expert(e, c):
            start = fill_ref[e]
            n = fill_ref[MOE_EXPERTS + e]
            head = jnp.minimum((8 - (start & 7)) & 7, n)
            mid = pl.multiple_of(lax.shift_left(lax.shift_right_logical(n - head, 3), 3), 8)
            lax.fori_loop(0, head, lambda j, c2: (one_row(start + j), c2)[1], 0)

            @pl.when(mid > 0)
            def _():
                at = pl.multiple_of(start + head, 8)
                go(pltpu.make_async_copy(zbuf.at[pl.ds(0, mid)], xs_hbm.at[pl.ds(at, mid)], zsem))

            lax.fori_loop(0, n - head - mid, lambda j, c2: (one_row(start + head + mid + j), c2)[1], 0)
            return c
        lax.fori_loop(0, MOE_EXPERTS, per_expert, 0)

        def per_block(j, c):
            at = pl.multiple_of(fill_ref[2 * MOE_EXPERTS] + j * MOE_ROWS, MOE_ROWS)
            go(pltpu.make_async_copy(zbuf, xs_hbm.at[pl.ds(at, MOE_ROWS)], zsem))
            return c
        lax.fori_loop(0, fill_ref[2 * MOE_EXPERTS + 1], per_block, 0)

    @pl.when(i == 0)
    def _():
        zbuf[...] = jnp.zeros(zbuf.shape, zbuf.dtype)
        zero_padding(False)

    @pl.when(i >= 2)
    def _():
        wait_tile(slot)

    stage[slot] = h_ref[...]

    def body(j, c):
        tok = i * ROW_TILE + j
        src = stage.at[slot, pl.ds(j, 1)]
        pltpu.make_async_copy(src, xs_hbm.at[pl.ds(dest_ref[2 * tok], 1)], sem.at[slot]).start()
        pltpu.make_async_copy(src, xs_hbm.at[pl.ds(dest_ref[2 * tok + 1], 1)], sem.at[slot]).start()
        return c
    lax.fori_loop(0, ROW_TILE, body, 0, unroll=8)

    @pl.when(i == nt - 1)
    def _():
        wait_tile(slot)
        if nt > 1:
            wait_tile(1 - slot)
        zero_padding(True)


def _dispatch(h, dest, fill, n_rows):
    t_count, d = h.shape
    nt = t_count // ROW_TILE
    assert MOE_ROWS <= ROW_TILE
    grid_spec = pltpu.PrefetchScalarGridSpec(
        num_scalar_prefetch=2, grid=(nt,),
        in_specs=[pl.BlockSpec((ROW_TILE, d), lambda i, de, fi: (i, 0))],
        out_specs=pl.BlockSpec(memory_space=pl.ANY),
        scratch_shapes=[pltpu.VMEM((2, ROW_TILE, d), F32), pltpu.VMEM((MOE_ROWS, d), F32),
                        pltpu.SemaphoreType.DMA((2,)), pltpu.SemaphoreType.DMA((1,))])
    return pl.pallas_call(
        functools.partial(_dispatch_kernel, nt=nt),
        out_shape=jax.ShapeDtypeStruct((n_rows, d), F32),
        grid_spec=grid_spec,
        compiler_params=_cparams("arbitrary"), name="dispatch",
    )(dest, fill, h)


def _expert_kernel(be_ref, nv_ref, x_ref, wg_ref, wu_ref, wd_ref, y_ref, wgb, wub, wdb):
    i = pl.program_id(0)

    @pl.when(nv_ref[i] > 0)
    def _():
        @pl.when((i == 0) | (be_ref[i] != be_ref[jnp.maximum(i - 1, 0)]))
        def _():
            wgb[...] = wg_ref[0, 0].astype(BF16)
            wub[...] = wu_ref[0, 0].astype(BF16)
            wdb[...] = wd_ref[0, 0].astype(BF16)

        x = x_ref[...].astype(BF16)
        a = (jax.nn.silu(_dot(x, wgb[...])) * _dot(x, wub[...])).astype(BF16)
        y_ref[...] = _dot(a, wdb[...])

    @pl.when(nv_ref[i] == 0)
    def _():
        y_ref[...] = jnp.zeros(y_ref.shape, y_ref.dtype)


def _experts(xs, block_e, n_valid, layer, wg, wu, wd):
    n_rows, d = xs.shape
    mb = MOE_ROWS
    hid = wg.shape[3]

    def wspec(shape):
        return pl.BlockSpec(shape, lambda i, be, nv: (layer, be[i], 0, 0))

    grid_spec = pltpu.PrefetchScalarGridSpec(
        num_scalar_prefetch=2, grid=(n_rows // mb,),
        in_specs=[pl.BlockSpec((mb, d), lambda i, be, nv: (i, 0)),
                  wspec((1, 1, d, hid)), wspec((1, 1, d, hid)), wspec((1, 1, hid, d))],
        out_specs=pl.BlockSpec((mb, d), lambda i, be, nv: (i, 0)),
        scratch_shapes=[pltpu.VMEM((d, hid), BF16), pltpu.VMEM((d, hid), BF16), pltpu.VMEM((hid, d), BF16)])
    return pl.pallas_call(
        _expert_kernel,
        out_shape=jax.ShapeDtypeStruct((n_rows, d), F32),
        grid_spec=grid_spec,
        compiler_params=_cparams("arbitrary"), name="experts",
    )(block_e, n_valid, xs, wg, wu, wd)


def _combine_kernel(dest_ref, x_ref, r_ref, modp_ref, *rest, final, tile_of):
    if final:
        g_ref, ys_hbm, o_ref, ybuf, sem = rest
    else:
        ys_hbm, o_ref, ybuf, sem = rest
    i = pl.program_id(0)
    n = pl.num_programs(0)
    slot = i % 2

    def issue(step, sl):
        base = tile_of(step) * ROW_TILE

        def body(j, c):
            tok = base + j
            for k in range(2):
                pltpu.make_async_copy(ys_hbm.at[pl.ds(dest_ref[2 * tok + k], 1)], ybuf.at[sl, k, pl.ds(j, 1)],
                                      sem.at[sl]).start()
            return c
        lax.fori_loop(0, ROW_TILE, body, 0, unroll=8)

    @pl.when(i == 0)
    def _():
        issue(0, 0)

    @pl.when(i + 1 < n)
    def _():
        issue(i + 1, 1 - slot)

    for k in range(2):
        pltpu.make_async_copy(ys_hbm.at[pl.ds(0, ROW_TILE)], ybuf.at[slot, k], sem.at[slot]).wait()
    r = r_ref[...]
    y = ybuf[slot, 0] * r[:, 2:3] + ybuf[slot, 1] * r[:, 3:4]
    x = x_ref[...] + modp_ref[0, 5:6, :] * y
    o_ref[...] = _rms(x, g_ref[...]) if final else x


def _combine(rt, x, route, mods, ys, dest, final_g=None, n_len=None):
    d = x.shape[1]
    final = final_g is not None
    if final:
        lt = n_len // ROW_TILE
        steps, out_rows = rt.b * lt, rt.b * n_len

        def tile_of(i):
            return (i // lt) * rt.tpb + rt.ctx_tiles + i % lt
    else:
        steps, out_rows = rt.n_tiles, rt.rows

        def tile_of(i):
            return i

    in_specs = [pl.BlockSpec((ROW_TILE, d), lambda i, de: (tile_of(i), 0)),
                pl.BlockSpec((ROW_TILE, LANES), lambda i, de: (tile_of(i), 0)),
                pl.BlockSpec((1, 8, d), lambda i, de: (rt.mod_idx(tile_of(i)), 0, 0))]
    args = [x, route, mods]
    if final:
        in_specs.append(pl.BlockSpec((1, d), lambda i, de: (0, 0)))
        args.append(final_g.reshape(1, d))
    in_specs.append(pl.BlockSpec(memory_space=pl.ANY))
    args.append(ys)
    grid_spec = pltpu.PrefetchScalarGridSpec(
        num_scalar_prefetch=1, grid=(steps,), in_specs=in_specs,
        out_specs=pl.BlockSpec((ROW_TILE, d), lambda i, de: (i, 0)),
        scratch_shapes=[pltpu.VMEM((2, 2, ROW_TILE, d), F32), pltpu.SemaphoreType.DMA((2,))])
    return pl.pallas_call(
        functools.partial(_combine_kernel, final=final, tile_of=tile_of),
        out_shape=jax.ShapeDtypeStruct((out_rows, d), F32),
        grid_spec=grid_spec,
        compiler_params=_cparams("arbitrary"), name="combine_final" if final else "combine",
    )(dest, *args)


def _pre_mla_kernel(x_ref, mod_ref, g_ref, win_ref, qg_ref, kvg_ref,
                    wq_ref, wqs_ref, wk_ref, wv_ref, vone_ref, ct_ref, st_ref, q_ref, k_ref, v_ref):
    h = _modulated(x_ref[...], g_ref, mod_ref, 0)
    p = _dot(h.astype(BF16), win_ref[...])
    cq = _rms(p[:, 0:MLA_Q_RANK], qg_ref[...]).astype(BF16)
    ckv = _rms(p[:, MLA_Q_RANK:MLA_Q_RANK + MLA_KV_RANK], kvg_ref[...]).astype(BF16)
    off = MLA_Q_RANK + MLA_KV_RANK
    ct, st = ct_ref[...], st_ref[...]
    k_rope = p[:, off:off + HEAD_PAD] * ct + p[:, off + HEAD_PAD:off + 2 * HEAD_PAD] * st
    qa = _dot(cq, wq_ref[...])
    qb = _dot(cq, wqs_ref[...])
    kn = _dot(ckv, wk_ref[...])
    v_ref[...] = (_dot(ckv, wv_ref[...]) + vone_ref[...]).astype(v_ref.dtype)
    for hh in range(MLA_HEADS):
        sl = slice(hh * HEAD_PAD, (hh + 1) * HEAD_PAD)
        q_ref[:, sl] = ((qa[:, sl] * ct + qb[:, sl] * st) * (MLA_SCALE * LOG2_E)).astype(q_ref.dtype)
        k_ref[:, sl] = (kn[:, sl] + k_rope).astype(k_ref.dtype)


def _pre_mla(rt, x, mods, g1, wts, ct, st):
    d = x.shape[1]
    w_in, qg, kvg, wq, wqs, wk, wv, vone = wts

    def full(a):
        return pl.BlockSpec(a.shape, lambda i: (0,) * a.ndim)

    hq = MLA_HEADS * HEAD_PAD
    hv = MLA_HEADS * HEAD_PAD
    return pl.pallas_call(
        _pre_mla_kernel,
        out_shape=[jax.ShapeDtypeStruct((rt.rows, hq), BF16),
                   jax.ShapeDtypeStruct((rt.rows, hq), BF16), jax.ShapeDtypeStruct((rt.rows, hv), BF16)],
        grid=(rt.n_tiles,),
        in_specs=[
            pl.BlockSpec((ROW_TILE, d), lambda i: (i, 0)),
            pl.BlockSpec((1, 8, d), lambda i: (rt.mod_idx(i), 0, 0)),
            pl.BlockSpec((1, d), lambda i: (0, 0)),
            full(w_in), full(qg), full(kvg), full(wq), full(wqs), full(wk), full(wv), full(vone),
            pl.BlockSpec((ROW_TILE, HEAD_PAD), lambda i: (rt.pos_idx(i), 0)),
            pl.BlockSpec((ROW_TILE, HEAD_PAD), lambda i: (rt.pos_idx(i), 0))],
        out_specs=[pl.BlockSpec((ROW_TILE, hq), lambda i: (i, 0)),
                   pl.BlockSpec((ROW_TILE, hq), lambda i: (i, 0)),
                   pl.BlockSpec((ROW_TILE, hv), lambda i: (i, 0))],
        compiler_params=_cparams("arbitrary"), name="pre_mla",
    )(x, mods, g1.reshape(1, d), w_in, qg, kvg, wq, wqs, wk, wv, vone, ct, st)


def _mla_params(w_in, q_g, kv_g, w_uq, w_ukv):
    d = w_in.shape[0]
    hp, hr = HEAD_PAD, MLA_ROPE // 2
    nq = MLA_NOPE + MLA_ROPE
    kr = w_in[:, MLA_Q_RANK + MLA_KV_RANK:]
    z = jnp.zeros((d, MLA_NOPE), F32)
    zt = jnp.zeros((d, hp - nq), F32)
    kr_a = jnp.concatenate([z, kr, zt], axis=1)
    kr_b = jnp.concatenate([z, -kr[:, hr:], kr[:, :hr], zt], axis=1)
    w_in_p = jnp.concatenate([w_in[:, :MLA_Q_RANK + MLA_KV_RANK], kr_a, kr_b], axis=1).astype(BF16)
    wq = w_uq.reshape(MLA_Q_RANK, MLA_HEADS, nq)
    zq = jnp.zeros((MLA_Q_RANK, MLA_HEADS, hp - nq), F32)
    wq_a = jnp.concatenate([wq, zq], axis=2).reshape(MLA_Q_RANK, MLA_HEADS * hp).astype(BF16)
    wq_b = jnp.concatenate([jnp.zeros_like(wq[:, :, :MLA_NOPE]), -wq[:, :, MLA_NOPE + hr:],
                            wq[:, :, MLA_NOPE:MLA_NOPE + hr], zq], axis=2)
    wq_b = wq_b.reshape(MLA_Q_RANK, MLA_HEADS * hp).astype(BF16)
    wkv = w_ukv.reshape(MLA_KV_RANK, MLA_HEADS, MLA_NOPE + MLA_V)
    wk = jnp.concatenate([wkv[:, :, :MLA_NOPE], jnp.zeros((MLA_KV_RANK, MLA_HEADS, hp - MLA_NOPE), F32)], axis=2)
    wk = wk.reshape(MLA_KV_RANK, MLA_HEADS * hp).astype(BF16)
    wv = wkv[:, :, MLA_NOPE:].reshape(MLA_KV_RANK, MLA_HEADS // 2, 2, MLA_V)
    zv = jnp.zeros((MLA_KV_RANK, MLA_HEADS // 2, hp - MLA_V), F32)
    wv = jnp.concatenate([wv[:, :, 0], zv, zv, wv[:, :, 1]], axis=2).reshape(MLA_KV_RANK, MLA_HEADS * hp)
    lane = jnp.arange(2 * hp) % (2 * hp)
    vone = jnp.tile(jnp.where((lane == ATT_DEN_EVEN) | (lane == hp + ATT_DEN_ODD), 1.0, 0.0), MLA_HEADS // 2)
    return (w_in_p, q_g.reshape(1, -1), kv_g.reshape(1, -1), wq_a, wq_b, wk, wv.astype(BF16),
            vone.reshape(1, -1).astype(F32))


def _attn_kernel(q_ref, k_ref, v_ref, o_ref, *, s_len, l_len):
    t = pl.program_id(2)
    lane = lax.broadcasted_iota(jnp.int32, (ATT_TQ, 2 * MLA_V), 1)

    def attend(nk):
        for pair in range(ATT_HEADS // 2):
            outs = []
            for j, den_lane in ((2 * pair, ATT_DEN_EVEN), (2 * pair + 1, ATT_DEN_ODD)):
                blk = slice(j * HEAD_PAD, (j + 1) * HEAD_PAD)
                s = _dot_nt(q_ref[0, :, blk], k_ref[0, 0:nk, blk])
                p = jnp.exp2(s - jnp.max(s, axis=1, keepdims=True))
                o = _dot(p.astype(BF16), v_ref[0, 0:nk, blk])
                outs.append(o / o[:, den_lane:den_lane + 1])
            o_ref[0, :, pair * 2 * MLA_V:(pair + 1) * 2 * MLA_V] = (
                jnp.where(lane < MLA_V, outs[0], outs[1]).astype(o_ref.dtype))

    ctx_tiles = l_len // ATT_TQ

    @pl.when(t < ctx_tiles)
    def _():
        attend(l_len)

    @pl.when(t >= ctx_tiles)
    def _():
        attend(s_len)


def _attention(q3, k3, v3, l_len):
    b, s, _ = q3.shape
    hq = ATT_HEADS * HEAD_PAD
    hv = ATT_HEADS * MLA_V
    return pl.pallas_call(
        functools.partial(_attn_kernel, s_len=s, l_len=l_len),
        out_shape=jax.ShapeDtypeStruct((b, s, MLA_HEADS * MLA_V), BF16),
        grid=(b, MLA_HEADS // ATT_HEADS, s // ATT_TQ),
        in_specs=[pl.BlockSpec((1, ATT_TQ, hq), lambda bi, hi, ti: (bi, ti, hi)),
                  pl.BlockSpec((1, s, hq), lambda bi, hi, ti: (bi, 0, hi)),
                  pl.BlockSpec((1, s, hq), lambda bi, hi, ti: (bi, 0, hi))],
        out_specs=pl.BlockSpec((1, ATT_TQ, hv), lambda bi, hi, ti: (bi, ti, hi)),
        compiler_params=_cparams("arbitrary", "arbitrary", "arbitrary"), name="attention",
    )(q3, k3, v3)


def _ret_tables(n, l):
    t = jnp.arange(n, dtype=F32)
    inv = ROPE_BASE ** (-jnp.arange(0, RET_DK, 2, dtype=F32) / RET_DK)
    ang = t[:, None] * inv[None, :]
    cos, sin = jnp.cos(ang), jnp.sin(ang)
    cos2 = jnp.concatenate([jnp.ones((l, RET_DK), F32), jnp.concatenate([cos, cos], axis=1)], axis=0)
    sin2 = jnp.concatenate([jnp.zeros((l, RET_DK), F32), jnp.concatenate([-sin, sin], axis=1)], axis=0)
    return cos2, sin2


def _mla_tables(n, l):
    rows = n // GRID_W
    r_pos = jnp.repeat(jnp.arange(rows, dtype=F32), GRID_W)
    c_pos = jnp.tile(jnp.arange(GRID_W, dtype=F32), rows)
    ax = MLA_ROPE // 2
    inv = ROPE_BASE ** (-jnp.arange(0, ax, 2, dtype=F32) / ax)
    ang = jnp.concatenate([r_pos[:, None] * inv[None, :], c_pos[:, None] * inv[None, :]], axis=-1)
    cos, sin = jnp.cos(ang), jnp.sin(ang)
    pad = HEAD_PAD - MLA_NOPE - MLA_ROPE
    ct_l = jnp.concatenate([jnp.ones((n, MLA_NOPE), F32), cos, cos, jnp.zeros((n, pad), F32)], axis=1)
    st_l = jnp.concatenate([jnp.zeros((n, MLA_NOPE), F32), sin, sin, jnp.zeros((n, pad), F32)], axis=1)
    ct_c = jnp.concatenate([jnp.ones((l, MLA_NOPE + MLA_ROPE), F32), jnp.zeros((l, pad), F32)], axis=1)
    return jnp.concatenate([ct_c, ct_l], axis=0), jnp.concatenate([jnp.zeros((l, HEAD_PAD), F32), st_l], axis=0)


def kernel(x, c, ctx, c_ctx, ada_w, ada_b, norm_g, ab_w_in, ab_w_out, ret_decay_logit, lru_conv_w, lru_conv_b, lru_gate_w, lru_gate_b, lru_lambda, mla_w_in, mla_q_norm_g, mla_kv_norm_g, mla_w_uq, mla_w_ukv, mla_w_out, moe_group_w, moe_group_b, moe_expert_w, moe_expert_b, moe_w_gate, moe_w_up, moe_w_down, final_norm_g):
    b, n, d = x.shape
    l = ctx.shape[1]
    s = l + n
    depth = ada_w.shape[0]
    rt = _Rows(b, s, l)

    nrow = (b + 1 + 7) // 8 * 8
    cvec = jnp.concatenate([c, c_ctx[None, :], jnp.zeros((nrow - b - 1, d), F32)], axis=0)
    ada = _ada_all(cvec, ada_w, ada_b)

    def layer_mods(layer):
        lat = ada[layer, :b].reshape(b, 1, 6, d)
        cx = jnp.broadcast_to(ada[layer, b].reshape(1, 1, 6, d), (b, 1, 6, d))
        m = jnp.concatenate([cx, lat], axis=1)
        m = jnp.concatenate([m, jnp.zeros((b, 2, 2, d), F32)], axis=2)
        return m.reshape(2 * b, 8, d)

    cos2, sin2 = _ret_tables(n, l)
    ct, st = _mla_tables(n, l)

    xs = jnp.concatenate([ctx, x], axis=1).reshape(b * s, d)
    out = None
    for layer in range(depth):
        mods = layer_mods(layer)
        i = layer // 2
        if layer % 2 == 0:
            p3 = _pre_ab(rt, xs, mods, norm_g[layer, 0], ab_w_in[i].astype(BF16)).reshape(b, s, -1)
            lg = jax.nn.log_sigmoid(ret_decay_logit[i].astype(F32))
            lgv = jnp.broadcast_to(lg.T[:, :, None], (RET_HEADS, 2, LANES))
            lgv = jnp.concatenate([lgv, jnp.zeros((RET_HEADS, 6, LANES), F32)], axis=1)
            ma = _retention(p3, cos2, sin2, lgv, l).reshape(b * s, -1)
            mb = _rglru(p3, *_lru_params(lru_conv_w[i], lru_conv_b[i], lru_gate_w[i], lru_gate_b[i],
                                         lru_lambda[i]), l).reshape(b * s, -1)
            cb = 0
            w_out = ab_w_out[i].astype(BF16)
        else:
            wts = _mla_params(mla_w_in[i], mla_q_norm_g[i], mla_kv_norm_g[i], mla_w_uq[i], mla_w_ukv[i])
            q, k, v = _pre_mla(rt, xs, mods, norm_g[layer, 0], wts, ct, st)
            att = _attention(q.reshape(b, s, -1), k.reshape(b, s, -1), v.reshape(b, s, -1), l)
            ma = mb = att.reshape(b * s, -1)
            cb = 1
            w_out = mla_w_out[i].astype(BF16)
        wr = jnp.concatenate([moe_group_w[layer], moe_expert_w[layer],
                              jnp.zeros((d, LANES - MOE_GROUPS - MOE_EXPERTS), F32)], axis=1)
        wr_hi = wr.astype(BF16)
        wr = jnp.concatenate([wr_hi, (wr - wr_hi.astype(F32)).astype(BF16)], axis=1)
        br = jnp.concatenate([moe_group_b[layer], moe_expert_b[layer],
                              jnp.zeros((LANES - MOE_GROUPS - MOE_EXPERTS,), F32)]).reshape(1, LANES)
        xs, h2, route, cnt = _post(rt, xs, ma, mb, cb, w_out, mods, norm_g[layer, 1], wr, br)
        dest, block_e, n_valid, fill, nb = _moe_plan(route, cnt)
        xsort = _dispatch(h2, dest, fill, nb * MOE_ROWS)
        ys = _experts(xsort, block_e, n_valid, layer, moe_w_gate, moe_w_up, moe_w_down)
        if layer + 1 < depth:
            xs = _combine(rt, xs, route, mods, ys, dest)
        else:
            out = _combine(rt, xs, route, mods, ys, dest, final_norm_g, n)
    return out.reshape(b, n, d)
```

```python
import functools

import jax
import jax.numpy as jnp
from jax import lax
from jax.experimental import pallas as pl
from jax.experimental.pallas import tpu as pltpu

F32 = jnp.float32
BF16 = jnp.bfloat16

EPS = 1e-6
ROPE_BASE = 10000.0
GRID_W = 64

RET_HEADS = 4
RET_DK = 128
RET_CHUNK = 128
RET_HPS = 2
LRU_WIDTH = 512
LRU_BLOCK = 64
LRU_C = 8.0
LRU_HALF = 256
LRU_TILE = 128
LRU_SUB = 8

MLA_HEADS = 16
MLA_NOPE = 64
MLA_ROPE = 32
MLA_V = 64
MLA_Q_RANK = 384
MLA_KV_RANK = 256
MLA_SCALE = (MLA_NOPE + MLA_ROPE) ** -0.5
LOG2_E = 1.4426950408889634
HEAD_PAD = 128

MOE_GROUPS = 4
MOE_PER_GROUP = 8
MOE_EXPERTS = 32
MOE_ROWS = 256

ROW_TILE = 256
ATT_TQ = 256
ATT_HEADS = 4
ATT_DEN_EVEN = 64
ATT_DEN_ODD = 0
LANES = 128
VMEM_LIMIT = 56 * 1024 * 1024


def _cparams(*sem):
    return pltpu.CompilerParams(dimension_semantics=sem, vmem_limit_bytes=VMEM_LIMIT)


def _rms(x, g):
    return x * lax.rsqrt(jnp.mean(x * x, axis=-1, keepdims=True) + EPS) * g


def _dot(a, b):
    return jnp.dot(a, b, preferred_element_type=F32)


def _dot_nt(a, b):
    return lax.dot_general(a, b, (((1,), (1,)), ((), ())), preferred_element_type=F32)


def _dot_tn(a, b):
    return lax.dot_general(a, b, (((0,), (0,)), ((), ())), preferred_element_type=F32)


def _ada_kernel(s_ref, w_ref, b_ref, o_ref):
    s = jax.nn.silu(s_ref[...])
    o_ref[0] = _dot(s.astype(BF16), w_ref[0].astype(BF16)) + b_ref[0]


def _ada_all(cvec, ada_w, ada_b):
    depth, d, n6 = ada_w.shape
    rows = cvec.shape[0]
    tn = n6 // 4
    return pl.pallas_call(
        _ada_kernel,
        out_shape=jax.ShapeDtypeStruct((depth, rows, n6), F32),
        grid=(depth, n6 // tn),
        in_specs=[pl.BlockSpec((rows, d), lambda l, j: (0, 0)),
                  pl.BlockSpec((1, d, tn), lambda l, j: (l, 0, j)),
                  pl.BlockSpec((1, 1, tn), lambda l, j: (l, 0, j))],
        out_specs=pl.BlockSpec((1, rows, tn), lambda l, j: (l, 0, j)),
        compiler_params=_cparams("arbitrary", "arbitrary"),
        name="adaln",
    )(cvec, ada_w, ada_b.reshape(depth, 1, n6))


class _Rows:
    def __init__(self, b, s, l):
        assert s % ROW_TILE == 0 and l % ROW_TILE == 0
        self.b, self.s, self.l = b, s, l
        self.tpb = s // ROW_TILE
        self.ctx_tiles = l // ROW_TILE
        self.n_tiles = b * self.tpb
        self.rows = b * s

    def mod_idx(self, i):
        return 2 * (i // self.tpb) + jnp.where(i % self.tpb >= self.ctx_tiles, 1, 0)

    def pos_idx(self, i):
        return i % self.tpb


def _modulated(x, g_ref, mod_ref, base):
    h = _rms(x, g_ref[...])
    return h * (1.0 + mod_ref[0, base + 1:base + 2, :]) + mod_ref[0, base:base + 1, :]


def _moe_update(dest_ref, x_ref, r_ref, modp_ref, ys_hbm, ybuf, sem, tile_of, inline):
    i = pl.program_id(0)
    n = pl.num_programs(0)
    slot = i % 2

    def start(tok, j, sl):
        for k in range(2):
            pltpu.make_async_copy(ys_hbm.at[pl.ds(dest_ref[2 * tok + k], 1)], ybuf.at[sl, k, pl.ds(j, 1)],
                                  sem.at[sl]).start()

    def issue_loop(step, sl):
        base = tile_of(step) * ROW_TILE
        lax.fori_loop(0, ROW_TILE, lambda j, c: (start(base + j, j, sl), c)[1], 0, unroll=8)

    def wait(sl):
        for k in range(2):
            pltpu.make_async_copy(ys_hbm.at[pl.ds(0, ROW_TILE)], ybuf.at[sl, k], sem.at[sl]).wait()

    @pl.when(i == 0)
    def _():
        issue_loop(0, 0)

    if not inline:
        @pl.when(i + 1 < n)
        def _():
            issue_loop(i + 1, 1 - slot)

    wait(slot)
    r = r_ref[...]
    y = ybuf[slot, 0] * r[:, 2:3] + ybuf[slot, 1] * r[:, 3:4]
    x = x_ref[...] + modp_ref[0, 5:6, :] * y
    if not inline:
        return x, None
    base = tile_of(jnp.minimum(i + 1, n - 1)) * ROW_TILE
    for j in range(ROW_TILE):
        start(base + j, j, 1 - slot)

    def finish():
        @pl.when(i == n - 1)
        def _():
            wait(1 - slot)
    return x, finish


def _moe_operands(rt, d, moe, tile_of):
    route, modp, ys, _ = moe
    specs = [pl.BlockSpec((ROW_TILE, LANES), lambda i, de: (tile_of(i), 0)),
             pl.BlockSpec((1, 8, d), lambda i, de: (rt.mod_idx(tile_of(i)), 0, 0)),
             pl.BlockSpec(memory_space=pl.ANY)]
    scratch = [pltpu.VMEM((2, 2, ROW_TILE, d), F32), pltpu.SemaphoreType.DMA((2,))]
    return specs, [route, modp, ys], scratch


def _pre_ab_kernel(*refs, has_moe):
    if has_moe:
        dest_ref, x_ref, r_ref, modp_ref, ys_hbm, mod_ref, g_ref, w_ref, xo_ref, p_ref, ybuf, sem = refs
        x, finish = _moe_update(dest_ref, x_ref, r_ref, modp_ref, ys_hbm, ybuf, sem, lambda t: t, True)
        xo_ref[...] = x
    else:
        x_ref, mod_ref, g_ref, w_ref, p_ref = refs
        x = x_ref[...]
    h = _modulated(x, g_ref, mod_ref, 0)
    p_ref[...] = _dot(h.astype(BF16), w_ref[...])
    if has_moe:
        finish()


def _pre_ab(rt, x, moe, mods, g1, w_in):
    d = x.shape[1]
    n_out = w_in.shape[1]
    if moe is None:
        p = pl.pallas_call(
            functools.partial(_pre_ab_kernel, has_moe=False),
            out_shape=jax.ShapeDtypeStruct((rt.rows, n_out), F32),
            grid=(rt.n_tiles,),
            in_specs=[pl.BlockSpec((ROW_TILE, d), lambda i: (i, 0)),
                      pl.BlockSpec((1, 8, d), lambda i: (rt.mod_idx(i), 0, 0)),
                      pl.BlockSpec((1, d), lambda i: (0, 0)),
                      pl.BlockSpec((d, n_out), lambda i: (0, 0))],
            out_specs=pl.BlockSpec((ROW_TILE, n_out), lambda i: (i, 0)),
            compiler_params=_cparams("arbitrary"), name="pre_ab",
        )(x, mods, g1.reshape(1, d), w_in)
        return x, p
    mspecs, margs, scratch = _moe_operands(rt, d, moe, lambda t: t)
    grid_spec = pltpu.PrefetchScalarGridSpec(
        num_scalar_prefetch=1, grid=(rt.n_tiles,),
        in_specs=[pl.BlockSpec((ROW_TILE, d), lambda i, de: (i, 0))] + mspecs + [
            pl.BlockSpec((1, 8, d), lambda i, de: (rt.mod_idx(i), 0, 0)),
            pl.BlockSpec((1, d), lambda i, de: (0, 0)),
            pl.BlockSpec((d, n_out), lambda i, de: (0, 0))],
        out_specs=[pl.BlockSpec((ROW_TILE, d), lambda i, de: (i, 0)),
                   pl.BlockSpec((ROW_TILE, n_out), lambda i, de: (i, 0))],
        scratch_shapes=scratch)
    return pl.pallas_call(
        functools.partial(_pre_ab_kernel, has_moe=True),
        out_shape=[jax.ShapeDtypeStruct((rt.rows, d), F32), jax.ShapeDtypeStruct((rt.rows, n_out), F32)],
        grid_spec=grid_spec,
        compiler_params=_cparams("arbitrary"), name="pre_ab_moe",
    )(moe[3], x, *margs, mods, g1.reshape(1, d), w_in)


def _ret_kernel(q_ref, k_ref, v_ref, g_ref, cos_ref, sin_ref, lg_ref, o_ref, qs, ks, acc, *, s_len, l_len):
    c, dk = RET_CHUNK, RET_DK
    nch, cch = s_len // c, l_len // c
    ii = lax.broadcasted_iota(jnp.int32, (c, c), 0).astype(F32)
    jj = lax.broadcasted_iota(jnp.int32, (c, c), 1).astype(F32)
    diff = ii - jj
    k_scale = RET_DK ** -0.5

    def head_consts(j):
        lgf = lg_ref[j, 0:1, :]
        lgb = lg_ref[j, 1:2, :]
        dmask = (jnp.where(diff > 0, jnp.exp(lgf * jnp.maximum(diff, 0.0)), 0.0)
                 + jnp.where(diff < 0, jnp.exp(lgb * jnp.maximum(-diff, 0.0)), 0.0)
                 + jnp.where(diff == 0, 2.0, 0.0))
        return dict(dmask=dmask,
                    zeta_f=jnp.exp(lgf * (c - 1.0 - ii)), xi_f=jnp.exp(lgf * (ii + 1.0)),
                    zeta_b=jnp.exp(lgb * ii), xi_b=jnp.exp(lgb * (c - ii)),
                    cd_f=jnp.exp(lgf * c), cd_b=jnp.exp(lgb * c))

    hc = [head_consts(j) for j in range(RET_HPS)]

    def fwd(n, sts):
        rows = pl.ds(pl.multiple_of(n * c, c), c)
        cs, sn = cos_ref[rows, :], sin_ref[rows, :]
        out = []
        for j, st in enumerate(sts):
            cols = slice(j * dk, (j + 1) * dk)
            q = q_ref[0, rows, cols]
            k = k_ref[0, rows, cols]
            v = v_ref[0, rows, cols]
            qb = (q * cs + pltpu.roll(q, 64, 1) * sn).astype(BF16)
            kb = ((k * cs + pltpu.roll(k, 64, 1) * sn) * k_scale).astype(BF16)
            qs[rows, cols] = qb
            ks[rows, cols] = kb
            sc = _dot_nt(qb, kb) * hc[j]["dmask"]
            acc[rows, cols] = _dot(sc.astype(BF16), v.astype(BF16)) + _dot(qb, st.astype(BF16)) * hc[j]["xi_f"]
            out.append(hc[j]["cd_f"] * st + _dot_tn(kb, (v * hc[j]["zeta_f"]).astype(BF16)))
        return tuple(out)

    def bwd(n, sts):
        rows = pl.ds(pl.multiple_of(n * c, c), c)
        out = []
        for j, st in enumerate(sts):
            cols = slice(j * dk, (j + 1) * dk)
            qb = qs[rows, cols]
            kb = ks[rows, cols]
            v = v_ref[0, rows, cols]
            y = acc[rows, cols] + _dot(qb, st.astype(BF16)) * hc[j]["xi_b"]
            y = y * lax.rsqrt(jnp.mean(y * y, axis=-1, keepdims=True) + EPS)
            o_ref[0, rows, cols] = (y * jax.nn.silu(g_ref[0, rows, cols])).astype(o_ref.dtype)
            out.append(hc[j]["cd_b"] * st + _dot_tn(kb, (v * hc[j]["zeta_b"]).astype(BF16)))
        return tuple(out)

    zero = tuple(jnp.zeros((c, c), F32) for _ in range(RET_HPS))
    lax.fori_loop(0, nch, fwd, zero)
    sts = lax.fori_loop(0, cch, lambda t, sts: bwd(cch - 1 - t, sts), zero)
    lax.fori_loop(0, nch - cch, lambda t, sts: bwd(nch - 1 - t, sts), sts)


def _retention(p3, cos2, sin2, lgv, l_len):
    b, s, _ = p3.shape
    groups = RET_HEADS // RET_HPS
    w = RET_HPS * RET_DK

    def col(off, **kw):
        return pl.BlockSpec((1, s, w), lambda bi, hi: (bi, 0, off + hi), **kw)

    once = dict(pipeline_mode=pl.Buffered(1))
    return pl.pallas_call(
        functools.partial(_ret_kernel, s_len=s, l_len=l_len),
        out_shape=jax.ShapeDtypeStruct((b, s, RET_HEADS * RET_DK), BF16),
        grid=(b, groups),
        in_specs=[col(0), col(groups), col(2 * groups), col(3 * groups, **once),
                  pl.BlockSpec((s, RET_DK), lambda bi, hi: (0, 0), **once),
                  pl.BlockSpec((s, RET_DK), lambda bi, hi: (0, 0), **once),
                  pl.BlockSpec((RET_HPS, 8, LANES), lambda bi, hi: (hi, 0, 0))],
        out_specs=pl.BlockSpec((1, s, w), lambda bi, hi: (bi, 0, hi)),
        scratch_shapes=[pltpu.VMEM((s, w), BF16), pltpu.VMEM((s, w), BF16), pltpu.VMEM((s, w), F32)],
        compiler_params=_cparams("arbitrary", "arbitrary"), name="retention",
    )(p3, p3, p3, p3, cos2, sin2, lgv)


def _tile_scan(a, b, reverse):
    n = a.shape[0]
    rows = lax.broadcasted_iota(jnp.int32, a.shape, 0)
    step = 1
    while step < n:
        shift = n - step if reverse else step
        a_s = pltpu.roll(a, shift, 0)
        b_s = pltpu.roll(b, shift, 0)
        m = (rows < n - step) if reverse else (rows >= step)
        b = jnp.where(m, a * b_s + b, b)
        a = jnp.where(m, a * a_s, a)
        step *= 2
    return a, b


def _scan_rows(a, b, carry, reverse):
    n = a.shape[0]
    pieces = [None] * (n // LRU_SUB)
    for i in (reversed(range(len(pieces))) if reverse else range(len(pieces))):
        rows = slice(i * LRU_SUB, (i + 1) * LRU_SUB)
        a_c, h_loc = _tile_scan(a[rows], b[rows], reverse)
        pieces[i] = h_loc + a_c * carry
        carry = pieces[i][0:1] if reverse else pieces[i][LRU_SUB - 1:LRU_SUB]
    return jnp.concatenate(pieces, axis=0), carry


def _lru_kernel(x_ref, y_ref, cw_ref, wg_ref, gb_ref, sp_ref, o_ref, xpad, hf, ab, bb, *, s_len, l_len):
    tl, w = LRU_TILE, LRU_HALF
    ntl, ctl = s_len // tl, l_len // tl
    xpad[0:8, :] = jnp.zeros((8, w), F32)
    xpad[s_len + 8:s_len + 16, :] = jnp.zeros((8, w), F32)
    xpad[8:s_len + 8, :] = x_ref[0]
    w0, w1, w2, w3, cb = (cw_ref[0, t:t + 1, :] for t in range(5))
    sp_f = sp_ref[0, 0:1, :]
    sp_b = sp_ref[0, 1:2, :]
    it = lax.broadcasted_iota(jnp.int32, (tl, w), 0)

    def coeff(gr, gi, sp, xc):
        r = jax.nn.sigmoid(gr)
        i = jax.nn.sigmoid(gi)
        log_a = -LRU_C * r * sp
        th = jnp.tanh(log_a)
        return jnp.exp(log_a), jnp.sqrt(-2.0 * th / (1.0 - th)) * (i * xc)

    def fwd(n, carry):
        r0 = pl.multiple_of(n * tl, tl)
        win = xpad[pl.ds(r0, tl + 16), :]
        t = r0 + it
        seg = jnp.where(t >= l_len, 1, 0)

        def tap(d):
            v = pltpu.roll(win, (tl + 16 - d) % (tl + 16), 0)[8:8 + tl]
            return jnp.where(jnp.where(t + d >= l_len, 1, 0) == seg, v, 0.0)

        xc = tap(-2) * w0
        xc = xc + tap(-1) * w1
        xc = xc + win[8:8 + tl] * w2
        xc = xc + tap(1) * w3
        xc = xc + cb
        gts = _dot(xc.astype(BF16), wg_ref[0]) + gb_ref[0]
        a_f, b_f = coeff(gts[:, 0:w], gts[:, w:2 * w], sp_f, xc)
        a_b, b_b = coeff(gts[:, 2 * w:3 * w], gts[:, 3 * w:4 * w], sp_b, xc)
        rows = pl.ds(r0, tl)
        ab[rows, :] = a_b
        bb[rows, :] = b_b
        hh, carry = _scan_rows(a_f, b_f, carry, False)
        hf[rows, :] = hh
        return carry

    def bwd(n, carry):
        rows = pl.ds(pl.multiple_of(n * tl, tl), tl)
        hh, carry = _scan_rows(ab[rows, :], bb[rows, :], carry, True)
        o_ref[0, rows, :] = ((hf[rows, :] + hh) * jax.nn.gelu(y_ref[0, rows, :])).astype(o_ref.dtype)
        return carry

    zero = jnp.zeros((1, w), F32)
    lax.fori_loop(0, ntl, fwd, zero)
    c = lax.fori_loop(0, ctl, lambda t, c: bwd(ctl - 1 - t, c), zero)
    lax.fori_loop(0, ntl - ctl, lambda t, c: bwd(ntl - 1 - t, c), c)


def _rglru(p3, conv_wb, gate_w, gate_b, sp, l_len):
    b, s, _ = p3.shape
    nh = LRU_WIDTH // LRU_HALF
    xoff = (4 * RET_HEADS * RET_DK) // LRU_HALF
    yoff = xoff + nh
    return pl.pallas_call(
        functools.partial(_lru_kernel, s_len=s, l_len=l_len),
        out_shape=jax.ShapeDtypeStruct((b, s, LRU_WIDTH), BF16),
        grid=(b, nh),
        in_specs=[pl.BlockSpec((1, s, LRU_HALF), lambda bi, j: (bi, 0, xoff + j)),
                  pl.BlockSpec((1, s, LRU_HALF), lambda bi, j: (bi, 0, yoff + j)),
                  pl.BlockSpec((1, 8, LRU_HALF), lambda bi, j: (j, 0, 0)),
                  pl.BlockSpec((1, LRU_HALF, 4 * LRU_HALF), lambda bi, j: (j, 0, 0)),
                  pl.BlockSpec((1, 1, 4 * LRU_HALF), lambda bi, j: (j, 0, 0)),
                  pl.BlockSpec((1, 8, LRU_HALF), lambda bi, j: (j, 0, 0))],
        out_specs=pl.BlockSpec((1, s, LRU_HALF), lambda bi, j: (bi, 0, j)),
        scratch_shapes=[pltpu.VMEM((s + 16, LRU_HALF), F32), pltpu.VMEM((s, LRU_HALF), F32),
                        pltpu.VMEM((s, LRU_HALF), F32), pltpu.VMEM((s, LRU_HALF), F32)],
        compiler_params=_cparams("arbitrary", "arbitrary"), name="rglru",
    )(p3, p3, conv_wb, gate_w, gate_b, sp)


def _lru_params(conv_w, conv_b, gate_w, gate_b, lam):
    nh = LRU_WIDTH // LRU_HALF
    bph = LRU_HALF // LRU_BLOCK
    cw = jnp.concatenate([conv_w, conv_b[None, :], jnp.zeros((3, LRU_WIDTH), F32)], axis=0)
    cw = cw.reshape(8, nh, LRU_HALF).transpose(1, 0, 2)
    eye = jnp.eye(bph, dtype=F32)
    gw = gate_w.reshape(2, 2, nh, bph, LRU_BLOCK, LRU_BLOCK)
    dense = jnp.einsum('dgjkio,kl->jkidglo', gw, eye)
    dense = dense.reshape(nh, LRU_HALF, 4 * LRU_HALF).astype(BF16)
    gb = gate_b.reshape(2, 2, nh, LRU_HALF).transpose(2, 0, 1, 3).reshape(nh, 1, 4 * LRU_HALF)
    sp = jax.nn.softplus(-lam.astype(F32)).reshape(2, nh, LRU_HALF).transpose(1, 0, 2)
    sp = jnp.concatenate([sp, jnp.zeros((nh, 6, LRU_HALF), F32)], axis=1)
    return cw, dense, gb, sp


def _post_kernel(x_ref, ma_ref, mb_ref, w_ref, mod_ref, g_ref, wr_ref, br_ref, xo_ref, h_ref, r_ref, cnt_ref):
    m = jnp.concatenate([ma_ref[...], mb_ref[...]], axis=1)
    o = _dot(m, w_ref[...])
    x = x_ref[...] + mod_ref[0, 2:3, :] * o
    xo_ref[...] = x
    h = _modulated(x, g_ref, mod_ref, 3)
    h_ref[...] = h
    h_hi = h.astype(BF16)
    h_lo = (h - h_hi.astype(F32)).astype(BF16)
    part = _dot(h_hi, wr_ref[...])
    lg = part[:, 0:LANES] + part[:, LANES:2 * LANES] + _dot(h_lo, wr_ref[:, 0:LANES]) + br_ref[...]
    lane = lax.broadcasted_iota(jnp.int32, lg.shape, 1)
    lanef = lane.astype(F32)
    ninf = -jnp.inf
    big = float(LANES)
    gl = jnp.where(lane < MOE_GROUPS, lg, ninf)
    gmax = jnp.max(gl, axis=1, keepdims=True)
    g_top = 1.0 / jnp.sum(jnp.exp(gl - gmax), axis=1, keepdims=True)
    g_sel = jnp.min(jnp.where(gl == gmax, lanef, big), axis=1, keepdims=True)
    lo = MOE_GROUPS + MOE_PER_GROUP * g_sel
    el = jnp.where((lanef >= lo) & (lanef < lo + MOE_PER_GROUP), lg, ninf)
    emax = jnp.max(el, axis=1, keepdims=True)
    esum = jnp.sum(jnp.exp(el - emax), axis=1, keepdims=True)
    i1 = jnp.min(jnp.where(el == emax, lanef, big), axis=1, keepdims=True)
    el2 = jnp.where(lanef == i1, ninf, el)
    m2 = jnp.max(el2, axis=1, keepdims=True)
    i2 = jnp.min(jnp.where(el2 == m2, lanef, big), axis=1, keepdims=True)
    p1 = 1.0 / esum
    p2 = jnp.exp(m2 - emax) / esum
    tot = p1 + p2
    w1 = g_top * (p1 / tot)
    w2 = g_top * (p2 / tot)
    hit1 = lanef == i1
    hit2 = lanef == i2
    onehot = jnp.where(hit1, 1.0, 0.0) + jnp.where(hit2, 1.0, 0.0)
    ti = lax.broadcasted_iota(jnp.int32, (ROW_TILE, ROW_TILE), 0)
    tj = lax.broadcasted_iota(jnp.int32, (ROW_TILE, ROW_TILE), 1)
    earlier = jnp.where(tj < ti, 1.0, 0.0).astype(BF16)

    @pl.when(pl.program_id(0) == 0)
    def _():
        cnt_ref[...] = jnp.zeros(cnt_ref.shape, F32)

    before = _dot(earlier, onehot.astype(BF16)) + cnt_ref[0:1, :]
    k1 = jnp.sum(jnp.where(hit1, before, 0.0), axis=1, keepdims=True)
    k2 = jnp.sum(jnp.where(hit2, before, 0.0), axis=1, keepdims=True)
    cnt_ref[0:1, :] = cnt_ref[0:1, :] + jnp.sum(onehot, axis=0, keepdims=True)
    vals = (i1 - MOE_GROUPS, i2 - MOE_GROUPS, w1, w2, k1, k2)
    slab = jnp.zeros(lg.shape, F32)
    for col, v in enumerate(vals):
        slab = jnp.where(lane == col, v, slab)
    r_ref[...] = slab


def _post(rt, x, ma, mb, cb, w_out, mods, g2, wr, br):
    d = x.shape[1]
    hd = d // 2
    return pl.pallas_call(
        _post_kernel,
        out_shape=[jax.ShapeDtypeStruct((rt.rows, d), F32), jax.ShapeDtypeStruct((rt.rows, d), F32),
                   jax.ShapeDtypeStruct((rt.rows, LANES), F32), jax.ShapeDtypeStruct((8, LANES), F32)],
        grid=(rt.n_tiles,),
        in_specs=[pl.BlockSpec((ROW_TILE, d), lambda i: (i, 0)),
                  pl.BlockSpec((ROW_TILE, hd), lambda i: (i, 0)),
                  pl.BlockSpec((ROW_TILE, hd), lambda i: (i, cb)),
                  pl.BlockSpec((d, d), lambda i: (0, 0)),
                  pl.BlockSpec((1, 8, d), lambda i: (rt.mod_idx(i), 0, 0)),
                  pl.BlockSpec((1, d), lambda i: (0, 0)),
                  pl.BlockSpec((d, 2 * LANES), lambda i: (0, 0)),
                  pl.BlockSpec((1, LANES), lambda i: (0, 0))],
        out_specs=[pl.BlockSpec((ROW_TILE, d), lambda i: (i, 0)),
                   pl.BlockSpec((ROW_TILE, d), lambda i: (i, 0)),
                   pl.BlockSpec((ROW_TILE, LANES), lambda i: (i, 0)),
                   pl.BlockSpec((8, LANES), lambda i: (0, 0))],
        compiler_params=_cparams("arbitrary"), name="post",
    )(x, ma, mb, w_out, mods, g2.reshape(1, d), wr, br)


def _moe_plan(route, cnt):
    mb = MOE_ROWS
    t_count = route.shape[0]
    nb = (2 * t_count + MOE_EXPERTS * (mb - 1) + mb - 1) // mb
    counts = cnt[0, MOE_GROUPS:MOE_GROUPS + MOE_EXPERTS].astype(jnp.int32)
    padded = (counts + mb - 1) // mb * mb
    pend = jnp.cumsum(padded)
    pstart = pend - padded
    experts = jnp.arange(MOE_EXPERTS, dtype=jnp.int32)
    e = route[:, 0:2].astype(jnp.int32)
    first = jnp.sum(jnp.where(e[:, :, None] == experts[None, None, :], pstart[None, None, :], 0), axis=-1)
    dest = (first + route[:, 4:6].astype(jnp.int32)).reshape(-1)
    blk0 = jnp.arange(nb, dtype=jnp.int32) * mb
    block_e = jnp.minimum(jnp.sum((blk0[:, None] >= pend[None, :]).astype(jnp.int32), axis=1), MOE_EXPERTS - 1)
    sel = block_e[:, None] == experts[None, :]
    used = blk0 - jnp.sum(jnp.where(sel, pstart[None, :], 0), axis=1)
    n_valid = jnp.clip(jnp.sum(jnp.where(sel, counts[None, :], 0), axis=1) - used, 0, mb).astype(jnp.int32)
    fill = jnp.concatenate([pstart + counts, padded - counts, pend[-1:], nb - pend[-1:] // mb])
    return dest, block_e, n_valid, fill.astype(jnp.int32), nb


def _dispatch_kernel(dest_ref, fill_ref, h_ref, xs_hbm, stage, zbuf, sem, zsems, *, nt):
    i = pl.program_id(0)
    slot = i % 2
    zsem = zsems.at[0]

    def wait_tile(sl):
        for _ in range(2):
            pltpu.make_async_copy(stage.at[sl], xs_hbm.at[pl.ds(0, ROW_TILE)], sem.at[sl]).wait()

    def zero_padding(wait):
        def go(cp):
            cp.wait() if wait else cp.start()

        def one_row(r):
            go(pltpu.make_async_copy(zbuf.at[pl.ds(0, 1)], xs_hbm.at[pl.ds(r, 1)], zsem))

        def per_expert(e, c):
            start = fill_ref[e]
            n = fill_ref[MOE_EXPERTS + e]
            head = jnp.minimum((8 - (start & 7)) & 7, n)
            mid = pl.multiple_of(lax.shift_left(lax.shift_right_logical(n - head, 3), 3), 8)
            lax.fori_loop(0, head, lambda j, c2: (one_row(start + j), c2)[1], 0)

            @pl.when(mid > 0)
            def _():
                at = pl.multiple_of(start + head, 8)
                go(pltpu.make_async_copy(zbuf.at[pl.ds(0, mid)], xs_hbm.at[pl.ds(at, mid)], zsem))

            lax.fori_loop(0, n - head - mid, lambda j, c2: (one_row(start + head + mid + j), c2)[1], 0)
            return c
        lax.fori_loop(0, MOE_EXPERTS, per_expert, 0)

        def per_block(j, c):
            at = pl.multiple_of(fill_ref[2 * MOE_EXPERTS] + j * MOE_ROWS, MOE_ROWS)
            go(pltpu.make_async_copy(zbuf, xs_hbm.at[pl.ds(at, MOE_ROWS)], zsem))
            return c
        lax.fori_loop(0, fill_ref[2 * MOE_EXPERTS + 1], per_block, 0)

    @pl.when(i == 0)
    def _():
        zbuf[...] = jnp.zeros(zbuf.shape, zbuf.dtype)
        zero_padding(False)

    @pl.when(i >= 2)
    def _():
        wait_tile(slot)

    stage[slot] = h_ref[...]

    def body(j, c):
        tok = i * ROW_TILE + j
        src = stage.at[slot, pl.ds(j, 1)]
        pltpu.make_async_copy(src, xs_hbm.at[pl.ds(dest_ref[2 * tok], 1)], sem.at[slot]).start()
        pltpu.make_async_copy(src, xs_hbm.at[pl.ds(dest_ref[2 * tok + 1], 1)], sem.at[slot]).start()
        return c
    lax.fori_loop(0, ROW_TILE, body, 0, unroll=8)

    @pl.when(i == nt - 1)
    def _():
        wait_tile(slot)
        if nt > 1:
            wait_tile(1 - slot)
        zero_padding(True)


def _dispatch(h, dest, fill, n_rows):
    t_count, d = h.shape
    nt = t_count // ROW_TILE
    assert MOE_ROWS <= ROW_TILE
    grid_spec = pltpu.PrefetchScalarGridSpec(
        num_scalar_prefetch=2, grid=(nt,),
        in_specs=[pl.BlockSpec((ROW_TILE, d), lambda i, de, fi: (i, 0))],
        out_specs=pl.BlockSpec(memory_space=pl.ANY),
        scratch_shapes=[pltpu.VMEM((2, ROW_TILE, d), F32), pltpu.VMEM((MOE_ROWS, d), F32),
                        pltpu.SemaphoreType.DMA((2,)), pltpu.SemaphoreType.DMA((1,))])
    return pl.pallas_call(
        functools.partial(_dispatch_kernel, nt=nt),
        out_shape=jax.ShapeDtypeStruct((n_rows, d), F32),
        grid_spec=grid_spec,
        compiler_params=_cparams("arbitrary"), name="dispatch",
    )(dest, fill, h)


def _expert_kernel(be_ref, nv_ref, x_ref, wg_ref, wu_ref, wd_ref, y_ref, wgb, wub, wdb):
    i = pl.program_id(0)

    @pl.when(nv_ref[i] > 0)
    def _():
        @pl.when((i == 0) | (be_ref[i] != be_ref[jnp.maximum(i - 1, 0)]))
        def _():
            wgb[...] = wg_ref[0, 0].astype(BF16)
            wub[...] = wu_ref[0, 0].astype(BF16)
            wdb[...] = wd_ref[0, 0].astype(BF16)

        x = x_ref[...].astype(BF16)
        a = (jax.nn.silu(_dot(x, wgb[...])) * _dot(x, wub[...])).astype(BF16)
        y_ref[...] = _dot(a, wdb[...])

    @pl.when(nv_ref[i] == 0)
    def _():
        y_ref[...] = jnp.zeros(y_ref.shape, y_ref.dtype)


def _experts(xs, block_e, n_valid, layer, wg, wu, wd):
    n_rows, d = xs.shape
    mb = MOE_ROWS
    hid = wg.shape[3]

    def wspec(shape):
        return pl.BlockSpec(shape, lambda i, be, nv: (layer, be[i], 0, 0))

    grid_spec = pltpu.PrefetchScalarGridSpec(
        num_scalar_prefetch=2, grid=(n_rows // mb,),
        in_specs=[pl.BlockSpec((mb, d), lambda i, be, nv: (i, 0)),
                  wspec((1, 1, d, hid)), wspec((1, 1, d, hid)), wspec((1, 1, hid, d))],
        out_specs=pl.BlockSpec((mb, d), lambda i, be, nv: (i, 0)),
        scratch_shapes=[pltpu.VMEM((d, hid), BF16), pltpu.VMEM((d, hid), BF16), pltpu.VMEM((hid, d), BF16)])
    return pl.pallas_call(
        _expert_kernel,
        out_shape=jax.ShapeDtypeStruct((n_rows, d), F32),
        grid_spec=grid_spec,
        compiler_params=_cparams("arbitrary"), name="experts",
    )(block_e, n_valid, xs, wg, wu, wd)


def _final_kernel(dest_ref, x_ref, r_ref, modp_ref, ys_hbm, g_ref, o_ref, ybuf, sem, *, tile_of):
    x, _ = _moe_update(dest_ref, x_ref, r_ref, modp_ref, ys_hbm, ybuf, sem, tile_of, False)
    o_ref[...] = _rms(x, g_ref[...])


def _final(rt, x, moe, final_g, n_len):
    d = x.shape[1]
    lt = n_len // ROW_TILE

    def tile_of(i):
        return (i // lt) * rt.tpb + rt.ctx_tiles + i % lt

    mspecs, margs, scratch = _moe_operands(rt, d, moe, tile_of)
    grid_spec = pltpu.PrefetchScalarGridSpec(
        num_scalar_prefetch=1, grid=(rt.b * lt,),
        in_specs=[pl.BlockSpec((ROW_TILE, d), lambda i, de: (tile_of(i), 0))] + mspecs + [
            pl.BlockSpec((1, d), lambda i, de: (0, 0))],
        out_specs=pl.BlockSpec((ROW_TILE, d), lambda i, de: (i, 0)),
        scratch_shapes=scratch)
    return pl.pallas_call(
        functools.partial(_final_kernel, tile_of=tile_of),
        out_shape=jax.ShapeDtypeStruct((rt.b * n_len, d), F32),
        grid_spec=grid_spec,
        compiler_params=_cparams("arbitrary"), name="final",
    )(moe[3], x, *margs, final_g.reshape(1, d))


def _pre_mla_kernel(dest_ref, x_ref, r_ref, modp_ref, ys_hbm, mod_ref, g_ref, win_ref, qg_ref, kvg_ref,
                    wq_ref, wqs_ref, wk_ref, wv_ref, vone_ref, ct_ref, st_ref,
                    xo_ref, q_ref, k_ref, v_ref, ybuf, sem):
    x, finish = _moe_update(dest_ref, x_ref, r_ref, modp_ref, ys_hbm, ybuf, sem, lambda t: t, True)
    xo_ref[...] = x
    h = _modulated(x, g_ref, mod_ref, 0)
    p = _dot(h.astype(BF16), win_ref[...])
    cq = _rms(p[:, 0:MLA_Q_RANK], qg_ref[...]).astype(BF16)
    ckv = _rms(p[:, MLA_Q_RANK:MLA_Q_RANK + MLA_KV_RANK], kvg_ref[...]).astype(BF16)
    off = MLA_Q_RANK + MLA_KV_RANK
    ct, st = ct_ref[...], st_ref[...]
    k_rope = p[:, off:off + HEAD_PAD] * ct + p[:, off + HEAD_PAD:off + 2 * HEAD_PAD] * st
    qa = _dot(cq, wq_ref[...])
    qb = _dot(cq, wqs_ref[...])
    kn = _dot(ckv, wk_ref[...])
    v_ref[...] = (_dot(ckv, wv_ref[...]) + vone_ref[...]).astype(v_ref.dtype)
    for hh in range(MLA_HEADS):
        sl = slice(hh * HEAD_PAD, (hh + 1) * HEAD_PAD)
        q_ref[:, sl] = ((qa[:, sl] * ct + qb[:, sl] * st) * (MLA_SCALE * LOG2_E)).astype(q_ref.dtype)
        k_ref[:, sl] = (kn[:, sl] + k_rope).astype(k_ref.dtype)
    finish()


def _pre_mla(rt, x, moe, mods, g1, wts, ct, st):
    d = x.shape[1]
    w_in, qg, kvg, wq, wqs, wk, wv, vone = wts

    def full(a):
        return pl.BlockSpec(a.shape, lambda i, de: (0,) * a.ndim)

    hq = MLA_HEADS * HEAD_PAD
    mspecs, margs, scratch = _moe_operands(rt, d, moe, lambda t: t)
    grid_spec = pltpu.PrefetchScalarGridSpec(
        num_scalar_prefetch=1, grid=(rt.n_tiles,),
        in_specs=[pl.BlockSpec((ROW_TILE, d), lambda i, de: (i, 0))] + mspecs + [
            pl.BlockSpec((1, 8, d), lambda i, de: (rt.mod_idx(i), 0, 0)),
            pl.BlockSpec((1, d), lambda i, de: (0, 0)),
            full(w_in), full(qg), full(kvg), full(wq), full(wqs), full(wk), full(wv), full(vone),
            pl.BlockSpec((ROW_TILE, HEAD_PAD), lambda i, de: (rt.pos_idx(i), 0)),
            pl.BlockSpec((ROW_TILE, HEAD_PAD), lambda i, de: (rt.pos_idx(i), 0))],
        out_specs=[pl.BlockSpec((ROW_TILE, d), lambda i, de: (i, 0)),
                   pl.BlockSpec((ROW_TILE, hq), lambda i, de: (i, 0)),
                   pl.BlockSpec((ROW_TILE, hq), lambda i, de: (i, 0)),
                   pl.BlockSpec((ROW_TILE, hq), lambda i, de: (i, 0))],
        scratch_shapes=scratch)
    return pl.pallas_call(
        _pre_mla_kernel,
        out_shape=[jax.ShapeDtypeStruct((rt.rows, d), F32), jax.ShapeDtypeStruct((rt.rows, hq), BF16),
                   jax.ShapeDtypeStruct((rt.rows, hq), BF16), jax.ShapeDtypeStruct((rt.rows, hq), BF16)],
        grid_spec=grid_spec,
        compiler_params=_cparams("arbitrary"), name="pre_mla",
    )(moe[3], x, *margs, mods, g1.reshape(1, d), w_in, qg, kvg, wq, wqs, wk, wv, vone, ct, st)


def _mla_params(w_in, q_g, kv_g, w_uq, w_ukv):
    d = w_in.shape[0]
    hp, hr = HEAD_PAD, MLA_ROPE // 2
    nq = MLA_NOPE + MLA_ROPE
    kr = w_in[:, MLA_Q_RANK + MLA_KV_RANK:]
    z = jnp.zeros((d, MLA_NOPE), F32)
    zt = jnp.zeros((d, hp - nq), F32)
    kr_a = jnp.concatenate([z, kr, zt], axis=1)
    kr_b = jnp.concatenate([z, -kr[:, hr:], kr[:, :hr], zt], axis=1)
    w_in_p = jnp.concatenate([w_in[:, :MLA_Q_RANK + MLA_KV_RANK], kr_a, kr_b], axis=1).astype(BF16)
    wq = w_uq.reshape(MLA_Q_RANK, MLA_HEADS, nq)
    zq = jnp.zeros((MLA_Q_RANK, MLA_HEADS, hp - nq), F32)
    wq_a = jnp.concatenate([wq, zq], axis=2).reshape(MLA_Q_RANK, MLA_HEADS * hp).astype(BF16)
    wq_b = jnp.concatenate([jnp.zeros_like(wq[:, :, :MLA_NOPE]), -wq[:, :, MLA_NOPE + hr:],
                            wq[:, :, MLA_NOPE:MLA_NOPE + hr], zq], axis=2)
    wq_b = wq_b.reshape(MLA_Q_RANK, MLA_HEADS * hp).astype(BF16)
    wkv = w_ukv.reshape(MLA_KV_RANK, MLA_HEADS, MLA_NOPE + MLA_V)
    wk = jnp.concatenate([wkv[:, :, :MLA_NOPE], jnp.zeros((MLA_KV_RANK, MLA_HEADS, hp - MLA_NOPE), F32)], axis=2)
    wk = wk.reshape(MLA_KV_RANK, MLA_HEADS * hp).astype(BF16)
    wv = wkv[:, :, MLA_NOPE:].reshape(MLA_KV_RANK, MLA_HEADS // 2, 2, MLA_V)
    zv = jnp.zeros((MLA_KV_RANK, MLA_HEADS // 2, hp - MLA_V), F32)
    wv = jnp.concatenate([wv[:, :, 0], zv, zv, wv[:, :, 1]], axis=2).reshape(MLA_KV_RANK, MLA_HEADS * hp)
    lane = jnp.arange(2 * hp) % (2 * hp)
    vone = jnp.tile(jnp.where((lane == ATT_DEN_EVEN) | (lane == hp + ATT_DEN_ODD), 1.0, 0.0), MLA_HEADS // 2)
    return (w_in_p, q_g.reshape(1, -1), kv_g.reshape(1, -1), wq_a, wq_b, wk, wv.astype(BF16),
            vone.reshape(1, -1).astype(F32))


def _attn_kernel(q_ref, k_ref, v_ref, o_ref, *, s_len, l_len):
    t = pl.program_id(2)
    lane = lax.broadcasted_iota(jnp.int32, (ATT_TQ, 2 * MLA_V), 1)

    def attend(nk):
        for pair in range(ATT_HEADS // 2):
            outs = []
            for j, den_lane in ((2 * pair, ATT_DEN_EVEN), (2 * pair + 1, ATT_DEN_ODD)):
                blk = slice(j * HEAD_PAD, (j + 1) * HEAD_PAD)
                s = _dot_nt(q_ref[0, :, blk], k_ref[0, 0:nk, blk])
                p = jnp.exp2(s - jnp.max(s, axis=1, keepdims=True))
                o = _dot(p.astype(BF16), v_ref[0, 0:nk, blk])
                outs.append(o / o[:, den_lane:den_lane + 1])
            o_ref[0, :, pair * 2 * MLA_V:(pair + 1) * 2 * MLA_V] = (
                jnp.where(lane < MLA_V, outs[0], outs[1]).astype(o_ref.dtype))

    ctx_tiles = l_len // ATT_TQ

    @pl.when(t < ctx_tiles)
    def _():
        attend(l_len)

    @pl.when(t >= ctx_tiles)
    def _():
        attend(s_len)


def _attention(q3, k3, v3, l_len):
    b, s, _ = q3.shape
    hq = ATT_HEADS * HEAD_PAD
    hv = ATT_HEADS * MLA_V
    return pl.pallas_call(
        functools.partial(_attn_kernel, s_len=s, l_len=l_len),
        out_shape=jax.ShapeDtypeStruct((b, s, MLA_HEADS * MLA_V), BF16),
        grid=(b, MLA_HEADS // ATT_HEADS, s // ATT_TQ),
        in_specs=[pl.BlockSpec((1, ATT_TQ, hq), lambda bi, hi, ti: (bi, ti, hi)),
                  pl.BlockSpec((1, s, hq), lambda bi, hi, ti: (bi, 0, hi)),
                  pl.BlockSpec((1, s, hq), lambda bi, hi, ti: (bi, 0, hi))],
        out_specs=pl.BlockSpec((1, ATT_TQ, hv), lambda bi, hi, ti: (bi, ti, hi)),
        compiler_params=_cparams("arbitrary", "arbitrary", "arbitrary"), name="attention",
    )(q3, k3, v3)


def _ret_tables(n, l):
    t = jnp.arange(n, dtype=F32)
    inv = ROPE_BASE ** (-jnp.arange(0, RET_DK, 2, dtype=F32) / RET_DK)
    ang = t[:, None] * inv[None, :]
    cos, sin = jnp.cos(ang), jnp.sin(ang)
    cos2 = jnp.concatenate([jnp.ones((l, RET_DK), F32), jnp.concatenate([cos, cos], axis=1)], axis=0)
    sin2 = jnp.concatenate([jnp.zeros((l, RET_DK), F32), jnp.concatenate([-sin, sin], axis=1)], axis=0)
    return cos2, sin2


def _mla_tables(n, l):
    rows = n // GRID_W
    r_pos = jnp.repeat(jnp.arange(rows, dtype=F32), GRID_W)
    c_pos = jnp.tile(jnp.arange(GRID_W, dtype=F32), rows)
    ax = MLA_ROPE // 2
    inv = ROPE_BASE ** (-jnp.arange(0, ax, 2, dtype=F32) / ax)
    ang = jnp.concatenate([r_pos[:, None] * inv[None, :], c_pos[:, None] * inv[None, :]], axis=-1)
    cos, sin = jnp.cos(ang), jnp.sin(ang)
    pad = HEAD_PAD - MLA_NOPE - MLA_ROPE
    ct_l = jnp.concatenate([jnp.ones((n, MLA_NOPE), F32), cos, cos, jnp.zeros((n, pad), F32)], axis=1)
    st_l = jnp.concatenate([jnp.zeros((n, MLA_NOPE), F32), sin, sin, jnp.zeros((n, pad), F32)], axis=1)
    ct_c = jnp.concatenate([jnp.ones((l, MLA_NOPE + MLA_ROPE), F32), jnp.zeros((l, pad), F32)], axis=1)
    return jnp.concatenate([ct_c, ct_l], axis=0), jnp.concatenate([jnp.zeros((l, HEAD_PAD), F32), st_l], axis=0)


def kernel(x, c, ctx, c_ctx, ada_w, ada_b, norm_g, ab_w_in, ab_w_out, ret_decay_logit, lru_conv_w, lru_conv_b, lru_gate_w, lru_gate_b, lru_lambda, mla_w_in, mla_q_norm_g, mla_kv_norm_g, mla_w_uq, mla_w_ukv, mla_w_out, moe_group_w, moe_group_b, moe_expert_w, moe_expert_b, moe_w_gate, moe_w_up, moe_w_down, final_norm_g):
    b, n, d = x.shape
    l = ctx.shape[1]
    s = l + n
    depth = ada_w.shape[0]
    rt = _Rows(b, s, l)

    nrow = (b + 1 + 7) // 8 * 8
    cvec = jnp.concatenate([c, c_ctx[None, :], jnp.zeros((nrow - b - 1, d), F32)], axis=0)
    ada = _ada_all(cvec, ada_w, ada_b)

    def layer_mods(layer):
        lat = ada[layer, :b].reshape(b, 1, 6, d)
        cx = jnp.broadcast_to(ada[layer, b].reshape(1, 1, 6, d), (b, 1, 6, d))
        m = jnp.concatenate([cx, lat], axis=1)
        m = jnp.concatenate([m, jnp.zeros((b, 2, 2, d), F32)], axis=2)
        return m.reshape(2 * b, 8, d)

    cos2, sin2 = _ret_tables(n, l)
    ct, st = _mla_tables(n, l)

    xs = jnp.concatenate([ctx, x], axis=1).reshape(b * s, d)
    out = None
    moe = None
    for layer in range(depth):
        mods = layer_mods(layer)
        i = layer // 2
        if layer % 2 == 0:
            xs, p = _pre_ab(rt, xs, moe, mods, norm_g[layer, 0], ab_w_in[i].astype(BF16))
            p3 = p.reshape(b, s, -1)
            lg = jax.nn.log_sigmoid(ret_decay_logit[i].astype(F32))
            lgv = jnp.broadcast_to(lg.T[:, :, None], (RET_HEADS, 2, LANES))
            lgv = jnp.concatenate([lgv, jnp.zeros((RET_HEADS, 6, LANES), F32)], axis=1)
            ma = _retention(p3, cos2, sin2, lgv, l).reshape(b * s, -1)
            mb = _rglru(p3, *_lru_params(lru_conv_w[i], lru_conv_b[i], lru_gate_w[i], lru_gate_b[i],
                                         lru_lambda[i]), l).reshape(b * s, -1)
            cb = 0
            w_out = ab_w_out[i].astype(BF16)
        else:
            wts = _mla_params(mla_w_in[i], mla_q_norm_g[i], mla_kv_norm_g[i], mla_w_uq[i], mla_w_ukv[i])
            xs, q, k, v = _pre_mla(rt, xs, moe, mods, norm_g[layer, 0], wts, ct, st)
            att = _attention(q.reshape(b, s, -1), k.reshape(b, s, -1), v.reshape(b, s, -1), l)
            ma = mb = att.reshape(b * s, -1)
            cb = 1
            w_out = mla_w_out[i].astype(BF16)
        wr = jnp.concatenate([moe_group_w[layer], moe_expert_w[layer],
                              jnp.zeros((d, LANES - MOE_GROUPS - MOE_EXPERTS), F32)], axis=1)
        wr_hi = wr.astype(BF16)
        wr = jnp.concatenate([wr_hi, (wr - wr_hi.astype(F32)).astype(BF16)], axis=1)
        br = jnp.concatenate([moe_group_b[layer], moe_expert_b[layer],
                              jnp.zeros((LANES - MOE_GROUPS - MOE_EXPERTS,), F32)]).reshape(1, LANES)
        xs, h2, route, cnt = _post(rt, xs, ma, mb, cb, w_out, mods, norm_g[layer, 1], wr, br)
        dest, block_e, n_valid, fill, nb = _moe_plan(route, cnt)
        xsort = _dispatch(h2, dest, fill, nb * MOE_ROWS)
        ys = _experts(xsort, block_e, n_valid, layer, moe_w_gate, moe_w_up, moe_w_down)
        moe = (route, mods, ys, dest)
    out = _final(rt, xs, moe, final_norm_g, n)
    return out.reshape(b, n, d)
```

```python
import functools

import jax
import jax.numpy as jnp
from jax import lax
from jax.experimental import pallas as pl
from jax.experimental.pallas import tpu as pltpu

F32 = jnp.float32
BF16 = jnp.bfloat16

EPS = 1e-6
ROPE_BASE = 10000.0
GRID_W = 64

RET_HEADS = 4
RET_DK = 128
RET_CHUNK = 128
RET_HPS = 2
LRU_WIDTH = 512
LRU_BLOCK = 64
LRU_C = 8.0
LRU_HALF = 256
LRU_TILE = 128
LRU_SUB = 8

MLA_HEADS = 16
MLA_NOPE = 64
MLA_ROPE = 32
MLA_V = 64
MLA_Q_RANK = 384
MLA_KV_RANK = 256
MLA_SCALE = (MLA_NOPE + MLA_ROPE) ** -0.5
LOG2_E = 1.4426950408889634
HEAD_PAD = 128
MLA_CHUNK = 512

MOE_GROUPS = 4
MOE_PER_GROUP = 8
MOE_EXPERTS = 32
MOE_ROWS = 256

ROW_TILE = 256
ATT_TQ = 256
ATT_HEADS = 4
ATT_DEN_EVEN = 64
ATT_DEN_ODD = 0
LANES = 128
VMEM_LIMIT = 56 * 1024 * 1024


def _cparams(*sem):
    return pltpu.CompilerParams(dimension_semantics=sem, vmem_limit_bytes=VMEM_LIMIT)


def _rms(x, g):
    return x * lax.rsqrt(jnp.mean(x * x, axis=-1, keepdims=True) + EPS) * g


def _dot(a, b):
    return jnp.dot(a, b, preferred_element_type=F32)


def _dot_nt(a, b):
    return lax.dot_general(a, b, (((1,), (1,)), ((), ())), preferred_element_type=F32)


def _dot_tn(a, b):
    return lax.dot_general(a, b, (((0,), (0,)), ((), ())), preferred_element_type=F32)


def _ada_kernel(s_ref, w_ref, b_ref, o_ref):
    s = jax.nn.silu(s_ref[...])
    o_ref[0] = _dot(s.astype(BF16), w_ref[0].astype(BF16)) + b_ref[0]


def _ada_all(cvec, ada_w, ada_b):
    depth, d, n6 = ada_w.shape
    rows = cvec.shape[0]
    tn = n6 // 4
    return pl.pallas_call(
        _ada_kernel,
        out_shape=jax.ShapeDtypeStruct((depth, rows, n6), F32),
        grid=(depth, n6 // tn),
        in_specs=[pl.BlockSpec((rows, d), lambda l, j: (0, 0)),
                  pl.BlockSpec((1, d, tn), lambda l, j: (l, 0, j)),
                  pl.BlockSpec((1, 1, tn), lambda l, j: (l, 0, j))],
        out_specs=pl.BlockSpec((1, rows, tn), lambda l, j: (l, 0, j)),
        compiler_params=_cparams("arbitrary", "arbitrary"),
        name="adaln",
    )(cvec, ada_w, ada_b.reshape(depth, 1, n6))


class _Rows:
    def __init__(self, b, s, l):
        assert s % ROW_TILE == 0 and l % ROW_TILE == 0
        self.b, self.s, self.l = b, s, l
        self.tpb = s // ROW_TILE
        self.ctx_tiles = l // ROW_TILE
        self.n_tiles = b * self.tpb
        self.rows = b * s

    def mod_idx(self, i):
        return 2 * (i // self.tpb) + jnp.where(i % self.tpb >= self.ctx_tiles, 1, 0)

    def pos_idx(self, i):
        return i % self.tpb


def _modulated(x, g_ref, mod_ref, base):
    h = _rms(x, g_ref[...])
    return h * (1.0 + mod_ref[0, base + 1:base + 2, :]) + mod_ref[0, base:base + 1, :]


def _moe_update(dest_ref, x_ref, r_ref, modp_ref, ys_hbm, ybuf, sem, tile_of, inline):
    i = pl.program_id(0)
    n = pl.num_programs(0)
    slot = i % 2

    def start(tok, j, sl):
        for k in range(2):
            pltpu.make_async_copy(ys_hbm.at[pl.ds(dest_ref[2 * tok + k], 1)], ybuf.at[sl, k, pl.ds(j, 1)],
                                  sem.at[sl]).start()

    def issue_loop(step, sl):
        base = tile_of(step) * ROW_TILE
        lax.fori_loop(0, ROW_TILE, lambda j, c: (start(base + j, j, sl), c)[1], 0, unroll=8)

    def wait(sl):
        for k in range(2):
            pltpu.make_async_copy(ys_hbm.at[pl.ds(0, ROW_TILE)], ybuf.at[sl, k], sem.at[sl]).wait()

    @pl.when(i == 0)
    def _():
        issue_loop(0, 0)

    if not inline:
        @pl.when(i + 1 < n)
        def _():
            issue_loop(i + 1, 1 - slot)

    wait(slot)
    r = r_ref[...]
    y = ybuf[slot, 0] * r[:, 2:3] + ybuf[slot, 1] * r[:, 3:4]
    x = x_ref[...] + modp_ref[0, 5:6, :] * y
    if not inline:
        return x, None
    base = tile_of(jnp.minimum(i + 1, n - 1)) * ROW_TILE
    for j in range(ROW_TILE):
        start(base + j, j, 1 - slot)

    def finish():
        @pl.when(i == n - 1)
        def _():
            wait(1 - slot)
    return x, finish


def _moe_operands(rt, d, moe, tile_of):
    route, modp, ys, _ = moe
    specs = [pl.BlockSpec((ROW_TILE, LANES), lambda i, de: (tile_of(i), 0)),
             pl.BlockSpec((1, 8, d), lambda i, de: (rt.mod_idx(tile_of(i)), 0, 0)),
             pl.BlockSpec(memory_space=pl.ANY)]
    scratch = [pltpu.VMEM((2, 2, ROW_TILE, d), F32), pltpu.SemaphoreType.DMA((2,))]
    return specs, [route, modp, ys], scratch


def _pre_ab_kernel(*refs, has_moe):
    if has_moe:
        dest_ref, x_ref, r_ref, modp_ref, ys_hbm, mod_ref, g_ref, w_ref, xo_ref, p_ref, ybuf, sem = refs
        x, finish = _moe_update(dest_ref, x_ref, r_ref, modp_ref, ys_hbm, ybuf, sem, lambda t: t, True)
        xo_ref[...] = x
    else:
        x_ref, mod_ref, g_ref, w_ref, p_ref = refs
        x = x_ref[...]
    h = _modulated(x, g_ref, mod_ref, 0)
    p_ref[...] = _dot(h.astype(BF16), w_ref[...])
    if has_moe:
        finish()


def _pre_ab(rt, x, moe, mods, g1, w_in):
    d = x.shape[1]
    n_out = w_in.shape[1]
    if moe is None:
        p = pl.pallas_call(
            functools.partial(_pre_ab_kernel, has_moe=False),
            out_shape=jax.ShapeDtypeStruct((rt.rows, n_out), F32),
            grid=(rt.n_tiles,),
            in_specs=[pl.BlockSpec((ROW_TILE, d), lambda i: (i, 0)),
                      pl.BlockSpec((1, 8, d), lambda i: (rt.mod_idx(i), 0, 0)),
                      pl.BlockSpec((1, d), lambda i: (0, 0)),
                      pl.BlockSpec((d, n_out), lambda i: (0, 0))],
            out_specs=pl.BlockSpec((ROW_TILE, n_out), lambda i: (i, 0)),
            compiler_params=_cparams("arbitrary"), name="pre_ab",
        )(x, mods, g1.reshape(1, d), w_in)
        return x, p
    mspecs, margs, scratch = _moe_operands(rt, d, moe, lambda t: t)
    grid_spec = pltpu.PrefetchScalarGridSpec(
        num_scalar_prefetch=1, grid=(rt.n_tiles,),
        in_specs=[pl.BlockSpec((ROW_TILE, d), lambda i, de: (i, 0))] + mspecs + [
            pl.BlockSpec((1, 8, d), lambda i, de: (rt.mod_idx(i), 0, 0)),
            pl.BlockSpec((1, d), lambda i, de: (0, 0)),
            pl.BlockSpec((d, n_out), lambda i, de: (0, 0))],
        out_specs=[pl.BlockSpec((ROW_TILE, d), lambda i, de: (i, 0)),
                   pl.BlockSpec((ROW_TILE, n_out), lambda i, de: (i, 0))],
        scratch_shapes=scratch)
    return pl.pallas_call(
        functools.partial(_pre_ab_kernel, has_moe=True),
        out_shape=[jax.ShapeDtypeStruct((rt.rows, d), F32), jax.ShapeDtypeStruct((rt.rows, n_out), F32)],
        grid_spec=grid_spec,
        compiler_params=_cparams("arbitrary"), name="pre_ab_moe",
    )(moe[3], x, *margs, mods, g1.reshape(1, d), w_in)


def _ret_kernel(q_ref, k_ref, v_ref, g_ref, cos_ref, sin_ref, lg_ref, o_ref, qs, ks, acc, *, s_len, l_len):
    c, dk = RET_CHUNK, RET_DK
    nch, cch = s_len // c, l_len // c
    ii = lax.broadcasted_iota(jnp.int32, (c, c), 0).astype(F32)
    jj = lax.broadcasted_iota(jnp.int32, (c, c), 1).astype(F32)
    diff = ii - jj
    k_scale = RET_DK ** -0.5

    def head_consts(j):
        lgf = lg_ref[j, 0:1, :]
        lgb = lg_ref[j, 1:2, :]
        dmask = (jnp.where(diff > 0, jnp.exp(lgf * jnp.maximum(diff, 0.0)), 0.0)
                 + jnp.where(diff < 0, jnp.exp(lgb * jnp.maximum(-diff, 0.0)), 0.0)
                 + jnp.where(diff == 0, 2.0, 0.0))
        return dict(dmask=dmask,
                    zeta_f=jnp.exp(lgf * (c - 1.0 - ii)), xi_f=jnp.exp(lgf * (ii + 1.0)),
                    zeta_b=jnp.exp(lgb * ii), xi_b=jnp.exp(lgb * (c - ii)),
                    cd_f=jnp.exp(lgf * c), cd_b=jnp.exp(lgb * c))

    hc = [head_consts(j) for j in range(RET_HPS)]

    def fwd(n, sts):
        rows = pl.ds(pl.multiple_of(n * c, c), c)
        cs, sn = cos_ref[rows, :], sin_ref[rows, :]
        out = []
        for j, st in enumerate(sts):
            cols = slice(j * dk, (j + 1) * dk)
            q = q_ref[0, rows, cols]
            k = k_ref[0, rows, cols]
            v = v_ref[0, rows, cols]
            qb = (q * cs + pltpu.roll(q, 64, 1) * sn).astype(BF16)
            kb = ((k * cs + pltpu.roll(k, 64, 1) * sn) * k_scale).astype(BF16)
            qs[rows, cols] = qb
            ks[rows, cols] = kb
            sc = _dot_nt(qb, kb) * hc[j]["dmask"]
            acc[rows, cols] = _dot(sc.astype(BF16), v.astype(BF16)) + _dot(qb, st.astype(BF16)) * hc[j]["xi_f"]
            out.append(hc[j]["cd_f"] * st + _dot_tn(kb, (v * hc[j]["zeta_f"]).astype(BF16)))
        return tuple(out)

    def bwd(n, sts):
        rows = pl.ds(pl.multiple_of(n * c, c), c)
        out = []
        for j, st in enumerate(sts):
            cols = slice(j * dk, (j + 1) * dk)
            qb = qs[rows, cols]
            kb = ks[rows, cols]
            v = v_ref[0, rows, cols]
            y = acc[rows, cols] + _dot(qb, st.astype(BF16)) * hc[j]["xi_b"]
            y = y * lax.rsqrt(jnp.mean(y * y, axis=-1, keepdims=True) + EPS)
            o_ref[0, rows, cols] = (y * jax.nn.silu(g_ref[0, rows, cols])).astype(o_ref.dtype)
            out.append(hc[j]["cd_b"] * st + _dot_tn(kb, (v * hc[j]["zeta_b"]).astype(BF16)))
        return tuple(out)

    zero = tuple(jnp.zeros((c, c), F32) for _ in range(RET_HPS))
    lax.fori_loop(0, nch, fwd, zero)
    sts = lax.fori_loop(0, cch, lambda t, sts: bwd(cch - 1 - t, sts), zero)
    lax.fori_loop(0, nch - cch, lambda t, sts: bwd(nch - 1 - t, sts), sts)


def _retention(p3, cos2, sin2, lgv, l_len):
    b, s, _ = p3.shape
    groups = RET_HEADS // RET_HPS
    w = RET_HPS * RET_DK

    def col(off, **kw):
        return pl.BlockSpec((1, s, w), lambda bi, hi: (bi, 0, off + hi), **kw)

    once = dict(pipeline_mode=pl.Buffered(1))
    return pl.pallas_call(
        functools.partial(_ret_kernel, s_len=s, l_len=l_len),
        out_shape=jax.ShapeDtypeStruct((b, s, RET_HEADS * RET_DK), BF16),
        grid=(b, groups),
        in_specs=[col(0), col(groups), col(2 * groups), col(3 * groups, **once),
                  pl.BlockSpec((s, RET_DK), lambda bi, hi: (0, 0), **once),
                  pl.BlockSpec((s, RET_DK), lambda bi, hi: (0, 0), **once),
                  pl.BlockSpec((RET_HPS, 8, LANES), lambda bi, hi: (hi, 0, 0))],
        out_specs=pl.BlockSpec((1, s, w), lambda bi, hi: (bi, 0, hi)),
        scratch_shapes=[pltpu.VMEM((s, w), BF16), pltpu.VMEM((s, w), BF16), pltpu.VMEM((s, w), F32)],
        compiler_params=_cparams("arbitrary", "arbitrary"), name="retention",
    )(p3, p3, p3, p3, cos2, sin2, lgv)


def _tile_scan(a, b, reverse):
    n = a.shape[0]
    rows = lax.broadcasted_iota(jnp.int32, a.shape, 0)
    step = 1
    while step < n:
        shift = n - step if reverse else step
        a_s = pltpu.roll(a, shift, 0)
        b_s = pltpu.roll(b, shift, 0)
        m = (rows < n - step) if reverse else (rows >= step)
        b = jnp.where(m, a * b_s + b, b)
        a = jnp.where(m, a * a_s, a)
        step *= 2
    return a, b


def _scan_rows(a, b, carry, reverse):
    n = a.shape[0]
    pieces = [None] * (n // LRU_SUB)
    for i in (reversed(range(len(pieces))) if reverse else range(len(pieces))):
        rows = slice(i * LRU_SUB, (i + 1) * LRU_SUB)
        a_c, h_loc = _tile_scan(a[rows], b[rows], reverse)
        pieces[i] = h_loc + a_c * carry
        carry = pieces[i][0:1] if reverse else pieces[i][LRU_SUB - 1:LRU_SUB]
    return jnp.concatenate(pieces, axis=0), carry


def _lru_kernel(x_ref, y_ref, cw_ref, wg_ref, gb_ref, sp_ref, o_ref, xpad, hf, ab, bb, *, s_len, l_len):
    tl, w = LRU_TILE, LRU_HALF
    ntl, ctl = s_len // tl, l_len // tl
    xpad[0:8, :] = jnp.zeros((8, w), F32)
    xpad[s_len + 8:s_len + 16, :] = jnp.zeros((8, w), F32)
    xpad[8:s_len + 8, :] = x_ref[0]
    w0, w1, w2, w3, cb = (cw_ref[0, t:t + 1, :] for t in range(5))
    sp_f = sp_ref[0, 0:1, :]
    sp_b = sp_ref[0, 1:2, :]
    it = lax.broadcasted_iota(jnp.int32, (tl, w), 0)

    def coeff(gr, gi, sp, xc):
        r = jax.nn.sigmoid(gr)
        i = jax.nn.sigmoid(gi)
        log_a = -LRU_C * r * sp
        th = jnp.tanh(log_a)
        return jnp.exp(log_a), jnp.sqrt(-2.0 * th / (1.0 - th)) * (i * xc)

    def fwd(n, carry):
        r0 = pl.multiple_of(n * tl, tl)
        win = xpad[pl.ds(r0, tl + 16), :]
        t = r0 + it
        seg = jnp.where(t >= l_len, 1, 0)

        def tap(d):
            v = pltpu.roll(win, (tl + 16 - d) % (tl + 16), 0)[8:8 + tl]
            return jnp.where(jnp.where(t + d >= l_len, 1, 0) == seg, v, 0.0)

        xc = tap(-2) * w0
        xc = xc + tap(-1) * w1
        xc = xc + win[8:8 + tl] * w2
        xc = xc + tap(1) * w3
        xc = xc + cb
        gts = _dot(xc.astype(BF16), wg_ref[0]) + gb_ref[0]
        a_f, b_f = coeff(gts[:, 0:w], gts[:, w:2 * w], sp_f, xc)
        a_b, b_b = coeff(gts[:, 2 * w:3 * w], gts[:, 3 * w:4 * w], sp_b, xc)
        rows = pl.ds(r0, tl)
        ab[rows, :] = a_b
        bb[rows, :] = b_b
        hh, carry = _scan_rows(a_f, b_f, carry, False)
        hf[rows, :] = hh
        return carry

    def bwd(n, carry):
        rows = pl.ds(pl.multiple_of(n * tl, tl), tl)
        hh, carry = _scan_rows(ab[rows, :], bb[rows, :], carry, True)
        o_ref[0, rows, :] = ((hf[rows, :] + hh) * jax.nn.gelu(y_ref[0, rows, :])).astype(o_ref.dtype)
        return carry

    zero = jnp.zeros((1, w), F32)
    lax.fori_loop(0, ntl, fwd, zero)
    c = lax.fori_loop(0, ctl, lambda t, c: bwd(ctl - 1 - t, c), zero)
    lax.fori_loop(0, ntl - ctl, lambda t, c: bwd(ntl - 1 - t, c), c)


def _rglru(p3, conv_wb, gate_w, gate_b, sp, l_len):
    b, s, _ = p3.shape
    nh = LRU_WIDTH // LRU_HALF
    xoff = (4 * RET_HEADS * RET_DK) // LRU_HALF
    yoff = xoff + nh
    return pl.pallas_call(
        functools.partial(_lru_kernel, s_len=s, l_len=l_len),
        out_shape=jax.ShapeDtypeStruct((b, s, LRU_WIDTH), BF16),
        grid=(b, nh),
        in_specs=[pl.BlockSpec((1, s, LRU_HALF), lambda bi, j: (bi, 0, xoff + j)),
                  pl.BlockSpec((1, s, LRU_HALF), lambda bi, j: (bi, 0, yoff + j)),
                  pl.BlockSpec((1, 8, LRU_HALF), lambda bi, j: (j, 0, 0)),
                  pl.BlockSpec((1, LRU_HALF, 4 * LRU_HALF), lambda bi, j: (j, 0, 0)),
                  pl.BlockSpec((1, 1, 4 * LRU_HALF), lambda bi, j: (j, 0, 0)),
                  pl.BlockSpec((1, 8, LRU_HALF), lambda bi, j: (j, 0, 0))],
        out_specs=pl.BlockSpec((1, s, LRU_HALF), lambda bi, j: (bi, 0, j)),
        scratch_shapes=[pltpu.VMEM((s + 16, LRU_HALF), F32), pltpu.VMEM((s, LRU_HALF), F32),
                        pltpu.VMEM((s, LRU_HALF), F32), pltpu.VMEM((s, LRU_HALF), F32)],
        compiler_params=_cparams("arbitrary", "arbitrary"), name="rglru",
    )(p3, p3, conv_wb, gate_w, gate_b, sp)


def _lru_params(conv_w, conv_b, gate_w, gate_b, lam):
    nh = LRU_WIDTH // LRU_HALF
    bph = LRU_HALF // LRU_BLOCK
    cw = jnp.concatenate([conv_w, conv_b[None, :], jnp.zeros((3, LRU_WIDTH), F32)], axis=0)
    cw = cw.reshape(8, nh, LRU_HALF).transpose(1, 0, 2)
    eye = jnp.eye(bph, dtype=F32)
    gw = gate_w.reshape(2, 2, nh, bph, LRU_BLOCK, LRU_BLOCK)
    dense = jnp.einsum('dgjkio,kl->jkidglo', gw, eye)
    dense = dense.reshape(nh, LRU_HALF, 4 * LRU_HALF).astype(BF16)
    gb = gate_b.reshape(2, 2, nh, LRU_HALF).transpose(2, 0, 1, 3).reshape(nh, 1, 4 * LRU_HALF)
    sp = jax.nn.softplus(-lam.astype(F32)).reshape(2, nh, LRU_HALF).transpose(1, 0, 2)
    sp = jnp.concatenate([sp, jnp.zeros((nh, 6, LRU_HALF), F32)], axis=1)
    return cw, dense, gb, sp


def _post_kernel(x_ref, ma_ref, mb_ref, w_ref, mod_ref, g_ref, wr_ref, br_ref, xo_ref, h_ref, r_ref, cnt_ref):
    m = jnp.concatenate([ma_ref[...], mb_ref[...]], axis=1)
    o = _dot(m, w_ref[...])
    x = x_ref[...] + mod_ref[0, 2:3, :] * o
    xo_ref[...] = x
    h = _modulated(x, g_ref, mod_ref, 3)
    h_ref[...] = h
    h_hi = h.astype(BF16)
    h_lo = (h - h_hi.astype(F32)).astype(BF16)
    part = _dot(h_hi, wr_ref[...])
    lg = part[:, 0:LANES] + part[:, LANES:2 * LANES] + _dot(h_lo, wr_ref[:, 0:LANES]) + br_ref[...]
    lane = lax.broadcasted_iota(jnp.int32, lg.shape, 1)
    lanef = lane.astype(F32)
    ninf = -jnp.inf
    big = float(LANES)
    gl = jnp.where(lane < MOE_GROUPS, lg, ninf)
    gmax = jnp.max(gl, axis=1, keepdims=True)
    g_top = 1.0 / jnp.sum(jnp.exp(gl - gmax), axis=1, keepdims=True)
    g_sel = jnp.min(jnp.where(gl == gmax, lanef, big), axis=1, keepdims=True)
    lo = MOE_GROUPS + MOE_PER_GROUP * g_sel
    el = jnp.where((lanef >= lo) & (lanef < lo + MOE_PER_GROUP), lg, ninf)
    emax = jnp.max(el, axis=1, keepdims=True)
    esum = jnp.sum(jnp.exp(el - emax), axis=1, keepdims=True)
    i1 = jnp.min(jnp.where(el == emax, lanef, big), axis=1, keepdims=True)
    el2 = jnp.where(lanef == i1, ninf, el)
    m2 = jnp.max(el2, axis=1, keepdims=True)
    i2 = jnp.min(jnp.where(el2 == m2, lanef, big), axis=1, keepdims=True)
    p1 = 1.0 / esum
    p2 = jnp.exp(m2 - emax) / esum
    tot = p1 + p2
    w1 = g_top * (p1 / tot)
    w2 = g_top * (p2 / tot)
    hit1 = lanef == i1
    hit2 = lanef == i2
    onehot = jnp.where(hit1, 1.0, 0.0) + jnp.where(hit2, 1.0, 0.0)
    ti = lax.broadcasted_iota(jnp.int32, (ROW_TILE, ROW_TILE), 0)
    tj = lax.broadcasted_iota(jnp.int32, (ROW_TILE, ROW_TILE), 1)
    earlier = jnp.where(tj < ti, 1.0, 0.0).astype(BF16)

    @pl.when(pl.program_id(0) == 0)
    def _():
        cnt_ref[...] = jnp.zeros(cnt_ref.shape, F32)

    before = _dot(earlier, onehot.astype(BF16)) + cnt_ref[0:1, :]
    k1 = jnp.sum(jnp.where(hit1, before, 0.0), axis=1, keepdims=True)
    k2 = jnp.sum(jnp.where(hit2, before, 0.0), axis=1, keepdims=True)
    cnt_ref[0:1, :] = cnt_ref[0:1, :] + jnp.sum(onehot, axis=0, keepdims=True)
    vals = (i1 - MOE_GROUPS, i2 - MOE_GROUPS, w1, w2, k1, k2)
    slab = jnp.zeros(lg.shape, F32)
    for col, v in enumerate(vals):
        slab = jnp.where(lane == col, v, slab)
    r_ref[...] = slab


def _post(rt, x, ma, mb, cb, w_out, mods, g2, wr, br):
    d = x.shape[1]
    hd = d // 2
    return pl.pallas_call(
        _post_kernel,
        out_shape=[jax.ShapeDtypeStruct((rt.rows, d), F32), jax.ShapeDtypeStruct((rt.rows, d), F32),
                   jax.ShapeDtypeStruct((rt.rows, LANES), F32), jax.ShapeDtypeStruct((8, LANES), F32)],
        grid=(rt.n_tiles,),
        in_specs=[pl.BlockSpec((ROW_TILE, d), lambda i: (i, 0)),
                  pl.BlockSpec((ROW_TILE, hd), lambda i: (i, 0)),
                  pl.BlockSpec((ROW_TILE, hd), lambda i: (i, cb)),
                  pl.BlockSpec((d, d), lambda i: (0, 0)),
                  pl.BlockSpec((1, 8, d), lambda i: (rt.mod_idx(i), 0, 0)),
                  pl.BlockSpec((1, d), lambda i: (0, 0)),
                  pl.BlockSpec((d, 2 * LANES), lambda i: (0, 0)),
                  pl.BlockSpec((1, LANES), lambda i: (0, 0))],
        out_specs=[pl.BlockSpec((ROW_TILE, d), lambda i: (i, 0)),
                   pl.BlockSpec((ROW_TILE, d), lambda i: (i, 0)),
                   pl.BlockSpec((ROW_TILE, LANES), lambda i: (i, 0)),
                   pl.BlockSpec((8, LANES), lambda i: (0, 0))],
        compiler_params=_cparams("arbitrary"), name="post",
    )(x, ma, mb, w_out, mods, g2.reshape(1, d), wr, br)


def _moe_plan(route, cnt):
    mb = MOE_ROWS
    t_count = route.shape[0]
    nb = (2 * t_count + MOE_EXPERTS * (mb - 1) + mb - 1) // mb
    counts = cnt[0, MOE_GROUPS:MOE_GROUPS + MOE_EXPERTS].astype(jnp.int32)
    padded = (counts + mb - 1) // mb * mb
    pend = jnp.cumsum(padded)
    pstart = pend - padded
    experts = jnp.arange(MOE_EXPERTS, dtype=jnp.int32)
    e = route[:, 0:2].astype(jnp.int32)
    first = jnp.sum(jnp.where(e[:, :, None] == experts[None, None, :], pstart[None, None, :], 0), axis=-1)
    dest = (first + route[:, 4:6].astype(jnp.int32)).reshape(-1)
    blk0 = jnp.arange(nb, dtype=jnp.int32) * mb
    block_e = jnp.minimum(jnp.sum((blk0[:, None] >= pend[None, :]).astype(jnp.int32), axis=1), MOE_EXPERTS - 1)
    sel = block_e[:, None] == experts[None, :]
    used = blk0 - jnp.sum(jnp.where(sel, pstart[None, :], 0), axis=1)
    n_valid = jnp.clip(jnp.sum(jnp.where(sel, counts[None, :], 0), axis=1) - used, 0, mb).astype(jnp.int32)
    fill = jnp.concatenate([pstart + counts, padded - counts, pend[-1:], nb - pend[-1:] // mb])
    return dest, block_e, n_valid, fill.astype(jnp.int32), nb


def _dispatch_kernel(dest_ref, fill_ref, h_ref, xs_hbm, stage, zbuf, sem, zsems, *, nt):
    i = pl.program_id(0)
    slot = i % 2
    zsem = zsems.at[0]

    def wait_tile(sl):
        for _ in range(2):
            pltpu.make_async_copy(stage.at[sl], xs_hbm.at[pl.ds(0, ROW_TILE)], sem.at[sl]).wait()

    def zero_padding(wait):
        def go(cp):
            cp.wait() if wait else cp.start()

        def one_row(r):
            go(pltpu.make_async_copy(zbuf.at[pl.ds(0, 1)], xs_hbm.at[pl.ds(r, 1)], zsem))

        def per_expert(e, c):
            start = fill_ref[e]
            n = fill_ref[MOE_EXPERTS + e]
            head = jnp.minimum((8 - (start & 7)) & 7, n)
            mid = pl.multiple_of(lax.shift_left(lax.shift_right_logical(n - head, 3), 3), 8)
            lax.fori_loop(0, head, lambda j, c2: (one_row(start + j), c2)[1], 0)

            @pl.when(mid > 0)
            def _():
                at = pl.multiple_of(start + head, 8)
                go(pltpu.make_async_copy(zbuf.at[pl.ds(0, mid)], xs_hbm.at[pl.ds(at, mid)], zsem))

            lax.fori_loop(0, n - head - mid, lambda j, c2: (one_row(start + head + mid + j), c2)[1], 0)
            return c
        lax.fori_loop(0, MOE_EXPERTS, per_expert, 0)

        def per_block(j, c):
            at = pl.multiple_of(fill_ref[2 * MOE_EXPERTS] + j * MOE_ROWS, MOE_ROWS)
            go(pltpu.make_async_copy(zbuf, xs_hbm.at[pl.ds(at, MOE_ROWS)], zsem))
            return c
        lax.fori_loop(0, fill_ref[2 * MOE_EXPERTS + 1], per_block, 0)

    @pl.when(i == 0)
    def _():
        zbuf[...] = jnp.zeros(zbuf.shape, zbuf.dtype)
        zero_padding(False)

    @pl.when(i >= 2)
    def _():
        wait_tile(slot)

    stage[slot] = h_ref[...]

    for j in range(ROW_TILE):
        src = stage.at[slot, pl.ds(j, 1)]
        for k in range(2):
            row = dest_ref[2 * (i * ROW_TILE + j) + k]
            pltpu.make_async_copy(src, xs_hbm.at[pl.ds(row, 1)], sem.at[slot]).start()

    @pl.when(i == nt - 1)
    def _():
        wait_tile(slot)
        if nt > 1:
            wait_tile(1 - slot)
        zero_padding(True)


def _dispatch(h, dest, fill, n_rows):
    t_count, d = h.shape
    nt = t_count // ROW_TILE
    assert MOE_ROWS <= ROW_TILE
    grid_spec = pltpu.PrefetchScalarGridSpec(
        num_scalar_prefetch=2, grid=(nt,),
        in_specs=[pl.BlockSpec((ROW_TILE, d), lambda i, de, fi: (i, 0))],
        out_specs=pl.BlockSpec(memory_space=pl.ANY),
        scratch_shapes=[pltpu.VMEM((2, ROW_TILE, d), F32), pltpu.VMEM((MOE_ROWS, d), F32),
                        pltpu.SemaphoreType.DMA((2,)), pltpu.SemaphoreType.DMA((1,))])
    return pl.pallas_call(
        functools.partial(_dispatch_kernel, nt=nt),
        out_shape=jax.ShapeDtypeStruct((n_rows, d), F32),
        grid_spec=grid_spec,
        compiler_params=_cparams("arbitrary"), name="dispatch",
    )(dest, fill, h)


def _expert_kernel(be_ref, nv_ref, x_ref, wg_ref, wu_ref, wd_ref, y_ref, wgb, wub, wdb):
    i = pl.program_id(0)

    @pl.when(nv_ref[i] > 0)
    def _():
        @pl.when((i == 0) | (be_ref[i] != be_ref[jnp.maximum(i - 1, 0)]))
        def _():
            wgb[...] = wg_ref[0, 0].astype(BF16)
            wub[...] = wu_ref[0, 0].astype(BF16)
            wdb[...] = wd_ref[0, 0].astype(BF16)

        x = x_ref[...].astype(BF16)
        a = (jax.nn.silu(_dot(x, wgb[...])) * _dot(x, wub[...])).astype(BF16)
        y_ref[...] = _dot(a, wdb[...])

    @pl.when(nv_ref[i] == 0)
    def _():
        y_ref[...] = jnp.zeros(y_ref.shape, y_ref.dtype)


def _experts(xs, block_e, n_valid, layer, wg, wu, wd):
    n_rows, d = xs.shape
    mb = MOE_ROWS
    hid = wg.shape[3]

    def wspec(shape):
        return pl.BlockSpec(shape, lambda i, be, nv: (layer, be[i], 0, 0))

    grid_spec = pltpu.PrefetchScalarGridSpec(
        num_scalar_prefetch=2, grid=(n_rows // mb,),
        in_specs=[pl.BlockSpec((mb, d), lambda i, be, nv: (i, 0)),
                  wspec((1, 1, d, hid)), wspec((1, 1, d, hid)), wspec((1, 1, hid, d))],
        out_specs=pl.BlockSpec((mb, d), lambda i, be, nv: (i, 0)),
        scratch_shapes=[pltpu.VMEM((d, hid), BF16), pltpu.VMEM((d, hid), BF16), pltpu.VMEM((hid, d), BF16)])
    return pl.pallas_call(
        _expert_kernel,
        out_shape=jax.ShapeDtypeStruct((n_rows, d), F32),
        grid_spec=grid_spec,
        compiler_params=_cparams("arbitrary"), name="experts",
    )(block_e, n_valid, xs, wg, wu, wd)


def _final_kernel(dest_ref, x_ref, r_ref, modp_ref, ys_hbm, g_ref, o_ref, ybuf, sem, *, tile_of):
    x, finish = _moe_update(dest_ref, x_ref, r_ref, modp_ref, ys_hbm, ybuf, sem, tile_of, True)
    o_ref[...] = _rms(x, g_ref[...])
    finish()


def _final(rt, x, moe, final_g, n_len):
    d = x.shape[1]
    lt = n_len // ROW_TILE

    def tile_of(i):
        return (i // lt) * rt.tpb + rt.ctx_tiles + i % lt

    mspecs, margs, scratch = _moe_operands(rt, d, moe, tile_of)
    grid_spec = pltpu.PrefetchScalarGridSpec(
        num_scalar_prefetch=1, grid=(rt.b * lt,),
        in_specs=[pl.BlockSpec((ROW_TILE, d), lambda i, de: (tile_of(i), 0))] + mspecs + [
            pl.BlockSpec((1, d), lambda i, de: (0, 0))],
        out_specs=pl.BlockSpec((ROW_TILE, d), lambda i, de: (i, 0)),
        scratch_shapes=scratch)
    return pl.pallas_call(
        functools.partial(_final_kernel, tile_of=tile_of),
        out_shape=jax.ShapeDtypeStruct((rt.b * n_len, d), F32),
        grid_spec=grid_spec,
        compiler_params=_cparams("arbitrary"), name="final",
    )(moe[3], x, *margs, final_g.reshape(1, d))


def _pre_mla_kernel(dest_ref, x_ref, r_ref, modp_ref, ys_hbm, mod_ref, g_ref, win_ref, qg_ref, kvg_ref,
                    wq_ref, wqs_ref, wk_ref, wv_ref, vone_ref, ct_ref, st_ref,
                    xo_ref, q_ref, k_ref, v_ref, ybuf, sem):
    x, finish = _moe_update(dest_ref, x_ref, r_ref, modp_ref, ys_hbm, ybuf, sem, lambda t: t, True)
    xo_ref[...] = x
    h = _modulated(x, g_ref, mod_ref, 0)
    p = _dot(h.astype(BF16), win_ref[...])
    cq = _rms(p[:, 0:MLA_Q_RANK], qg_ref[...]).astype(BF16)
    ckv = _rms(p[:, MLA_Q_RANK:MLA_Q_RANK + MLA_KV_RANK], kvg_ref[...]).astype(BF16)
    off = MLA_Q_RANK + MLA_KV_RANK
    ct, st = ct_ref[...], st_ref[...]
    k_rope = p[:, off:off + HEAD_PAD] * ct + p[:, off + HEAD_PAD:off + 2 * HEAD_PAD] * st
    qscale = MLA_SCALE * LOG2_E
    for c0 in range(0, MLA_HEADS * HEAD_PAD, MLA_CHUNK):
        cols = slice(c0, c0 + MLA_CHUNK)
        qa = _dot(cq, wq_ref[:, cols])
        qb = _dot(cq, wqs_ref[:, cols])
        kn = _dot(ckv, wk_ref[:, cols])
        v_ref[:, cols] = (_dot(ckv, wv_ref[:, cols]) + vone_ref[:, cols]).astype(v_ref.dtype)
        for h0 in range(0, MLA_CHUNK, HEAD_PAD):
            sl = slice(h0, h0 + HEAD_PAD)
            out = slice(c0 + h0, c0 + h0 + HEAD_PAD)
            q_ref[:, out] = ((qa[:, sl] * ct + qb[:, sl] * st) * qscale).astype(q_ref.dtype)
            k_ref[:, out] = (kn[:, sl] + k_rope).astype(k_ref.dtype)
    finish()


def _pre_mla(rt, x, moe, mods, g1, wts, ct, st):
    d = x.shape[1]
    w_in, qg, kvg, wq, wqs, wk, wv, vone = wts

    def full(a):
        return pl.BlockSpec(a.shape, lambda i, de: (0,) * a.ndim)

    hq = MLA_HEADS * HEAD_PAD
    mspecs, margs, scratch = _moe_operands(rt, d, moe, lambda t: t)
    grid_spec = pltpu.PrefetchScalarGridSpec(
        num_scalar_prefetch=1, grid=(rt.n_tiles,),
        in_specs=[pl.BlockSpec((ROW_TILE, d), lambda i, de: (i, 0))] + mspecs + [
            pl.BlockSpec((1, 8, d), lambda i, de: (rt.mod_idx(i), 0, 0)),
            pl.BlockSpec((1, d), lambda i, de: (0, 0)),
            full(w_in), full(qg), full(kvg), full(wq), full(wqs), full(wk), full(wv), full(vone),
            pl.BlockSpec((ROW_TILE, HEAD_PAD), lambda i, de: (rt.pos_idx(i), 0)),
            pl.BlockSpec((ROW_TILE, HEAD_PAD), lambda i, de: (rt.pos_idx(i), 0))],
        out_specs=[pl.BlockSpec((ROW_TILE, d), lambda i, de: (i, 0)),
                   pl.BlockSpec((ROW_TILE, hq), lambda i, de: (i, 0)),
                   pl.BlockSpec((ROW_TILE, hq), lambda i, de: (i, 0)),
                   pl.BlockSpec((ROW_TILE, hq), lambda i, de: (i, 0))],
        scratch_shapes=scratch)
    return pl.pallas_call(
        _pre_mla_kernel,
        out_shape=[jax.ShapeDtypeStruct((rt.rows, d), F32), jax.ShapeDtypeStruct((rt.rows, hq), BF16),
                   jax.ShapeDtypeStruct((rt.rows, hq), BF16), jax.ShapeDtypeStruct((rt.rows, hq), BF16)],
        grid_spec=grid_spec,
        compiler_params=_cparams("arbitrary"), name="pre_mla",
    )(moe[3], x, *margs, mods, g1.reshape(1, d), w_in, qg, kvg, wq, wqs, wk, wv, vone, ct, st)


def _mla_params(w_in, q_g, kv_g, w_uq, w_ukv):
    d = w_in.shape[0]
    hp, hr = HEAD_PAD, MLA_ROPE // 2
    nq = MLA_NOPE + MLA_ROPE
    kr = w_in[:, MLA_Q_RANK + MLA_KV_RANK:]
    z = jnp.zeros((d, MLA_NOPE), F32)
    zt = jnp.zeros((d, hp - nq), F32)
    kr_a = jnp.concatenate([z, kr, zt], axis=1)
    kr_b = jnp.concatenate([z, -kr[:, hr:], kr[:, :hr], zt], axis=1)
    w_in_p = jnp.concatenate([w_in[:, :MLA_Q_RANK + MLA_KV_RANK], kr_a, kr_b], axis=1).astype(BF16)
    wq = w_uq.reshape(MLA_Q_RANK, MLA_HEADS, nq)
    zq = jnp.zeros((MLA_Q_RANK, MLA_HEADS, hp - nq), F32)
    wq_a = jnp.concatenate([wq, zq], axis=2).reshape(MLA_Q_RANK, MLA_HEADS * hp).astype(BF16)
    wq_b = jnp.concatenate([jnp.zeros_like(wq[:, :, :MLA_NOPE]), -wq[:, :, MLA_NOPE + hr:],
                            wq[:, :, MLA_NOPE:MLA_NOPE + hr], zq], axis=2)
    wq_b = wq_b.reshape(MLA_Q_RANK, MLA_HEADS * hp).astype(BF16)
    wkv = w_ukv.reshape(MLA_KV_RANK, MLA_HEADS, MLA_NOPE + MLA_V)
    wk = jnp.concatenate([wkv[:, :, :MLA_NOPE], jnp.zeros((MLA_KV_RANK, MLA_HEADS, hp - MLA_NOPE), F32)], axis=2)
    wk = wk.reshape(MLA_KV_RANK, MLA_HEADS * hp).astype(BF16)
    wv = wkv[:, :, MLA_NOPE:].reshape(MLA_KV_RANK, MLA_HEADS // 2, 2, MLA_V)
    zv = jnp.zeros((MLA_KV_RANK, MLA_HEADS // 2, hp - MLA_V), F32)
    wv = jnp.concatenate([wv[:, :, 0], zv, zv, wv[:, :, 1]], axis=2).reshape(MLA_KV_RANK, MLA_HEADS * hp)
    lane = jnp.arange(2 * hp) % (2 * hp)
    vone = jnp.tile(jnp.where((lane == ATT_DEN_EVEN) | (lane == hp + ATT_DEN_ODD), 1.0, 0.0), MLA_HEADS // 2)
    return (w_in_p, q_g.reshape(1, -1), kv_g.reshape(1, -1), wq_a, wq_b, wk, wv.astype(BF16),
            vone.reshape(1, -1).astype(F32))


def _attn_kernel(q_ref, k_ref, v_ref, o_ref, *, s_len, l_len):
    t = pl.program_id(2)
    lane = lax.broadcasted_iota(jnp.int32, (ATT_TQ, 2 * MLA_V), 1)

    def attend(nk):
        for pair in range(ATT_HEADS // 2):
            outs = []
            for j, den_lane in ((2 * pair, ATT_DEN_EVEN), (2 * pair + 1, ATT_DEN_ODD)):
                blk = slice(j * HEAD_PAD, (j + 1) * HEAD_PAD)
                s = _dot_nt(q_ref[0, :, blk], k_ref[0, 0:nk, blk])
                p = jnp.exp2(s - jnp.max(s, axis=1, keepdims=True))
                o = _dot(p.astype(BF16), v_ref[0, 0:nk, blk])
                outs.append(o / o[:, den_lane:den_lane + 1])
            o_ref[0, :, pair * 2 * MLA_V:(pair + 1) * 2 * MLA_V] = (
                jnp.where(lane < MLA_V, outs[0], outs[1]).astype(o_ref.dtype))

    ctx_tiles = l_len // ATT_TQ

    @pl.when(t < ctx_tiles)
    def _():
        attend(l_len)

    @pl.when(t >= ctx_tiles)
    def _():
        attend(s_len)


def _attention(q3, k3, v3, l_len):
    b, s, _ = q3.shape
    hq = ATT_HEADS * HEAD_PAD
    hv = ATT_HEADS * MLA_V
    return pl.pallas_call(
        functools.partial(_attn_kernel, s_len=s, l_len=l_len),
        out_shape=jax.ShapeDtypeStruct((b, s, MLA_HEADS * MLA_V), BF16),
        grid=(b, MLA_HEADS // ATT_HEADS, s // ATT_TQ),
        in_specs=[pl.BlockSpec((1, ATT_TQ, hq), lambda bi, hi, ti: (bi, ti, hi)),
                  pl.BlockSpec((1, s, hq), lambda bi, hi, ti: (bi, 0, hi)),
                  pl.BlockSpec((1, s, hq), lambda bi, hi, ti: (bi, 0, hi))],
        out_specs=pl.BlockSpec((1, ATT_TQ, hv), lambda bi, hi, ti: (bi, ti, hi)),
        compiler_params=_cparams("arbitrary", "arbitrary", "arbitrary"), name="attention",
    )(q3, k3, v3)


def _ret_tables(n, l):
    t = jnp.arange(n, dtype=F32)
    inv = ROPE_BASE ** (-jnp.arange(0, RET_DK, 2, dtype=F32) / RET_DK)
    ang = t[:, None] * inv[None, :]
    cos, sin = jnp.cos(ang), jnp.sin(ang)
    cos2 = jnp.concatenate([jnp.ones((l, RET_DK), F32), jnp.concatenate([cos, cos], axis=1)], axis=0)
    sin2 = jnp.concatenate([jnp.zeros((l, RET_DK), F32), jnp.concatenate([-sin, sin], axis=1)], axis=0)
    return cos2, sin2


def _mla_tables(n, l):
    rows = n // GRID_W
    r_pos = jnp.repeat(jnp.arange(rows, dtype=F32), GRID_W)
    c_pos = jnp.tile(jnp.arange(GRID_W, dtype=F32), rows)
    ax = MLA_ROPE // 2
    inv = ROPE_BASE ** (-jnp.arange(0, ax, 2, dtype=F32) / ax)
    ang = jnp.concatenate([r_pos[:, None] * inv[None, :], c_pos[:, None] * inv[None, :]], axis=-1)
    cos, sin = jnp.cos(ang), jnp.sin(ang)
    pad = HEAD_PAD - MLA_NOPE - MLA_ROPE
    ct_l = jnp.concatenate([jnp.ones((n, MLA_NOPE), F32), cos, cos, jnp.zeros((n, pad), F32)], axis=1)
    st_l = jnp.concatenate([jnp.zeros((n, MLA_NOPE), F32), sin, sin, jnp.zeros((n, pad), F32)], axis=1)
    ct_c = jnp.concatenate([jnp.ones((l, MLA_NOPE + MLA_ROPE), F32), jnp.zeros((l, pad), F32)], axis=1)
    return jnp.concatenate([ct_c, ct_l], axis=0), jnp.concatenate([jnp.zeros((l, HEAD_PAD), F32), st_l], axis=0)


def kernel(x, c, ctx, c_ctx, ada_w, ada_b, norm_g, ab_w_in, ab_w_out, ret_decay_logit, lru_conv_w, lru_conv_b, lru_gate_w, lru_gate_b, lru_lambda, mla_w_in, mla_q_norm_g, mla_kv_norm_g, mla_w_uq, mla_w_ukv, mla_w_out, moe_group_w, moe_group_b, moe_expert_w, moe_expert_b, moe_w_gate, moe_w_up, moe_w_down, final_norm_g):
    b, n, d = x.shape
    l = ctx.shape[1]
    s = l + n
    depth = ada_w.shape[0]
    rt = _Rows(b, s, l)

    nrow = (b + 1 + 7) // 8 * 8
    cvec = jnp.concatenate([c, c_ctx[None, :], jnp.zeros((nrow - b - 1, d), F32)], axis=0)
    ada = _ada_all(cvec, ada_w, ada_b)

    def layer_mods(layer):
        lat = ada[layer, :b].reshape(b, 1, 6, d)
        cx = jnp.broadcast_to(ada[layer, b].reshape(1, 1, 6, d), (b, 1, 6, d))
        m = jnp.concatenate([cx, lat], axis=1)
        m = jnp.concatenate([m, jnp.zeros((b, 2, 2, d), F32)], axis=2)
        return m.reshape(2 * b, 8, d)

    cos2, sin2 = _ret_tables(n, l)
    ct, st = _mla_tables(n, l)

    xs = jnp.concatenate([ctx, x], axis=1).reshape(b * s, d)
    out = None
    moe = None
    for layer in range(depth):
        mods = layer_mods(layer)
        i = layer // 2
        if layer % 2 == 0:
            xs, p = _pre_ab(rt, xs, moe, mods, norm_g[layer, 0], ab_w_in[i].astype(BF16))
            p3 = p.reshape(b, s, -1)
            lg = jax.nn.log_sigmoid(ret_decay_logit[i].astype(F32))
            lgv = jnp.broadcast_to(lg.T[:, :, None], (RET_HEADS, 2, LANES))
            lgv = jnp.concatenate([lgv, jnp.zeros((RET_HEADS, 6, LANES), F32)], axis=1)
            ma = _retention(p3, cos2, sin2, lgv, l).reshape(b * s, -1)
            mb = _rglru(p3, *_lru_params(lru_conv_w[i], lru_conv_b[i], lru_gate_w[i], lru_gate_b[i],
                                         lru_lambda[i]), l).reshape(b * s, -1)
            cb = 0
            w_out = ab_w_out[i].astype(BF16)
        else:
            wts = _mla_params(mla_w_in[i], mla_q_norm_g[i], mla_kv_norm_g[i], mla_w_uq[i], mla_w_ukv[i])
            xs, q, k, v = _pre_mla(rt, xs, moe, mods, norm_g[layer, 0], wts, ct, st)
            att = _attention(q.reshape(b, s, -1), k.reshape(b, s, -1), v.reshape(b, s, -1), l)
            ma = mb = att.reshape(b * s, -1)
            cb = 1
            w_out = mla_w_out[i].astype(BF16)
        wr = jnp.concatenate([moe_group_w[layer], moe_expert_w[layer],
                              jnp.zeros((d, LANES - MOE_GROUPS - MOE_EXPERTS), F32)], axis=1)
        wr_hi = wr.astype(BF16)
        wr = jnp.concatenate([wr_hi, (wr - wr_hi.astype(F32)).astype(BF16)], axis=1)
        br = jnp.concatenate([moe_group_b[layer], moe_expert_b[layer],
                              jnp.zeros((LANES - MOE_GROUPS - MOE_EXPERTS,), F32)]).reshape(1, LANES)
        xs, h2, route, cnt = _post(rt, xs, ma, mb, cb, w_out, mods, norm_g[layer, 1], wr, br)
        dest, block_e, n_valid, fill, nb = _moe_plan(route, cnt)
        xsort = _dispatch(h2, dest, fill, nb * MOE_ROWS)
        ys = _experts(xsort, block_e, n_valid, layer, moe_w_gate, moe_w_up, moe_w_down)
        moe = (route, mods, ys, dest)
    out = _final(rt, xs, moe, final_norm_g, n)
    return out.reshape(b, n, d)
```

```python
import functools

import jax
import jax.numpy as jnp
import numpy as np
from jax import lax
from jax.experimental import pallas as pl
from jax.experimental.pallas import tpu as pltpu

F32 = jnp.float32
BF16 = jnp.bfloat16

EPS = 1e-6
ROPE_BASE = 10000.0
GRID_W = 64

RET_HEADS = 4
RET_DK = 128
RET_CHUNK = 128
RET_HPS = 2
LRU_WIDTH = 512
LRU_BLOCK = 64
LRU_C = 8.0
LRU_HALF = 256
LRU_TILE = 128
LRU_SUB = 8

MLA_HEADS = 16
MLA_NOPE = 64
MLA_ROPE = 32
MLA_V = 64
MLA_Q_RANK = 384
MLA_KV_RANK = 256
MLA_SCALE = (MLA_NOPE + MLA_ROPE) ** -0.5
LOG2_E = 1.4426950408889634
HEAD_PAD = 128
MLA_CHUNK = 512

MOE_GROUPS = 4
MOE_PER_GROUP = 8
MOE_EXPERTS = 32
MOE_ROWS = 256

ROW_TILE = 256
ATT_TQ = 256
ATT_HEADS = 4
ATT_DEN_EVEN = 64
ATT_DEN_ODD = 0
LANES = 128
VMEM_LIMIT = 56 * 1024 * 1024


def _cparams(*sem):
    return pltpu.CompilerParams(dimension_semantics=sem, vmem_limit_bytes=VMEM_LIMIT)


def _rms(x, g):
    return x * lax.rsqrt(jnp.mean(x * x, axis=-1, keepdims=True) + EPS) * g


def _dot(a, b):
    return jnp.dot(a, b, preferred_element_type=F32)


def _dot_nt(a, b):
    return lax.dot_general(a, b, (((1,), (1,)), ((), ())), preferred_element_type=F32)


def _dot_tn(a, b):
    return lax.dot_general(a, b, (((0,), (0,)), ((), ())), preferred_element_type=F32)


def _ada_kernel(s_ref, w_ref, b_ref, o_ref):
    s = jax.nn.silu(s_ref[...])
    o_ref[0] = _dot(s.astype(BF16), w_ref[0].astype(BF16)) + b_ref[0]


def _ada_all(cvec, ada_w, ada_b):
    depth, d, n6 = ada_w.shape
    rows = cvec.shape[0]
    tn = n6 // 4
    return pl.pallas_call(
        _ada_kernel,
        out_shape=jax.ShapeDtypeStruct((depth, rows, n6), F32),
        grid=(depth, n6 // tn),
        in_specs=[pl.BlockSpec((rows, d), lambda l, j: (0, 0)),
                  pl.BlockSpec((1, d, tn), lambda l, j: (l, 0, j)),
                  pl.BlockSpec((1, 1, tn), lambda l, j: (l, 0, j))],
        out_specs=pl.BlockSpec((1, rows, tn), lambda l, j: (l, 0, j)),
        compiler_params=_cparams("arbitrary", "arbitrary"),
        name="adaln",
    )(cvec, ada_w, ada_b.reshape(depth, 1, n6))


class _Rows:
    def __init__(self, b, s, l):
        assert s % ROW_TILE == 0 and l % ROW_TILE == 0
        self.b, self.s, self.l = b, s, l
        self.tpb = s // ROW_TILE
        self.ctx_tiles = l // ROW_TILE
        self.n_tiles = b * self.tpb
        self.rows = b * s

    def mod_idx(self, i):
        return 2 * (i // self.tpb) + jnp.where(i % self.tpb >= self.ctx_tiles, 1, 0)

    def pos_idx(self, i):
        return i % self.tpb


def _modulated(x, g_ref, mod_ref, base):
    h = _rms(x, g_ref[...])
    return h * (1.0 + mod_ref[0, base + 1:base + 2, :]) + mod_ref[0, base:base + 1, :]


def _moe_update(dest_ref, x_ref, r_ref, modp_ref, ys_hbm, ybuf, sem, tile_of, inline):
    i = pl.program_id(0)
    n = pl.num_programs(0)
    slot = i % 2

    def start(tok, j, sl):
        for k in range(2):
            pltpu.make_async_copy(ys_hbm.at[pl.ds(dest_ref[2 * tok + k], 1)], ybuf.at[sl, k, pl.ds(j, 1)],
                                  sem.at[sl]).start()

    def issue_loop(step, sl):
        base = tile_of(step) * ROW_TILE
        lax.fori_loop(0, ROW_TILE, lambda j, c: (start(base + j, j, sl), c)[1], 0, unroll=8)

    def wait(sl):
        for k in range(2):
            pltpu.make_async_copy(ys_hbm.at[pl.ds(0, ROW_TILE)], ybuf.at[sl, k], sem.at[sl]).wait()

    @pl.when(i == 0)
    def _():
        issue_loop(0, 0)

    if not inline:
        @pl.when(i + 1 < n)
        def _():
            issue_loop(i + 1, 1 - slot)

    wait(slot)
    r = r_ref[...]
    y = ybuf[slot, 0] * r[:, 2:3] + ybuf[slot, 1] * r[:, 3:4]
    x = x_ref[...] + modp_ref[0, 5:6, :] * y
    if not inline:
        return x, None
    base = tile_of(jnp.minimum(i + 1, n - 1)) * ROW_TILE
    for j in range(ROW_TILE):
        start(base + j, j, 1 - slot)

    def finish():
        @pl.when(i == n - 1)
        def _():
            wait(1 - slot)
    return x, finish


def _moe_operands(rt, d, moe, tile_of):
    route, modp, ys, _ = moe
    specs = [pl.BlockSpec((ROW_TILE, LANES), lambda i, de: (tile_of(i), 0)),
             pl.BlockSpec((1, 8, d), lambda i, de: (rt.mod_idx(tile_of(i)), 0, 0)),
             pl.BlockSpec(memory_space=pl.ANY)]
    scratch = [pltpu.VMEM((2, 2, ROW_TILE, d), F32), pltpu.SemaphoreType.DMA((2,))]
    return specs, [route, modp, ys], scratch


def _pre_ab_kernel(*refs, has_moe):
    if has_moe:
        dest_ref, x_ref, r_ref, modp_ref, ys_hbm, mod_ref, g_ref, w_ref, xo_ref, p_ref, ybuf, sem = refs
        x, finish = _moe_update(dest_ref, x_ref, r_ref, modp_ref, ys_hbm, ybuf, sem, lambda t: t, True)
        xo_ref[...] = x
    else:
        x_ref, mod_ref, g_ref, w_ref, p_ref = refs
        x = x_ref[...]
    h = _modulated(x, g_ref, mod_ref, 0)
    p_ref[...] = _dot(h.astype(BF16), w_ref[...])
    if has_moe:
        finish()


def _pre_ab(rt, x, moe, mods, g1, w_in):
    d = x.shape[1]
    n_out = w_in.shape[1]
    if moe is None:
        p = pl.pallas_call(
            functools.partial(_pre_ab_kernel, has_moe=False),
            out_shape=jax.ShapeDtypeStruct((rt.rows, n_out), F32),
            grid=(rt.n_tiles,),
            in_specs=[pl.BlockSpec((ROW_TILE, d), lambda i: (i, 0)),
                      pl.BlockSpec((1, 8, d), lambda i: (rt.mod_idx(i), 0, 0)),
                      pl.BlockSpec((1, d), lambda i: (0, 0)),
                      pl.BlockSpec((d, n_out), lambda i: (0, 0))],
            out_specs=pl.BlockSpec((ROW_TILE, n_out), lambda i: (i, 0)),
            compiler_params=_cparams("arbitrary"), name="pre_ab",
        )(x, mods, g1.reshape(1, d), w_in)
        return x, p
    mspecs, margs, scratch = _moe_operands(rt, d, moe, lambda t: t)
    grid_spec = pltpu.PrefetchScalarGridSpec(
        num_scalar_prefetch=1, grid=(rt.n_tiles,),
        in_specs=[pl.BlockSpec((ROW_TILE, d), lambda i, de: (i, 0))] + mspecs + [
            pl.BlockSpec((1, 8, d), lambda i, de: (rt.mod_idx(i), 0, 0)),
            pl.BlockSpec((1, d), lambda i, de: (0, 0)),
            pl.BlockSpec((d, n_out), lambda i, de: (0, 0))],
        out_specs=[pl.BlockSpec((ROW_TILE, d), lambda i, de: (i, 0)),
                   pl.BlockSpec((ROW_TILE, n_out), lambda i, de: (i, 0))],
        scratch_shapes=scratch)
    return pl.pallas_call(
        functools.partial(_pre_ab_kernel, has_moe=True),
        out_shape=[jax.ShapeDtypeStruct((rt.rows, d), F32), jax.ShapeDtypeStruct((rt.rows, n_out), F32)],
        grid_spec=grid_spec,
        compiler_params=_cparams("arbitrary"), name="pre_ab_moe",
    )(moe[3], x, *margs, mods, g1.reshape(1, d), w_in)


def _ret_kernel(q_ref, k_ref, v_ref, g_ref, cos_ref, sin_ref, lg_ref, o_ref, qs, ks, acc, *, s_len, l_len):
    c, dk = RET_CHUNK, RET_DK
    nch, cch = s_len // c, l_len // c
    ii = lax.broadcasted_iota(jnp.int32, (c, c), 0).astype(F32)
    jj = lax.broadcasted_iota(jnp.int32, (c, c), 1).astype(F32)
    diff = ii - jj
    k_scale = RET_DK ** -0.5

    def head_consts(j):
        lgf = lg_ref[j, 0:1, :]
        lgb = lg_ref[j, 1:2, :]
        dmask = (jnp.where(diff > 0, jnp.exp(lgf * jnp.maximum(diff, 0.0)), 0.0)
                 + jnp.where(diff < 0, jnp.exp(lgb * jnp.maximum(-diff, 0.0)), 0.0)
                 + jnp.where(diff == 0, 2.0, 0.0))
        return dict(dmask=dmask,
                    zeta_f=jnp.exp(lgf * (c - 1.0 - ii)), xi_f=jnp.exp(lgf * (ii + 1.0)),
                    zeta_b=jnp.exp(lgb * ii), xi_b=jnp.exp(lgb * (c - ii)),
                    cd_f=jnp.exp(lgf * c), cd_b=jnp.exp(lgb * c))

    hc = [head_consts(j) for j in range(RET_HPS)]

    def fwd(n, sts):
        rows = pl.ds(pl.multiple_of(n * c, c), c)
        cs, sn = cos_ref[rows, :], sin_ref[rows, :]
        out = []
        for j, st in enumerate(sts):
            cols = slice(j * dk, (j + 1) * dk)
            q = q_ref[0, rows, cols]
            k = k_ref[0, rows, cols]
            v = v_ref[0, rows, cols]
            qb = (q * cs + pltpu.roll(q, 64, 1) * sn).astype(BF16)
            kb = ((k * cs + pltpu.roll(k, 64, 1) * sn) * k_scale).astype(BF16)
            qs[rows, cols] = qb
            ks[rows, cols] = kb
            sc = _dot_nt(qb, kb) * hc[j]["dmask"]
            acc[rows, cols] = _dot(sc.astype(BF16), v.astype(BF16)) + _dot(qb, st.astype(BF16)) * hc[j]["xi_f"]
            out.append(hc[j]["cd_f"] * st + _dot_tn(kb, (v * hc[j]["zeta_f"]).astype(BF16)))
        return tuple(out)

    def bwd(n, sts):
        rows = pl.ds(pl.multiple_of(n * c, c), c)
        out = []
        for j, st in enumerate(sts):
            cols = slice(j * dk, (j + 1) * dk)
            qb = qs[rows, cols]
            kb = ks[rows, cols]
            v = v_ref[0, rows, cols]
            y = acc[rows, cols] + _dot(qb, st.astype(BF16)) * hc[j]["xi_b"]
            y = y * lax.rsqrt(jnp.mean(y * y, axis=-1, keepdims=True) + EPS)
            o_ref[0, rows, cols] = (y * jax.nn.silu(g_ref[0, rows, cols])).astype(o_ref.dtype)
            out.append(hc[j]["cd_b"] * st + _dot_tn(kb, (v * hc[j]["zeta_b"]).astype(BF16)))
        return tuple(out)

    zero = tuple(jnp.zeros((c, c), F32) for _ in range(RET_HPS))
    assert nch % 2 == 0 and cch % 2 == 0
    lax.fori_loop(0, nch, fwd, zero, unroll=2)
    sts = lax.fori_loop(0, cch, lambda t, sts: bwd(cch - 1 - t, sts), zero, unroll=2)
    lax.fori_loop(0, nch - cch, lambda t, sts: bwd(nch - 1 - t, sts), sts, unroll=2)


def _retention(p3, cos2, sin2, lgv, l_len):
    b, s, _ = p3.shape
    groups = RET_HEADS // RET_HPS
    w = RET_HPS * RET_DK

    def col(off, **kw):
        return pl.BlockSpec((1, s, w), lambda bi, hi: (bi, 0, off + hi), **kw)

    once = dict(pipeline_mode=pl.Buffered(1))
    return pl.pallas_call(
        functools.partial(_ret_kernel, s_len=s, l_len=l_len),
        out_shape=jax.ShapeDtypeStruct((b, s, RET_HEADS * RET_DK), BF16),
        grid=(b, groups),
        in_specs=[col(0), col(groups), col(2 * groups), col(3 * groups, **once),
                  pl.BlockSpec((s, RET_DK), lambda bi, hi: (0, 0), **once),
                  pl.BlockSpec((s, RET_DK), lambda bi, hi: (0, 0), **once),
                  pl.BlockSpec((RET_HPS, 8, LANES), lambda bi, hi: (hi, 0, 0))],
        out_specs=pl.BlockSpec((1, s, w), lambda bi, hi: (bi, 0, hi)),
        scratch_shapes=[pltpu.VMEM((s, w), BF16), pltpu.VMEM((s, w), BF16), pltpu.VMEM((s, w), F32)],
        compiler_params=_cparams("arbitrary", "arbitrary"), name="retention",
    )(p3, p3, p3, p3, cos2, sin2, lgv)


def _tile_scan(a, b, reverse):
    n = a.shape[0]
    rows = lax.broadcasted_iota(jnp.int32, a.shape, 0)
    step = 1
    while step < n:
        shift = n - step if reverse else step
        a_s = pltpu.roll(a, shift, 0)
        b_s = pltpu.roll(b, shift, 0)
        m = (rows < n - step) if reverse else (rows >= step)
        b = jnp.where(m, a * b_s + b, b)
        a = jnp.where(m, a * a_s, a)
        step *= 2
    return a, b


def _scan_rows(a, b, carry, reverse):
    n = a.shape[0]
    pieces = [None] * (n // LRU_SUB)
    for i in (reversed(range(len(pieces))) if reverse else range(len(pieces))):
        rows = slice(i * LRU_SUB, (i + 1) * LRU_SUB)
        a_c, h_loc = _tile_scan(a[rows], b[rows], reverse)
        pieces[i] = h_loc + a_c * carry
        carry = pieces[i][0:1] if reverse else pieces[i][LRU_SUB - 1:LRU_SUB]
    return jnp.concatenate(pieces, axis=0), carry


def _lru_kernel(x_ref, y_ref, cw_ref, wg_ref, gb_ref, sp_ref, o_ref, xpad, hf, ab, bb, *, s_len, l_len):
    tl, w = LRU_TILE, LRU_HALF
    ntl, ctl = s_len // tl, l_len // tl
    xpad[0:8, :] = jnp.zeros((8, w), F32)
    xpad[s_len + 8:s_len + 16, :] = jnp.zeros((8, w), F32)
    xpad[8:s_len + 8, :] = x_ref[0]
    w0, w1, w2, w3, cb = (cw_ref[0, t:t + 1, :] for t in range(5))
    sp_f = sp_ref[0, 0:1, :]
    sp_b = sp_ref[0, 1:2, :]
    it = lax.broadcasted_iota(jnp.int32, (tl, w), 0)

    def coeff(gr, gi, sp, xc):
        r = jax.nn.sigmoid(gr)
        i = jax.nn.sigmoid(gi)
        log_a = -LRU_C * r * sp
        th = jnp.tanh(log_a)
        return jnp.exp(log_a), jnp.sqrt(-2.0 * th / (1.0 - th)) * (i * xc)

    def fwd(n, carry):
        r0 = pl.multiple_of(n * tl, tl)
        win = xpad[pl.ds(r0, tl + 16), :]
        t = r0 + it
        seg = jnp.where(t >= l_len, 1, 0)

        def tap(d):
            v = pltpu.roll(win, (tl + 16 - d) % (tl + 16), 0)[8:8 + tl]
            return jnp.where(jnp.where(t + d >= l_len, 1, 0) == seg, v, 0.0)

        xc = tap(-2) * w0
        xc = xc + tap(-1) * w1
        xc = xc + win[8:8 + tl] * w2
        xc = xc + tap(1) * w3
        xc = xc + cb
        gts = _dot(xc.astype(BF16), wg_ref[0]) + gb_ref[0]
        a_f, b_f = coeff(gts[:, 0:w], gts[:, w:2 * w], sp_f, xc)
        a_b, b_b = coeff(gts[:, 2 * w:3 * w], gts[:, 3 * w:4 * w], sp_b, xc)
        rows = pl.ds(r0, tl)
        ab[rows, :] = a_b
        bb[rows, :] = b_b
        hh, carry = _scan_rows(a_f, b_f, carry, False)
        hf[rows, :] = hh
        return carry

    def bwd(n, carry):
        rows = pl.ds(pl.multiple_of(n * tl, tl), tl)
        hh, carry = _scan_rows(ab[rows, :], bb[rows, :], carry, True)
        o_ref[0, rows, :] = ((hf[rows, :] + hh) * jax.nn.gelu(y_ref[0, rows, :])).astype(o_ref.dtype)
        return carry

    zero = jnp.zeros((1, w), F32)
    lax.fori_loop(0, ntl, fwd, zero)
    c = lax.fori_loop(0, ctl, lambda t, c: bwd(ctl - 1 - t, c), zero)
    lax.fori_loop(0, ntl - ctl, lambda t, c: bwd(ntl - 1 - t, c), c)


def _rglru(p3, conv_wb, gate_w, gate_b, sp, l_len):
    b, s, _ = p3.shape
    nh = LRU_WIDTH // LRU_HALF
    xoff = (4 * RET_HEADS * RET_DK) // LRU_HALF
    yoff = xoff + nh
    return pl.pallas_call(
        functools.partial(_lru_kernel, s_len=s, l_len=l_len),
        out_shape=jax.ShapeDtypeStruct((b, s, LRU_WIDTH), BF16),
        grid=(b, nh),
        in_specs=[pl.BlockSpec((1, s, LRU_HALF), lambda bi, j: (bi, 0, xoff + j)),
                  pl.BlockSpec((1, s, LRU_HALF), lambda bi, j: (bi, 0, yoff + j)),
                  pl.BlockSpec((1, 8, LRU_HALF), lambda bi, j: (j, 0, 0)),
                  pl.BlockSpec((1, LRU_HALF, 4 * LRU_HALF), lambda bi, j: (j, 0, 0)),
                  pl.BlockSpec((1, 1, 4 * LRU_HALF), lambda bi, j: (j, 0, 0)),
                  pl.BlockSpec((1, 8, LRU_HALF), lambda bi, j: (j, 0, 0))],
        out_specs=pl.BlockSpec((1, s, LRU_HALF), lambda bi, j: (bi, 0, j)),
        scratch_shapes=[pltpu.VMEM((s + 16, LRU_HALF), F32), pltpu.VMEM((s, LRU_HALF), F32),
                        pltpu.VMEM((s, LRU_HALF), F32), pltpu.VMEM((s, LRU_HALF), F32)],
        compiler_params=_cparams("arbitrary", "arbitrary"), name="rglru",
    )(p3, p3, conv_wb, gate_w, gate_b, sp)


def _lru_params(conv_w, conv_b, gate_w, gate_b, lam):
    nh = LRU_WIDTH // LRU_HALF
    bph = LRU_HALF // LRU_BLOCK
    cw = jnp.concatenate([conv_w, conv_b[None, :], jnp.zeros((3, LRU_WIDTH), F32)], axis=0)
    cw = cw.reshape(8, nh, LRU_HALF).transpose(1, 0, 2)
    eye = jnp.eye(bph, dtype=F32)
    gw = gate_w.reshape(2, 2, nh, bph, LRU_BLOCK, LRU_BLOCK)
    dense = jnp.einsum('dgjkio,kl->jkidglo', gw, eye)
    dense = dense.reshape(nh, LRU_HALF, 4 * LRU_HALF).astype(BF16)
    gb = gate_b.reshape(2, 2, nh, LRU_HALF).transpose(2, 0, 1, 3).reshape(nh, 1, 4 * LRU_HALF)
    sp = jax.nn.softplus(-lam.astype(F32)).reshape(2, nh, LRU_HALF).transpose(1, 0, 2)
    sp = jnp.concatenate([sp, jnp.zeros((nh, 6, LRU_HALF), F32)], axis=1)
    return cw, dense, gb, sp


def _post_kernel(x_ref, ma_ref, mb_ref, w_ref, mod_ref, g_ref, wr_ref, br_ref, xo_ref, h_ref, r_ref, cnt_ref):
    m = jnp.concatenate([ma_ref[...], mb_ref[...]], axis=1)
    o = _dot(m, w_ref[...])
    x = x_ref[...] + mod_ref[0, 2:3, :] * o
    xo_ref[...] = x
    h = _modulated(x, g_ref, mod_ref, 3)
    h_ref[...] = h
    h_hi = h.astype(BF16)
    h_lo = (h - h_hi.astype(F32)).astype(BF16)
    part = _dot(h_hi, wr_ref[...])
    lg = part[:, 0:LANES] + part[:, LANES:2 * LANES] + _dot(h_lo, wr_ref[:, 0:LANES]) + br_ref[...]
    lane = lax.broadcasted_iota(jnp.int32, lg.shape, 1)
    lanef = lane.astype(F32)
    ninf = -jnp.inf
    big = float(LANES)
    gl = jnp.where(lane < MOE_GROUPS, lg, ninf)
    gmax = jnp.max(gl, axis=1, keepdims=True)
    g_top = 1.0 / jnp.sum(jnp.exp(gl - gmax), axis=1, keepdims=True)
    g_sel = jnp.min(jnp.where(gl == gmax, lanef, big), axis=1, keepdims=True)
    lo = MOE_GROUPS + MOE_PER_GROUP * g_sel
    el = jnp.where((lanef >= lo) & (lanef < lo + MOE_PER_GROUP), lg, ninf)
    emax = jnp.max(el, axis=1, keepdims=True)
    esum = jnp.sum(jnp.exp(el - emax), axis=1, keepdims=True)
    i1 = jnp.min(jnp.where(el == emax, lanef, big), axis=1, keepdims=True)
    el2 = jnp.where(lanef == i1, ninf, el)
    m2 = jnp.max(el2, axis=1, keepdims=True)
    i2 = jnp.min(jnp.where(el2 == m2, lanef, big), axis=1, keepdims=True)
    p1 = 1.0 / esum
    p2 = jnp.exp(m2 - emax) / esum
    tot = p1 + p2
    w1 = g_top * (p1 / tot)
    w2 = g_top * (p2 / tot)
    hit1 = lanef == i1
    hit2 = lanef == i2
    onehot = jnp.where(hit1, 1.0, 0.0) + jnp.where(hit2, 1.0, 0.0)
    ti = lax.broadcasted_iota(jnp.int32, (ROW_TILE, ROW_TILE), 0)
    tj = lax.broadcasted_iota(jnp.int32, (ROW_TILE, ROW_TILE), 1)
    earlier = jnp.where(tj < ti, 1.0, 0.0).astype(BF16)

    @pl.when(pl.program_id(0) == 0)
    def _():
        cnt_ref[...] = jnp.zeros(cnt_ref.shape, F32)

    before = _dot(earlier, onehot.astype(BF16)) + cnt_ref[0:1, :]
    k1 = jnp.sum(jnp.where(hit1, before, 0.0), axis=1, keepdims=True)
    k2 = jnp.sum(jnp.where(hit2, before, 0.0), axis=1, keepdims=True)
    cnt_ref[0:1, :] = cnt_ref[0:1, :] + jnp.sum(onehot, axis=0, keepdims=True)
    vals = (i1 - MOE_GROUPS, i2 - MOE_GROUPS, w1, w2, k1, k2)
    slab = jnp.zeros(lg.shape, F32)
    for col, v in enumerate(vals):
        slab = jnp.where(lane == col, v, slab)
    r_ref[...] = slab


def _post(rt, x, ma, mb, cb, w_out, mods, g2, wr, br):
    d = x.shape[1]
    hd = d // 2
    return pl.pallas_call(
        _post_kernel,
        out_shape=[jax.ShapeDtypeStruct((rt.rows, d), F32), jax.ShapeDtypeStruct((rt.rows, d), F32),
                   jax.ShapeDtypeStruct((rt.rows, LANES), F32), jax.ShapeDtypeStruct((8, LANES), F32)],
        grid=(rt.n_tiles,),
        in_specs=[pl.BlockSpec((ROW_TILE, d), lambda i: (i, 0)),
                  pl.BlockSpec((ROW_TILE, hd), lambda i: (i, 0)),
                  pl.BlockSpec((ROW_TILE, hd), lambda i: (i, cb)),
                  pl.BlockSpec((d, d), lambda i: (0, 0)),
                  pl.BlockSpec((1, 8, d), lambda i: (rt.mod_idx(i), 0, 0)),
                  pl.BlockSpec((1, d), lambda i: (0, 0)),
                  pl.BlockSpec((d, 2 * LANES), lambda i: (0, 0)),
                  pl.BlockSpec((1, LANES), lambda i: (0, 0))],
        out_specs=[pl.BlockSpec((ROW_TILE, d), lambda i: (i, 0)),
                   pl.BlockSpec((ROW_TILE, d), lambda i: (i, 0)),
                   pl.BlockSpec((ROW_TILE, LANES), lambda i: (i, 0)),
                   pl.BlockSpec((8, LANES), lambda i: (0, 0))],
        compiler_params=_cparams("arbitrary"), name="post",
    )(x, ma, mb, w_out, mods, g2.reshape(1, d), wr, br)


def _moe_plan(route, cnt):
    mb = MOE_ROWS
    t_count = route.shape[0]
    nb = (2 * t_count + MOE_EXPERTS * (mb - 1) + mb - 1) // mb
    counts = cnt[0, MOE_GROUPS:MOE_GROUPS + MOE_EXPERTS].astype(jnp.int32)
    padded = (counts + mb - 1) // mb * mb
    pend = jnp.cumsum(padded)
    pstart = pend - padded
    experts = jnp.arange(MOE_EXPERTS, dtype=jnp.int32)
    e = route[:, 0:2].astype(jnp.int32)
    first = jnp.sum(jnp.where(e[:, :, None] == experts[None, None, :], pstart[None, None, :], 0), axis=-1)
    dest = (first + route[:, 4:6].astype(jnp.int32)).reshape(-1)
    blk0 = jnp.arange(nb, dtype=jnp.int32) * mb
    block_e = jnp.minimum(jnp.sum((blk0[:, None] >= pend[None, :]).astype(jnp.int32), axis=1), MOE_EXPERTS - 1)
    sel = block_e[:, None] == experts[None, :]
    used = blk0 - jnp.sum(jnp.where(sel, pstart[None, :], 0), axis=1)
    n_valid = jnp.clip(jnp.sum(jnp.where(sel, counts[None, :], 0), axis=1) - used, 0, mb).astype(jnp.int32)
    fill = jnp.concatenate([pstart + counts, padded - counts, pend[-1:], nb - pend[-1:] // mb])
    return dest, block_e, n_valid, fill.astype(jnp.int32), nb


def _dispatch_kernel(dest_ref, fill_ref, h_ref, xs_hbm, stage, zbuf, sem, zsems, *, nt):
    i = pl.program_id(0)
    slot = i % 2
    zsem = zsems.at[0]

    def wait_tile(sl):
        for _ in range(2):
            pltpu.make_async_copy(stage.at[sl], xs_hbm.at[pl.ds(0, ROW_TILE)], sem.at[sl]).wait()

    def zero_padding(wait):
        def go(cp):
            cp.wait() if wait else cp.start()

        def one_row(r):
            go(pltpu.make_async_copy(zbuf.at[pl.ds(0, 1)], xs_hbm.at[pl.ds(r, 1)], zsem))

        def per_expert(e, c):
            start = fill_ref[e]
            n = fill_ref[MOE_EXPERTS + e]
            head = jnp.minimum((8 - (start & 7)) & 7, n)
            mid = pl.multiple_of(lax.shift_left(lax.shift_right_logical(n - head, 3), 3), 8)
            lax.fori_loop(0, head, lambda j, c2: (one_row(start + j), c2)[1], 0)

            @pl.when(mid > 0)
            def _():
                at = pl.multiple_of(start + head, 8)
                go(pltpu.make_async_copy(zbuf.at[pl.ds(0, mid)], xs_hbm.at[pl.ds(at, mid)], zsem))

            lax.fori_loop(0, n - head - mid, lambda j, c2: (one_row(start + head + mid + j), c2)[1], 0)
            return c
        lax.fori_loop(0, MOE_EXPERTS, per_expert, 0)

        def per_block(j, c):
            at = pl.multiple_of(fill_ref[2 * MOE_EXPERTS] + j * MOE_ROWS, MOE_ROWS)
            go(pltpu.make_async_copy(zbuf, xs_hbm.at[pl.ds(at, MOE_ROWS)], zsem))
            return c
        lax.fori_loop(0, fill_ref[2 * MOE_EXPERTS + 1], per_block, 0)

    @pl.when(i == 0)
    def _():
        zbuf[...] = jnp.zeros(zbuf.shape, zbuf.dtype)
        zero_padding(False)

    @pl.when(i >= 2)
    def _():
        wait_tile(slot)

    stage[slot] = h_ref[...]

    for j in range(ROW_TILE):
        src = stage.at[slot, pl.ds(j, 1)]
        for k in range(2):
            row = dest_ref[2 * (i * ROW_TILE + j) + k]
            pltpu.make_async_copy(src, xs_hbm.at[pl.ds(row, 1)], sem.at[slot]).start()

    @pl.when(i == nt - 1)
    def _():
        wait_tile(slot)
        if nt > 1:
            wait_tile(1 - slot)
        zero_padding(True)


def _dispatch(h, dest, fill, n_rows):
    t_count, d = h.shape
    nt = t_count // ROW_TILE
    assert MOE_ROWS <= ROW_TILE
    grid_spec = pltpu.PrefetchScalarGridSpec(
        num_scalar_prefetch=2, grid=(nt,),
        in_specs=[pl.BlockSpec((ROW_TILE, d), lambda i, de, fi: (i, 0))],
        out_specs=pl.BlockSpec(memory_space=pl.ANY),
        scratch_shapes=[pltpu.VMEM((2, ROW_TILE, d), F32), pltpu.VMEM((MOE_ROWS, d), F32),
                        pltpu.SemaphoreType.DMA((2,)), pltpu.SemaphoreType.DMA((1,))])
    return pl.pallas_call(
        functools.partial(_dispatch_kernel, nt=nt),
        out_shape=jax.ShapeDtypeStruct((n_rows, d), F32),
        grid_spec=grid_spec,
        compiler_params=_cparams("arbitrary"), name="dispatch",
    )(dest, fill, h)


def _expert_kernel(be_ref, nv_ref, x_ref, wg_ref, wu_ref, wd_ref, y_ref, wgb, wub, wdb):
    i = pl.program_id(0)

    @pl.when(nv_ref[i] > 0)
    def _():
        @pl.when((i == 0) | (be_ref[i] != be_ref[jnp.maximum(i - 1, 0)]))
        def _():
            wgb[...] = wg_ref[0, 0].astype(BF16)
            wub[...] = wu_ref[0, 0].astype(BF16)
            wdb[...] = wd_ref[0, 0].astype(BF16)

        x = x_ref[...].astype(BF16)
        a = (jax.nn.silu(_dot(x, wgb[...])) * _dot(x, wub[...])).astype(BF16)
        y_ref[...] = _dot(a, wdb[...])

    @pl.when(nv_ref[i] == 0)
    def _():
        y_ref[...] = jnp.zeros(y_ref.shape, y_ref.dtype)


def _experts(xs, block_e, n_valid, layer, wg, wu, wd):
    n_rows, d = xs.shape
    mb = MOE_ROWS
    hid = wg.shape[3]

    def wspec(shape):
        return pl.BlockSpec(shape, lambda i, be, nv: (layer, be[i], 0, 0))

    grid_spec = pltpu.PrefetchScalarGridSpec(
        num_scalar_prefetch=2, grid=(n_rows // mb,),
        in_specs=[pl.BlockSpec((mb, d), lambda i, be, nv: (i, 0)),
                  wspec((1, 1, d, hid)), wspec((1, 1, d, hid)), wspec((1, 1, hid, d))],
        out_specs=pl.BlockSpec((mb, d), lambda i, be, nv: (i, 0)),
        scratch_shapes=[pltpu.VMEM((d, hid), BF16), pltpu.VMEM((d, hid), BF16), pltpu.VMEM((hid, d), BF16)])
    return pl.pallas_call(
        _expert_kernel,
        out_shape=jax.ShapeDtypeStruct((n_rows, d), F32),
        grid_spec=grid_spec,
        compiler_params=_cparams("arbitrary"), name="experts",
    )(block_e, n_valid, xs, wg, wu, wd)


def _final_kernel(dest_ref, x_ref, r_ref, modp_ref, ys_hbm, g_ref, o_ref, ybuf, sem, *, tile_of):
    x, finish = _moe_update(dest_ref, x_ref, r_ref, modp_ref, ys_hbm, ybuf, sem, tile_of, True)
    o_ref[...] = _rms(x, g_ref[...])
    finish()


def _final(rt, x, moe, final_g, n_len):
    d = x.shape[1]
    lt = n_len // ROW_TILE

    def tile_of(i):
        return (i // lt) * rt.tpb + rt.ctx_tiles + i % lt

    mspecs, margs, scratch = _moe_operands(rt, d, moe, tile_of)
    grid_spec = pltpu.PrefetchScalarGridSpec(
        num_scalar_prefetch=1, grid=(rt.b * lt,),
        in_specs=[pl.BlockSpec((ROW_TILE, d), lambda i, de: (tile_of(i), 0))] + mspecs + [
            pl.BlockSpec((1, d), lambda i, de: (0, 0))],
        out_specs=pl.BlockSpec((ROW_TILE, d), lambda i, de: (i, 0)),
        scratch_shapes=scratch)
    return pl.pallas_call(
        functools.partial(_final_kernel, tile_of=tile_of),
        out_shape=jax.ShapeDtypeStruct((rt.b * n_len, d), F32),
        grid_spec=grid_spec,
        compiler_params=_cparams("arbitrary"), name="final",
    )(moe[3], x, *margs, final_g.reshape(1, d))


def _pre_mla_kernel(dest_ref, x_ref, r_ref, modp_ref, ys_hbm, mod_ref, g_ref, win_ref, qg_ref, kvg_ref,
                    wq_ref, wqs_ref, wk_ref, wv_ref, vone_ref, ct_ref, st_ref,
                    xo_ref, q_ref, k_ref, v_ref, ybuf, sem):
    x, finish = _moe_update(dest_ref, x_ref, r_ref, modp_ref, ys_hbm, ybuf, sem, lambda t: t, True)
    xo_ref[...] = x
    h = _modulated(x, g_ref, mod_ref, 0)
    p = _dot(h.astype(BF16), win_ref[...])
    cq = _rms(p[:, 0:MLA_Q_RANK], qg_ref[...]).astype(BF16)
    ckv = _rms(p[:, MLA_Q_RANK:MLA_Q_RANK + MLA_KV_RANK], kvg_ref[...]).astype(BF16)
    off = MLA_Q_RANK + MLA_KV_RANK
    ct, st = ct_ref[...], st_ref[...]
    k_rope = p[:, off:off + HEAD_PAD] * ct + p[:, off + HEAD_PAD:off + 2 * HEAD_PAD] * st
    qscale = MLA_SCALE * LOG2_E
    for c0 in range(0, MLA_HEADS * HEAD_PAD, MLA_CHUNK):
        cols = slice(c0, c0 + MLA_CHUNK)
        qa = _dot(cq, wq_ref[:, cols])
        qb = _dot(cq, wqs_ref[:, cols])
        kn = _dot(ckv, wk_ref[:, cols])
        v_ref[:, cols] = (_dot(ckv, wv_ref[:, cols]) + vone_ref[:, cols]).astype(v_ref.dtype)
        for h0 in range(0, MLA_CHUNK, HEAD_PAD):
            sl = slice(h0, h0 + HEAD_PAD)
            out = slice(c0 + h0, c0 + h0 + HEAD_PAD)
            q_ref[:, out] = ((qa[:, sl] * ct + qb[:, sl] * st) * qscale).astype(q_ref.dtype)
            k_ref[:, out] = (kn[:, sl] + k_rope).astype(k_ref.dtype)
    finish()


def _pre_mla(rt, x, moe, mods, g1, wts, ct, st):
    d = x.shape[1]
    w_in, qg, kvg, wq, wqs, wk, wv, vone = wts

    def full(a):
        return pl.BlockSpec(a.shape, lambda i, de: (0,) * a.ndim)

    hq = MLA_HEADS * HEAD_PAD
    mspecs, margs, scratch = _moe_operands(rt, d, moe, lambda t: t)
    grid_spec = pltpu.PrefetchScalarGridSpec(
        num_scalar_prefetch=1, grid=(rt.n_tiles,),
        in_specs=[pl.BlockSpec((ROW_TILE, d), lambda i, de: (i, 0))] + mspecs + [
            pl.BlockSpec((1, 8, d), lambda i, de: (rt.mod_idx(i), 0, 0)),
            pl.BlockSpec((1, d), lambda i, de: (0, 0)),
            full(w_in), full(qg), full(kvg), full(wq), full(wqs), full(wk), full(wv), full(vone),
            pl.BlockSpec((ROW_TILE, HEAD_PAD), lambda i, de: (rt.pos_idx(i), 0)),
            pl.BlockSpec((ROW_TILE, HEAD_PAD), lambda i, de: (rt.pos_idx(i), 0))],
        out_specs=[pl.BlockSpec((ROW_TILE, d), lambda i, de: (i, 0)),
                   pl.BlockSpec((ROW_TILE, hq), lambda i, de: (i, 0)),
                   pl.BlockSpec((ROW_TILE, hq), lambda i, de: (i, 0)),
                   pl.BlockSpec((ROW_TILE, hq), lambda i, de: (i, 0))],
        scratch_shapes=scratch)
    return pl.pallas_call(
        _pre_mla_kernel,
        out_shape=[jax.ShapeDtypeStruct((rt.rows, d), F32), jax.ShapeDtypeStruct((rt.rows, hq), BF16),
                   jax.ShapeDtypeStruct((rt.rows, hq), BF16), jax.ShapeDtypeStruct((rt.rows, hq), BF16)],
        grid_spec=grid_spec,
        compiler_params=_cparams("arbitrary"), name="pre_mla",
    )(moe[3], x, *margs, mods, g1.reshape(1, d), w_in, qg, kvg, wq, wqs, wk, wv, vone, ct, st)


def _mla_params(w_in, q_g, kv_g, w_uq, w_ukv):
    d = w_in.shape[0]
    hp, hr = HEAD_PAD, MLA_ROPE // 2
    nq = MLA_NOPE + MLA_ROPE
    kr = w_in[:, MLA_Q_RANK + MLA_KV_RANK:]
    z = jnp.zeros((d, MLA_NOPE), F32)
    zt = jnp.zeros((d, hp - nq), F32)
    kr_a = jnp.concatenate([z, kr, zt], axis=1)
    kr_b = jnp.concatenate([z, -kr[:, hr:], kr[:, :hr], zt], axis=1)
    w_in_p = jnp.concatenate([w_in[:, :MLA_Q_RANK + MLA_KV_RANK], kr_a, kr_b], axis=1).astype(BF16)
    wq = w_uq.reshape(MLA_Q_RANK, MLA_HEADS, nq)
    zq = jnp.zeros((MLA_Q_RANK, MLA_HEADS, hp - nq), F32)
    wq_a = jnp.concatenate([wq, zq], axis=2).reshape(MLA_Q_RANK, MLA_HEADS * hp).astype(BF16)
    wq_b = jnp.concatenate([jnp.zeros_like(wq[:, :, :MLA_NOPE]), -wq[:, :, MLA_NOPE + hr:],
                            wq[:, :, MLA_NOPE:MLA_NOPE + hr], zq], axis=2)
    wq_b = wq_b.reshape(MLA_Q_RANK, MLA_HEADS * hp).astype(BF16)
    wkv = w_ukv.reshape(MLA_KV_RANK, MLA_HEADS, MLA_NOPE + MLA_V)
    wk = jnp.concatenate([wkv[:, :, :MLA_NOPE], jnp.zeros((MLA_KV_RANK, MLA_HEADS, hp - MLA_NOPE), F32)], axis=2)
    wk = wk.reshape(MLA_KV_RANK, MLA_HEADS * hp).astype(BF16)
    wv = wkv[:, :, MLA_NOPE:].reshape(MLA_KV_RANK, MLA_HEADS // 2, 2, MLA_V)
    zv = jnp.zeros((MLA_KV_RANK, MLA_HEADS // 2, hp - MLA_V), F32)
    wv = jnp.concatenate([wv[:, :, 0], zv, zv, wv[:, :, 1]], axis=2).reshape(MLA_KV_RANK, MLA_HEADS * hp)
    lane = jnp.arange(2 * hp) % (2 * hp)
    vone = jnp.tile(jnp.where((lane == ATT_DEN_EVEN) | (lane == hp + ATT_DEN_ODD), 1.0, 0.0), MLA_HEADS // 2)
    return (w_in_p, q_g.reshape(1, -1), kv_g.reshape(1, -1), wq_a, wq_b, wk, wv.astype(BF16),
            vone.reshape(1, -1).astype(F32))


def _attn_kernel(q_ref, k_ref, v_ref, o_ref, *, s_len, l_len):
    t = pl.program_id(2)
    lane = lax.broadcasted_iota(jnp.int32, (ATT_TQ, 2 * MLA_V), 1)

    def attend(nk):
        for pair in range(ATT_HEADS // 2):
            outs = []
            for j, den_lane in ((2 * pair, ATT_DEN_EVEN), (2 * pair + 1, ATT_DEN_ODD)):
                blk = slice(j * HEAD_PAD, (j + 1) * HEAD_PAD)
                s = _dot_nt(q_ref[0, :, blk], k_ref[0, 0:nk, blk])
                p = jnp.exp2(s - jnp.max(s, axis=1, keepdims=True))
                o = _dot(p.astype(BF16), v_ref[0, 0:nk, blk])
                outs.append(o / o[:, den_lane:den_lane + 1])
            o_ref[0, :, pair * 2 * MLA_V:(pair + 1) * 2 * MLA_V] = (
                jnp.where(lane < MLA_V, outs[0], outs[1]).astype(o_ref.dtype))

    ctx_tiles = l_len // ATT_TQ

    @pl.when(t < ctx_tiles)
    def _():
        attend(l_len)

    @pl.when(t >= ctx_tiles)
    def _():
        attend(s_len)


def _attention(q3, k3, v3, l_len):
    b, s, _ = q3.shape
    hq = ATT_HEADS * HEAD_PAD
    hv = ATT_HEADS * MLA_V
    return pl.pallas_call(
        functools.partial(_attn_kernel, s_len=s, l_len=l_len),
        out_shape=jax.ShapeDtypeStruct((b, s, MLA_HEADS * MLA_V), BF16),
        grid=(b, MLA_HEADS // ATT_HEADS, s // ATT_TQ),
        in_specs=[pl.BlockSpec((1, ATT_TQ, hq), lambda bi, hi, ti: (bi, ti, hi)),
                  pl.BlockSpec((1, s, hq), lambda bi, hi, ti: (bi, 0, hi)),
                  pl.BlockSpec((1, s, hq), lambda bi, hi, ti: (bi, 0, hi))],
        out_specs=pl.BlockSpec((1, ATT_TQ, hv), lambda bi, hi, ti: (bi, ti, hi)),
        compiler_params=_cparams("arbitrary", "arbitrary", "arbitrary"), name="attention",
    )(q3, k3, v3)


def _ret_tables(n, l):
    t = np.arange(n, dtype=np.float64)
    inv = ROPE_BASE ** (-np.arange(0, RET_DK, 2, dtype=np.float64) / RET_DK)
    ang = t[:, None] * inv[None, :]
    cos, sin = np.cos(ang), np.sin(ang)
    cos2 = np.concatenate([np.ones((l, RET_DK)), np.concatenate([cos, cos], axis=1)], axis=0)
    sin2 = np.concatenate([np.zeros((l, RET_DK)), np.concatenate([-sin, sin], axis=1)], axis=0)
    return jnp.asarray(cos2, F32), jnp.asarray(sin2, F32)


def _mla_tables(n, l):
    rows = n // GRID_W
    r_pos = np.repeat(np.arange(rows, dtype=np.float64), GRID_W)
    c_pos = np.tile(np.arange(GRID_W, dtype=np.float64), rows)
    ax = MLA_ROPE // 2
    inv = ROPE_BASE ** (-np.arange(0, ax, 2, dtype=np.float64) / ax)
    ang = np.concatenate([r_pos[:, None] * inv[None, :], c_pos[:, None] * inv[None, :]], axis=-1)
    cos, sin = np.cos(ang), np.sin(ang)
    pad = HEAD_PAD - MLA_NOPE - MLA_ROPE
    ct_l = np.concatenate([np.ones((n, MLA_NOPE)), cos, cos, np.zeros((n, pad))], axis=1)
    st_l = np.concatenate([np.zeros((n, MLA_NOPE)), sin, sin, np.zeros((n, pad))], axis=1)
    ct_c = np.concatenate([np.ones((l, MLA_NOPE + MLA_ROPE)), np.zeros((l, pad))], axis=1)
    ct = np.concatenate([ct_c, ct_l], axis=0)
    st = np.concatenate([np.zeros((l, HEAD_PAD)), st_l], axis=0)
    return jnp.asarray(ct, F32), jnp.asarray(st, F32)


def kernel(x, c, ctx, c_ctx, ada_w, ada_b, norm_g, ab_w_in, ab_w_out, ret_decay_logit, lru_conv_w, lru_conv_b, lru_gate_w, lru_gate_b, lru_lambda, mla_w_in, mla_q_norm_g, mla_kv_norm_g, mla_w_uq, mla_w_ukv, mla_w_out, moe_group_w, moe_group_b, moe_expert_w, moe_expert_b, moe_w_gate, moe_w_up, moe_w_down, final_norm_g):
    b, n, d = x.shape
    l = ctx.shape[1]
    s = l + n
    depth = ada_w.shape[0]
    rt = _Rows(b, s, l)

    nrow = (b + 1 + 7) // 8 * 8
    cvec = jnp.concatenate([c, c_ctx[None, :], jnp.zeros((nrow - b - 1, d), F32)], axis=0)
    ada = _ada_all(cvec, ada_w, ada_b)

    def layer_mods(layer):
        lat = ada[layer, :b].reshape(b, 1, 6, d)
        cx = jnp.broadcast_to(ada[layer, b].reshape(1, 1, 6, d), (b, 1, 6, d))
        m = jnp.concatenate([cx, lat], axis=1)
        m = jnp.concatenate([m, jnp.zeros((b, 2, 2, d), F32)], axis=2)
        return m.reshape(2 * b, 8, d)

    cos2, sin2 = _ret_tables(n, l)
    ct, st = _mla_tables(n, l)

    xs = jnp.concatenate([ctx, x], axis=1).reshape(b * s, d)
    out = None
    moe = None
    for layer in range(depth):
        mods = layer_mods(layer)
        i = layer // 2
        if layer % 2 == 0:
            xs, p = _pre_ab(rt, xs, moe, mods, norm_g[layer, 0], ab_w_in[i].astype(BF16))
            p3 = p.reshape(b, s, -1)
            lg = jax.nn.log_sigmoid(ret_decay_logit[i].astype(F32))
            lgv = jnp.broadcast_to(lg.T[:, :, None], (RET_HEADS, 2, LANES))
            lgv = jnp.concatenate([lgv, jnp.zeros((RET_HEADS, 6, LANES), F32)], axis=1)
            ma = _retention(p3, cos2, sin2, lgv, l).reshape(b * s, -1)
            mb = _rglru(p3, *_lru_params(lru_conv_w[i], lru_conv_b[i], lru_gate_w[i], lru_gate_b[i],
                                         lru_lambda[i]), l).reshape(b * s, -1)
            cb = 0
            w_out = ab_w_out[i].astype(BF16)
        else:
            wts = _mla_params(mla_w_in[i], mla_q_norm_g[i], mla_kv_norm_g[i], mla_w_uq[i], mla_w_ukv[i])
            xs, q, k, v = _pre_mla(rt, xs, moe, mods, norm_g[layer, 0], wts, ct, st)
            att = _attention(q.reshape(b, s, -1), k.reshape(b, s, -1), v.reshape(b, s, -1), l)
            ma = mb = att.reshape(b * s, -1)
            cb = 1
            w_out = mla_w_out[i].astype(BF16)
        wr = jnp.concatenate([moe_group_w[layer], moe_expert_w[layer],
                              jnp.zeros((d, LANES - MOE_GROUPS - MOE_EXPERTS), F32)], axis=1)
        wr_hi = wr.astype(BF16)
        wr = jnp.concatenate([wr_hi, (wr - wr_hi.astype(F32)).astype(BF16)], axis=1)
        br = jnp.concatenate([moe_group_b[layer], moe_expert_b[layer],
                              jnp.zeros((LANES - MOE_GROUPS - MOE_EXPERTS,), F32)]).reshape(1, LANES)
        xs, h2, route, cnt = _post(rt, xs, ma, mb, cb, w_out, mods, norm_g[layer, 1], wr, br)
        dest, block_e, n_valid, fill, nb = _moe_plan(route, cnt)
        xsort = _dispatch(h2, dest, fill, nb * MOE_ROWS)
        ys = _experts(xsort, block_e, n_valid, layer, moe_w_gate, moe_w_up, moe_w_down)
        moe = (route, mods, ys, dest)
    out = _final(rt, xs, moe, final_norm_g, n)
    return out.reshape(b, n, d)
```

```python
import functools

import jax
import jax.numpy as jnp
import numpy as np
from jax import lax
from jax.experimental import pallas as pl
from jax.experimental.pallas import tpu as pltpu

F32 = jnp.float32
BF16 = jnp.bfloat16

EPS = 1e-6
ROPE_BASE = 10000.0
GRID_W = 64

RET_HEADS = 4
RET_DK = 128
RET_CHUNK = 128
RET_HPS = 2
LRU_WIDTH = 512
LRU_BLOCK = 64
LRU_C = 8.0
LRU_HALF = 256
LRU_TILE = 128
LRU_SUB = 8

MLA_HEADS = 16
MLA_NOPE = 64
MLA_ROPE = 32
MLA_V = 64
MLA_Q_RANK = 384
MLA_KV_RANK = 256
MLA_SCALE = (MLA_NOPE + MLA_ROPE) ** -0.5
LOG2_E = 1.4426950408889634
HEAD_PAD = 128
MLA_CHUNK = 512

MOE_GROUPS = 4
MOE_PER_GROUP = 8
MOE_EXPERTS = 32
MOE_ROWS = 512

ROW_TILE = 256
ATT_TQ = 256
ATT_HEADS = 4
ATT_DEN_EVEN = 64
ATT_DEN_ODD = 0
LANES = 128
VMEM_LIMIT = 56 * 1024 * 1024


def _cparams(*sem):
    return pltpu.CompilerParams(dimension_semantics=sem, vmem_limit_bytes=VMEM_LIMIT)


def _rms(x, g):
    return x * lax.rsqrt(jnp.mean(x * x, axis=-1, keepdims=True) + EPS) * g


def _dot(a, b):
    return jnp.dot(a, b, preferred_element_type=F32)


def _dot_nt(a, b):
    return lax.dot_general(a, b, (((1,), (1,)), ((), ())), preferred_element_type=F32)


def _dot_tn(a, b):
    return lax.dot_general(a, b, (((0,), (0,)), ((), ())), preferred_element_type=F32)


def _ada_kernel(s_ref, w_ref, b_ref, o_ref):
    s = jax.nn.silu(s_ref[...])
    o_ref[0] = _dot(s.astype(BF16), w_ref[0].astype(BF16)) + b_ref[0]


def _ada_all(cvec, ada_w, ada_b):
    depth, d, n6 = ada_w.shape
    rows = cvec.shape[0]
    tn = n6 // 4
    return pl.pallas_call(
        _ada_kernel,
        out_shape=jax.ShapeDtypeStruct((depth, rows, n6), F32),
        grid=(depth, n6 // tn),
        in_specs=[pl.BlockSpec((rows, d), lambda l, j: (0, 0)),
                  pl.BlockSpec((1, d, tn), lambda l, j: (l, 0, j)),
                  pl.BlockSpec((1, 1, tn), lambda l, j: (l, 0, j))],
        out_specs=pl.BlockSpec((1, rows, tn), lambda l, j: (l, 0, j)),
        compiler_params=_cparams("arbitrary", "arbitrary"),
        name="adaln",
    )(cvec, ada_w, ada_b.reshape(depth, 1, n6))


class _Rows:
    def __init__(self, b, s, l):
        assert s % ROW_TILE == 0 and l % ROW_TILE == 0
        self.b, self.s, self.l = b, s, l
        self.tpb = s // ROW_TILE
        self.ctx_tiles = l // ROW_TILE
        self.n_tiles = b * self.tpb
        self.rows = b * s

    def mod_idx(self, i):
        return 2 * (i // self.tpb) + jnp.where(i % self.tpb >= self.ctx_tiles, 1, 0)

    def pos_idx(self, i):
        return i % self.tpb


def _modulated(x, g_ref, mod_ref, base):
    h = _rms(x, g_ref[...])
    return h * (1.0 + mod_ref[0, base + 1:base + 2, :]) + mod_ref[0, base:base + 1, :]


def _moe_update(dest_ref, x_ref, r_ref, modp_ref, ys_hbm, ybuf, sem, tile_of, inline):
    i = pl.program_id(0)
    n = pl.num_programs(0)
    slot = i % 2

    def start(tok, j, sl):
        for k in range(2):
            pltpu.make_async_copy(ys_hbm.at[pl.ds(dest_ref[2 * tok + k], 1)], ybuf.at[sl, k, pl.ds(j, 1)],
                                  sem.at[sl]).start()

    def issue_loop(step, sl):
        base = tile_of(step) * ROW_TILE
        lax.fori_loop(0, ROW_TILE, lambda j, c: (start(base + j, j, sl), c)[1], 0, unroll=8)

    def wait(sl):
        for k in range(2):
            pltpu.make_async_copy(ys_hbm.at[pl.ds(0, ROW_TILE)], ybuf.at[sl, k], sem.at[sl]).wait()

    @pl.when(i == 0)
    def _():
        issue_loop(0, 0)

    if not inline:
        @pl.when(i + 1 < n)
        def _():
            issue_loop(i + 1, 1 - slot)

    wait(slot)
    r = r_ref[...]
    y = ybuf[slot, 0] * r[:, 2:3] + ybuf[slot, 1] * r[:, 3:4]
    x = x_ref[...] + modp_ref[0, 5:6, :] * y
    if not inline:
        return x, None
    base = tile_of(jnp.minimum(i + 1, n - 1)) * ROW_TILE
    for j in range(ROW_TILE):
        start(base + j, j, 1 - slot)

    def finish():
        @pl.when(i == n - 1)
        def _():
            wait(1 - slot)
    return x, finish


def _moe_operands(rt, d, moe, tile_of):
    route, modp, ys, _ = moe
    specs = [pl.BlockSpec((ROW_TILE, LANES), lambda i, de: (tile_of(i), 0)),
             pl.BlockSpec((1, 8, d), lambda i, de: (rt.mod_idx(tile_of(i)), 0, 0)),
             pl.BlockSpec(memory_space=pl.ANY)]
    scratch = [pltpu.VMEM((2, 2, ROW_TILE, d), F32), pltpu.SemaphoreType.DMA((2,))]
    return specs, [route, modp, ys], scratch


def _pre_ab_kernel(*refs, has_moe):
    if has_moe:
        dest_ref, x_ref, r_ref, modp_ref, ys_hbm, mod_ref, g_ref, w_ref, xo_ref, p_ref, ybuf, sem = refs
        x, finish = _moe_update(dest_ref, x_ref, r_ref, modp_ref, ys_hbm, ybuf, sem, lambda t: t, True)
        xo_ref[...] = x
    else:
        x_ref, mod_ref, g_ref, w_ref, p_ref = refs
        x = x_ref[...]
    h = _modulated(x, g_ref, mod_ref, 0)
    p_ref[...] = _dot(h.astype(BF16), w_ref[...])
    if has_moe:
        finish()


def _pre_ab(rt, x, moe, mods, g1, w_in):
    d = x.shape[1]
    n_out = w_in.shape[1]
    if moe is None:
        p = pl.pallas_call(
            functools.partial(_pre_ab_kernel, has_moe=False),
            out_shape=jax.ShapeDtypeStruct((rt.rows, n_out), F32),
            grid=(rt.n_tiles,),
            in_specs=[pl.BlockSpec((ROW_TILE, d), lambda i: (i, 0)),
                      pl.BlockSpec((1, 8, d), lambda i: (rt.mod_idx(i), 0, 0)),
                      pl.BlockSpec((1, d), lambda i: (0, 0)),
                      pl.BlockSpec((d, n_out), lambda i: (0, 0))],
            out_specs=pl.BlockSpec((ROW_TILE, n_out), lambda i: (i, 0)),
            compiler_params=_cparams("arbitrary"), name="pre_ab",
        )(x, mods, g1.reshape(1, d), w_in)
        return x, p
    mspecs, margs, scratch = _moe_operands(rt, d, moe, lambda t: t)
    grid_spec = pltpu.PrefetchScalarGridSpec(
        num_scalar_prefetch=1, grid=(rt.n_tiles,),
        in_specs=[pl.BlockSpec((ROW_TILE, d), lambda i, de: (i, 0))] + mspecs + [
            pl.BlockSpec((1, 8, d), lambda i, de: (rt.mod_idx(i), 0, 0)),
            pl.BlockSpec((1, d), lambda i, de: (0, 0)),
            pl.BlockSpec((d, n_out), lambda i, de: (0, 0))],
        out_specs=[pl.BlockSpec((ROW_TILE, d), lambda i, de: (i, 0)),
                   pl.BlockSpec((ROW_TILE, n_out), lambda i, de: (i, 0))],
        scratch_shapes=scratch)
    return pl.pallas_call(
        functools.partial(_pre_ab_kernel, has_moe=True),
        out_shape=[jax.ShapeDtypeStruct((rt.rows, d), F32), jax.ShapeDtypeStruct((rt.rows, n_out), F32)],
        grid_spec=grid_spec,
        compiler_params=_cparams("arbitrary"), name="pre_ab_moe",
    )(moe[3], x, *margs, mods, g1.reshape(1, d), w_in)


def _ret_kernel(q_ref, k_ref, v_ref, g_ref, cos_ref, sin_ref, lg_ref, o_ref, qs, ks, acc, *, s_len, l_len):
    c, dk = RET_CHUNK, RET_DK
    nch, cch = s_len // c, l_len // c
    ii = lax.broadcasted_iota(jnp.int32, (c, c), 0).astype(F32)
    jj = lax.broadcasted_iota(jnp.int32, (c, c), 1).astype(F32)
    diff = ii - jj
    k_scale = RET_DK ** -0.5

    def head_consts(j):
        lgf = lg_ref[j, 0:1, :]
        lgb = lg_ref[j, 1:2, :]
        dmask = (jnp.where(diff > 0, jnp.exp(lgf * jnp.maximum(diff, 0.0)), 0.0)
                 + jnp.where(diff < 0, jnp.exp(lgb * jnp.maximum(-diff, 0.0)), 0.0)
                 + jnp.where(diff == 0, 2.0, 0.0))
        return dict(dmask=dmask,
                    zeta_f=jnp.exp(lgf * (c - 1.0 - ii)), xi_f=jnp.exp(lgf * (ii + 1.0)),
                    zeta_b=jnp.exp(lgb * ii), xi_b=jnp.exp(lgb * (c - ii)),
                    cd_f=jnp.exp(lgf * c), cd_b=jnp.exp(lgb * c))

    hc = [head_consts(j) for j in range(RET_HPS)]

    def fwd(n, sts):
        rows = pl.ds(pl.multiple_of(n * c, c), c)
        cs, sn = cos_ref[rows, :], sin_ref[rows, :]
        out = []
        for j, st in enumerate(sts):
            cols = slice(j * dk, (j + 1) * dk)
            q = q_ref[0, rows, cols]
            k = k_ref[0, rows, cols]
            v = v_ref[0, rows, cols]
            qb = (q * cs + pltpu.roll(q, 64, 1) * sn).astype(BF16)
            kb = ((k * cs + pltpu.roll(k, 64, 1) * sn) * k_scale).astype(BF16)
            qs[rows, cols] = qb
            ks[rows, cols] = kb
            sc = _dot_nt(qb, kb) * hc[j]["dmask"]
            acc[rows, cols] = _dot(sc.astype(BF16), v.astype(BF16)) + _dot(qb, st.astype(BF16)) * hc[j]["xi_f"]
            out.append(hc[j]["cd_f"] * st + _dot_tn(kb, (v * hc[j]["zeta_f"]).astype(BF16)))
        return tuple(out)

    def bwd(n, sts):
        rows = pl.ds(pl.multiple_of(n * c, c), c)
        out = []
        for j, st in enumerate(sts):
            cols = slice(j * dk, (j + 1) * dk)
            qb = qs[rows, cols]
            kb = ks[rows, cols]
            v = v_ref[0, rows, cols]
            y = acc[rows, cols] + _dot(qb, st.astype(BF16)) * hc[j]["xi_b"]
            y = y * lax.rsqrt(jnp.mean(y * y, axis=-1, keepdims=True) + EPS)
            o_ref[0, rows, cols] = (y * jax.nn.silu(g_ref[0, rows, cols])).astype(o_ref.dtype)
            out.append(hc[j]["cd_b"] * st + _dot_tn(kb, (v * hc[j]["zeta_b"]).astype(BF16)))
        return tuple(out)

    zero = tuple(jnp.zeros((c, c), F32) for _ in range(RET_HPS))
    assert nch % 2 == 0 and cch % 2 == 0
    lax.fori_loop(0, nch, fwd, zero, unroll=2)
    sts = lax.fori_loop(0, cch, lambda t, sts: bwd(cch - 1 - t, sts), zero, unroll=2)
    lax.fori_loop(0, nch - cch, lambda t, sts: bwd(nch - 1 - t, sts), sts, unroll=2)


def _retention(p3, cos2, sin2, lgv, l_len):
    b, s, _ = p3.shape
    groups = RET_HEADS // RET_HPS
    w = RET_HPS * RET_DK

    def col(off, **kw):
        return pl.BlockSpec((1, s, w), lambda bi, hi: (bi, 0, off + hi), **kw)

    once = dict(pipeline_mode=pl.Buffered(1))
    return pl.pallas_call(
        functools.partial(_ret_kernel, s_len=s, l_len=l_len),
        out_shape=jax.ShapeDtypeStruct((b, s, RET_HEADS * RET_DK), BF16),
        grid=(b, groups),
        in_specs=[col(0), col(groups), col(2 * groups), col(3 * groups, **once),
                  pl.BlockSpec((s, RET_DK), lambda bi, hi: (0, 0), **once),
                  pl.BlockSpec((s, RET_DK), lambda bi, hi: (0, 0), **once),
                  pl.BlockSpec((RET_HPS, 8, LANES), lambda bi, hi: (hi, 0, 0))],
        out_specs=pl.BlockSpec((1, s, w), lambda bi, hi: (bi, 0, hi)),
        scratch_shapes=[pltpu.VMEM((s, w), BF16), pltpu.VMEM((s, w), BF16), pltpu.VMEM((s, w), F32)],
        compiler_params=_cparams("arbitrary", "arbitrary"), name="retention",
    )(p3, p3, p3, p3, cos2, sin2, lgv)


def _tile_scan(a, b, reverse):
    n = a.shape[0]
    rows = lax.broadcasted_iota(jnp.int32, a.shape, 0)
    step = 1
    while step < n:
        shift = n - step if reverse else step
        a_s = pltpu.roll(a, shift, 0)
        b_s = pltpu.roll(b, shift, 0)
        m = (rows < n - step) if reverse else (rows >= step)
        b = jnp.where(m, a * b_s + b, b)
        a = jnp.where(m, a * a_s, a)
        step *= 2
    return a, b


def _scan_rows(a, b, carry, reverse):
    n = a.shape[0]
    pieces = [None] * (n // LRU_SUB)
    for i in (reversed(range(len(pieces))) if reverse else range(len(pieces))):
        rows = slice(i * LRU_SUB, (i + 1) * LRU_SUB)
        a_c, h_loc = _tile_scan(a[rows], b[rows], reverse)
        pieces[i] = h_loc + a_c * carry
        carry = pieces[i][0:1] if reverse else pieces[i][LRU_SUB - 1:LRU_SUB]
    return jnp.concatenate(pieces, axis=0), carry


def _lru_kernel(x_ref, y_ref, cw_ref, wg_ref, gb_ref, sp_ref, o_ref, xpad, hf, ab, bb, *, s_len, l_len):
    tl, w = LRU_TILE, LRU_HALF
    assert s_len % tl == 0 and l_len % tl == 0
    ntl, ctl = s_len // tl, l_len // tl
    xpad[0:8, :] = jnp.zeros((8, w), F32)
    xpad[s_len + 8:s_len + 16, :] = jnp.zeros((8, w), F32)
    xpad[8:s_len + 8, :] = x_ref[0]
    w0, w1, w2, w3, cb = (cw_ref[0, t:t + 1, :] for t in range(5))
    sp_f = sp_ref[0, 0:1, :]
    sp_b = sp_ref[0, 1:2, :]

    def coeff(gr, gi, sp, xc):
        r = jax.nn.sigmoid(gr)
        i = jax.nn.sigmoid(gi)
        log_a = -LRU_C * r * sp
        th = jnp.tanh(log_a)
        return jnp.exp(log_a), jnp.sqrt(-2.0 * th / (1.0 - th)) * (i * xc)

    def fwd(n, carry):
        r0 = pl.multiple_of(n * tl, tl)
        win = xpad[pl.ds(r0, tl + 16), :]
        win = jnp.concatenate([jnp.where(r0 == l_len, 0.0, win[0:8]), win[8:tl + 8],
                               jnp.where(r0 + tl == l_len, 0.0, win[tl + 8:tl + 16])], axis=0)

        def tap(d):
            return pltpu.roll(win, (tl + 16 - d) % (tl + 16), 0)[8:8 + tl]

        xc = tap(-2) * w0
        xc = xc + tap(-1) * w1
        xc = xc + win[8:8 + tl] * w2
        xc = xc + tap(1) * w3
        xc = xc + cb
        gts = _dot(xc.astype(BF16), wg_ref[0]) + gb_ref[0]
        a_f, b_f = coeff(gts[:, 0:w], gts[:, w:2 * w], sp_f, xc)
        a_b, b_b = coeff(gts[:, 2 * w:3 * w], gts[:, 3 * w:4 * w], sp_b, xc)
        rows = pl.ds(r0, tl)
        ab[rows, :] = a_b
        bb[rows, :] = b_b
        hh, carry = _scan_rows(a_f, b_f, carry, False)
        hf[rows, :] = hh
        return carry

    def bwd(n, carry):
        rows = pl.ds(pl.multiple_of(n * tl, tl), tl)
        hh, carry = _scan_rows(ab[rows, :], bb[rows, :], carry, True)
        o_ref[0, rows, :] = ((hf[rows, :] + hh) * jax.nn.gelu(y_ref[0, rows, :])).astype(o_ref.dtype)
        return carry

    zero = jnp.zeros((1, w), F32)
    assert ntl % 2 == 0 and ctl % 2 == 0
    lax.fori_loop(0, ntl, fwd, zero, unroll=2)
    c = lax.fori_loop(0, ctl, lambda t, c: bwd(ctl - 1 - t, c), zero, unroll=2)
    lax.fori_loop(0, ntl - ctl, lambda t, c: bwd(ntl - 1 - t, c), c, unroll=2)


def _rglru(p3, conv_wb, gate_w, gate_b, sp, l_len):
    b, s, _ = p3.shape
    nh = LRU_WIDTH // LRU_HALF
    xoff = (4 * RET_HEADS * RET_DK) // LRU_HALF
    yoff = xoff + nh
    return pl.pallas_call(
        functools.partial(_lru_kernel, s_len=s, l_len=l_len),
        out_shape=jax.ShapeDtypeStruct((b, s, LRU_WIDTH), BF16),
        grid=(b, nh),
        in_specs=[pl.BlockSpec((1, s, LRU_HALF), lambda bi, j: (bi, 0, xoff + j)),
                  pl.BlockSpec((1, s, LRU_HALF), lambda bi, j: (bi, 0, yoff + j)),
                  pl.BlockSpec((1, 8, LRU_HALF), lambda bi, j: (j, 0, 0)),
                  pl.BlockSpec((1, LRU_HALF, 4 * LRU_HALF), lambda bi, j: (j, 0, 0)),
                  pl.BlockSpec((1, 1, 4 * LRU_HALF), lambda bi, j: (j, 0, 0)),
                  pl.BlockSpec((1, 8, LRU_HALF), lambda bi, j: (j, 0, 0))],
        out_specs=pl.BlockSpec((1, s, LRU_HALF), lambda bi, j: (bi, 0, j)),
        scratch_shapes=[pltpu.VMEM((s + 16, LRU_HALF), F32), pltpu.VMEM((s, LRU_HALF), F32),
                        pltpu.VMEM((s, LRU_HALF), F32), pltpu.VMEM((s, LRU_HALF), F32)],
        compiler_params=_cparams("arbitrary", "arbitrary"), name="rglru",
    )(p3, p3, conv_wb, gate_w, gate_b, sp)


def _lru_params(conv_w, conv_b, gate_w, gate_b, lam):
    nh = LRU_WIDTH // LRU_HALF
    bph = LRU_HALF // LRU_BLOCK
    cw = jnp.concatenate([conv_w, conv_b[None, :], jnp.zeros((3, LRU_WIDTH), F32)], axis=0)
    cw = cw.reshape(8, nh, LRU_HALF).transpose(1, 0, 2)
    eye = jnp.eye(bph, dtype=F32)
    gw = gate_w.reshape(2, 2, nh, bph, LRU_BLOCK, LRU_BLOCK)
    dense = jnp.einsum('dgjkio,kl->jkidglo', gw, eye)
    dense = dense.reshape(nh, LRU_HALF, 4 * LRU_HALF).astype(BF16)
    gb = gate_b.reshape(2, 2, nh, LRU_HALF).transpose(2, 0, 1, 3).reshape(nh, 1, 4 * LRU_HALF)
    sp = jax.nn.softplus(-lam.astype(F32)).reshape(2, nh, LRU_HALF).transpose(1, 0, 2)
    sp = jnp.concatenate([sp, jnp.zeros((nh, 6, LRU_HALF), F32)], axis=1)
    return cw, dense, gb, sp


def _post_kernel(x_ref, ma_ref, mb_ref, w_ref, mod_ref, g_ref, wr_ref, br_ref, xo_ref, h_ref, r_ref, cnt_ref):
    m = jnp.concatenate([ma_ref[...], mb_ref[...]], axis=1)
    o = _dot(m, w_ref[...])
    x = x_ref[...] + mod_ref[0, 2:3, :] * o
    xo_ref[...] = x
    h = _modulated(x, g_ref, mod_ref, 3)
    h_ref[...] = h
    h_hi = h.astype(BF16)
    h_lo = (h - h_hi.astype(F32)).astype(BF16)
    part = _dot(h_hi, wr_ref[...])
    lg = part[:, 0:LANES] + part[:, LANES:2 * LANES] + _dot(h_lo, wr_ref[:, 0:LANES]) + br_ref[...]
    lane = lax.broadcasted_iota(jnp.int32, lg.shape, 1)
    lanef = lane.astype(F32)
    ninf = -jnp.inf
    big = float(LANES)
    gl = jnp.where(lane < MOE_GROUPS, lg, ninf)
    gmax = jnp.max(gl, axis=1, keepdims=True)
    g_top = 1.0 / jnp.sum(jnp.exp(gl - gmax), axis=1, keepdims=True)
    g_sel = jnp.min(jnp.where(gl == gmax, lanef, big), axis=1, keepdims=True)
    lo = MOE_GROUPS + MOE_PER_GROUP * g_sel
    el = jnp.where((lanef >= lo) & (lanef < lo + MOE_PER_GROUP), lg, ninf)
    emax = jnp.max(el, axis=1, keepdims=True)
    esum = jnp.sum(jnp.exp(el - emax), axis=1, keepdims=True)
    i1 = jnp.min(jnp.where(el == emax, lanef, big), axis=1, keepdims=True)
    el2 = jnp.where(lanef == i1, ninf, el)
    m2 = jnp.max(el2, axis=1, keepdims=True)
    i2 = jnp.min(jnp.where(el2 == m2, lanef, big), axis=1, keepdims=True)
    p1 = 1.0 / esum
    p2 = jnp.exp(m2 - emax) / esum
    tot = p1 + p2
    w1 = g_top * (p1 / tot)
    w2 = g_top * (p2 / tot)
    hit1 = lanef == i1
    hit2 = lanef == i2
    onehot = jnp.where(hit1, 1.0, 0.0) + jnp.where(hit2, 1.0, 0.0)
    ti = lax.broadcasted_iota(jnp.int32, (ROW_TILE, ROW_TILE), 0)
    tj = lax.broadcasted_iota(jnp.int32, (ROW_TILE, ROW_TILE), 1)
    earlier = jnp.where(tj < ti, 1.0, 0.0).astype(BF16)

    @pl.when(pl.program_id(0) == 0)
    def _():
        cnt_ref[...] = jnp.zeros(cnt_ref.shape, F32)

    before = _dot(earlier, onehot.astype(BF16)) + cnt_ref[0:1, :]
    k1 = jnp.sum(jnp.where(hit1, before, 0.0), axis=1, keepdims=True)
    k2 = jnp.sum(jnp.where(hit2, before, 0.0), axis=1, keepdims=True)
    cnt_ref[0:1, :] = cnt_ref[0:1, :] + jnp.sum(onehot, axis=0, keepdims=True)
    vals = (i1 - MOE_GROUPS, i2 - MOE_GROUPS, w1, w2, k1, k2)
    slab = jnp.zeros(lg.shape, F32)
    for col, v in enumerate(vals):
        slab = jnp.where(lane == col, v, slab)
    r_ref[...] = slab


def _post(rt, x, ma, mb, cb, w_out, mods, g2, wr, br):
    d = x.shape[1]
    hd = d // 2
    return pl.pallas_call(
        _post_kernel,
        out_shape=[jax.ShapeDtypeStruct((rt.rows, d), F32), jax.ShapeDtypeStruct((rt.rows, d), F32),
                   jax.ShapeDtypeStruct((rt.rows, LANES), F32), jax.ShapeDtypeStruct((8, LANES), F32)],
        grid=(rt.n_tiles,),
        in_specs=[pl.BlockSpec((ROW_TILE, d), lambda i: (i, 0)),
                  pl.BlockSpec((ROW_TILE, hd), lambda i: (i, 0)),
                  pl.BlockSpec((ROW_TILE, hd), lambda i: (i, cb)),
                  pl.BlockSpec((d, d), lambda i: (0, 0)),
                  pl.BlockSpec((1, 8, d), lambda i: (rt.mod_idx(i), 0, 0)),
                  pl.BlockSpec((1, d), lambda i: (0, 0)),
                  pl.BlockSpec((d, 2 * LANES), lambda i: (0, 0)),
                  pl.BlockSpec((1, LANES), lambda i: (0, 0))],
        out_specs=[pl.BlockSpec((ROW_TILE, d), lambda i: (i, 0)),
                   pl.BlockSpec((ROW_TILE, d), lambda i: (i, 0)),
                   pl.BlockSpec((ROW_TILE, LANES), lambda i: (i, 0)),
                   pl.BlockSpec((8, LANES), lambda i: (0, 0))],
        compiler_params=_cparams("arbitrary"), name="post",
    )(x, ma, mb, w_out, mods, g2.reshape(1, d), wr, br)


def _moe_plan(route, cnt):
    mb = MOE_ROWS
    t_count = route.shape[0]
    nb = (2 * t_count + MOE_EXPERTS * (mb - 1) + mb - 1) // mb
    counts = cnt[0, MOE_GROUPS:MOE_GROUPS + MOE_EXPERTS].astype(jnp.int32)
    padded = (counts + mb - 1) // mb * mb
    pend = jnp.cumsum(padded)
    pstart = pend - padded
    experts = jnp.arange(MOE_EXPERTS, dtype=jnp.int32)
    e = route[:, 0:2].astype(jnp.int32)
    first = jnp.sum(jnp.where(e[:, :, None] == experts[None, None, :], pstart[None, None, :], 0), axis=-1)
    dest = (first + route[:, 4:6].astype(jnp.int32)).reshape(-1)
    blk0 = jnp.arange(nb, dtype=jnp.int32) * mb
    block_e = jnp.minimum(jnp.sum((blk0[:, None] >= pend[None, :]).astype(jnp.int32), axis=1), MOE_EXPERTS - 1)
    sel = block_e[:, None] == experts[None, :]
    used = blk0 - jnp.sum(jnp.where(sel, pstart[None, :], 0), axis=1)
    n_valid = jnp.clip(jnp.sum(jnp.where(sel, counts[None, :], 0), axis=1) - used, 0, mb).astype(jnp.int32)
    fill = jnp.concatenate([pstart + counts, padded - counts, pend[-1:], nb - pend[-1:] // mb])
    return dest, block_e, n_valid, fill.astype(jnp.int32), nb


def _dispatch_kernel(dest_ref, fill_ref, h_ref, xs_hbm, stage, zbuf, sem, zsems, *, nt):
    i = pl.program_id(0)
    slot = i % 2
    zsem = zsems.at[0]

    def wait_tile(sl):
        for _ in range(2):
            pltpu.make_async_copy(stage.at[sl], xs_hbm.at[pl.ds(0, ROW_TILE)], sem.at[sl]).wait()

    def zero_padding(wait):
        def go(cp):
            cp.wait() if wait else cp.start()

        def one_row(r):
            go(pltpu.make_async_copy(zbuf.at[pl.ds(0, 1)], xs_hbm.at[pl.ds(r, 1)], zsem))

        def per_expert(e, c):
            start = fill_ref[e]
            n = fill_ref[MOE_EXPERTS + e]
            head = jnp.minimum((8 - (start & 7)) & 7, n)
            mid = pl.multiple_of(lax.shift_left(lax.shift_right_logical(n - head, 3), 3), 8)
            lax.fori_loop(0, head, lambda j, c2: (one_row(start + j), c2)[1], 0)

            @pl.when(mid > 0)
            def _():
                at = pl.multiple_of(start + head, 8)
                go(pltpu.make_async_copy(zbuf.at[pl.ds(0, mid)], xs_hbm.at[pl.ds(at, mid)], zsem))

            lax.fori_loop(0, n - head - mid, lambda j, c2: (one_row(start + head + mid + j), c2)[1], 0)
            return c
        lax.fori_loop(0, MOE_EXPERTS, per_expert, 0)

        def per_block(j, c):
            at = pl.multiple_of(fill_ref[2 * MOE_EXPERTS] + j * MOE_ROWS, MOE_ROWS)
            go(pltpu.make_async_copy(zbuf, xs_hbm.at[pl.ds(at, MOE_ROWS)], zsem))
            return c
        lax.fori_loop(0, fill_ref[2 * MOE_EXPERTS + 1], per_block, 0)

    @pl.when(i == 0)
    def _():
        zbuf[...] = jnp.zeros(zbuf.shape, zbuf.dtype)
        zero_padding(False)

    @pl.when(i >= 2)
    def _():
        wait_tile(slot)

    stage[slot] = h_ref[...]

    for j in range(ROW_TILE):
        src = stage.at[slot, pl.ds(j, 1)]
        for k in range(2):
            row = dest_ref[2 * (i * ROW_TILE + j) + k]
            pltpu.make_async_copy(src, xs_hbm.at[pl.ds(row, 1)], sem.at[slot]).start()

    @pl.when(i == nt - 1)
    def _():
        wait_tile(slot)
        if nt > 1:
            wait_tile(1 - slot)
        zero_padding(True)


def _dispatch(h, dest, fill, n_rows):
    t_count, d = h.shape
    nt = t_count // ROW_TILE
    grid_spec = pltpu.PrefetchScalarGridSpec(
        num_scalar_prefetch=2, grid=(nt,),
        in_specs=[pl.BlockSpec((ROW_TILE, d), lambda i, de, fi: (i, 0))],
        out_specs=pl.BlockSpec(memory_space=pl.ANY),
        scratch_shapes=[pltpu.VMEM((2, ROW_TILE, d), F32), pltpu.VMEM((MOE_ROWS, d), F32),
                        pltpu.SemaphoreType.DMA((2,)), pltpu.SemaphoreType.DMA((1,))])
    return pl.pallas_call(
        functools.partial(_dispatch_kernel, nt=nt),
        out_shape=jax.ShapeDtypeStruct((n_rows, d), F32),
        grid_spec=grid_spec,
        compiler_params=_cparams("arbitrary"), name="dispatch",
    )(dest, fill, h)


def _expert_kernel(be_ref, nv_ref, x_ref, wg_ref, wu_ref, wd_ref, y_ref, wgb, wub, wdb):
    i = pl.program_id(0)

    @pl.when(nv_ref[i] > 0)
    def _():
        @pl.when((i == 0) | (be_ref[i] != be_ref[jnp.maximum(i - 1, 0)]))
        def _():
            wgb[...] = wg_ref[0, 0].astype(BF16)
            wub[...] = wu_ref[0, 0].astype(BF16)
            wdb[...] = wd_ref[0, 0].astype(BF16)

        x = x_ref[...].astype(BF16)
        a = (jax.nn.silu(_dot(x, wgb[...])) * _dot(x, wub[...])).astype(BF16)
        y_ref[...] = _dot(a, wdb[...])

    @pl.when(nv_ref[i] == 0)
    def _():
        y_ref[...] = jnp.zeros(y_ref.shape, y_ref.dtype)


def _experts(xs, block_e, n_valid, layer, wg, wu, wd):
    n_rows, d = xs.shape
    mb = MOE_ROWS
    hid = wg.shape[3]

    def wspec(shape):
        return pl.BlockSpec(shape, lambda i, be, nv: (layer, be[i], 0, 0))

    grid_spec = pltpu.PrefetchScalarGridSpec(
        num_scalar_prefetch=2, grid=(n_rows // mb,),
        in_specs=[pl.BlockSpec((mb, d), lambda i, be, nv: (i, 0)),
                  wspec((1, 1, d, hid)), wspec((1, 1, d, hid)), wspec((1, 1, hid, d))],
        out_specs=pl.BlockSpec((mb, d), lambda i, be, nv: (i, 0)),
        scratch_shapes=[pltpu.VMEM((d, hid), BF16), pltpu.VMEM((d, hid), BF16), pltpu.VMEM((hid, d), BF16)])
    return pl.pallas_call(
        _expert_kernel,
        out_shape=jax.ShapeDtypeStruct((n_rows, d), F32),
        grid_spec=grid_spec,
        compiler_params=_cparams("arbitrary"), name="experts",
    )(block_e, n_valid, xs, wg, wu, wd)


def _final_kernel(dest_ref, x_ref, r_ref, modp_ref, ys_hbm, g_ref, o_ref, ybuf, sem, *, tile_of):
    x, finish = _moe_update(dest_ref, x_ref, r_ref, modp_ref, ys_hbm, ybuf, sem, tile_of, True)
    o_ref[...] = _rms(x, g_ref[...])
    finish()


def _final(rt, x, moe, final_g, n_len):
    d = x.shape[1]
    lt = n_len // ROW_TILE

    def tile_of(i):
        return (i // lt) * rt.tpb + rt.ctx_tiles + i % lt

    mspecs, margs, scratch = _moe_operands(rt, d, moe, tile_of)
    grid_spec = pltpu.PrefetchScalarGridSpec(
        num_scalar_prefetch=1, grid=(rt.b * lt,),
        in_specs=[pl.BlockSpec((ROW_TILE, d), lambda i, de: (tile_of(i), 0))] + mspecs + [
            pl.BlockSpec((1, d), lambda i, de: (0, 0))],
        out_specs=pl.BlockSpec((ROW_TILE, d), lambda i, de: (i, 0)),
        scratch_shapes=scratch)
    return pl.pallas_call(
        functools.partial(_final_kernel, tile_of=tile_of),
        out_shape=jax.ShapeDtypeStruct((rt.b * n_len, d), F32),
        grid_spec=grid_spec,
        compiler_params=_cparams("arbitrary"), name="final",
    )(moe[3], x, *margs, final_g.reshape(1, d))


def _pre_mla_kernel(dest_ref, x_ref, r_ref, modp_ref, ys_hbm, mod_ref, g_ref, win_ref, qg_ref, kvg_ref,
                    wq_ref, wqs_ref, wk_ref, wv_ref, vone_ref, ct_ref, st_ref,
                    xo_ref, q_ref, k_ref, v_ref, ybuf, sem):
    x, finish = _moe_update(dest_ref, x_ref, r_ref, modp_ref, ys_hbm, ybuf, sem, lambda t: t, True)
    xo_ref[...] = x
    h = _modulated(x, g_ref, mod_ref, 0)
    p = _dot(h.astype(BF16), win_ref[...])
    cq = _rms(p[:, 0:MLA_Q_RANK], qg_ref[...]).astype(BF16)
    ckv = _rms(p[:, MLA_Q_RANK:MLA_Q_RANK + MLA_KV_RANK], kvg_ref[...]).astype(BF16)
    off = MLA_Q_RANK + MLA_KV_RANK
    ct, st = ct_ref[...], st_ref[...]
    k_rope = p[:, off:off + HEAD_PAD] * ct + p[:, off + HEAD_PAD:off + 2 * HEAD_PAD] * st
    qscale = MLA_SCALE * LOG2_E
    for c0 in range(0, MLA_HEADS * HEAD_PAD, MLA_CHUNK):
        cols = slice(c0, c0 + MLA_CHUNK)
        qa = _dot(cq, wq_ref[:, cols])
        qb = _dot(cq, wqs_ref[:, cols])
        kn = _dot(ckv, wk_ref[:, cols])
        v_ref[:, cols] = (_dot(ckv, wv_ref[:, cols]) + vone_ref[:, cols]).astype(v_ref.dtype)
        for h0 in range(0, MLA_CHUNK, HEAD_PAD):
            sl = slice(h0, h0 + HEAD_PAD)
            out = slice(c0 + h0, c0 + h0 + HEAD_PAD)
            q_ref[:, out] = ((qa[:, sl] * ct + qb[:, sl] * st) * qscale).astype(q_ref.dtype)
            k_ref[:, out] = (kn[:, sl] + k_rope).astype(k_ref.dtype)
    finish()


def _pre_mla(rt, x, moe, mods, g1, wts, ct, st):
    d = x.shape[1]
    w_in, qg, kvg, wq, wqs, wk, wv, vone = wts

    def full(a):
        return pl.BlockSpec(a.shape, lambda i, de: (0,) * a.ndim)

    hq = MLA_HEADS * HEAD_PAD
    mspecs, margs, scratch = _moe_operands(rt, d, moe, lambda t: t)
    grid_spec = pltpu.PrefetchScalarGridSpec(
        num_scalar_prefetch=1, grid=(rt.n_tiles,),
        in_specs=[pl.BlockSpec((ROW_TILE, d), lambda i, de: (i, 0))] + mspecs + [
            pl.BlockSpec((1, 8, d), lambda i, de: (rt.mod_idx(i), 0, 0)),
            pl.BlockSpec((1, d), lambda i, de: (0, 0)),
            full(w_in), full(qg), full(kvg), full(wq), full(wqs), full(wk), full(wv), full(vone),
            pl.BlockSpec((ROW_TILE, HEAD_PAD), lambda i, de: (rt.pos_idx(i), 0)),
            pl.BlockSpec((ROW_TILE, HEAD_PAD), lambda i, de: (rt.pos_idx(i), 0))],
        out_specs=[pl.BlockSpec((ROW_TILE, d), lambda i, de: (i, 0)),
                   pl.BlockSpec((ROW_TILE, hq), lambda i, de: (i, 0)),
                   pl.BlockSpec((ROW_TILE, hq), lambda i, de: (i, 0)),
                   pl.BlockSpec((ROW_TILE, hq), lambda i, de: (i, 0))],
        scratch_shapes=scratch)
    return pl.pallas_call(
        _pre_mla_kernel,
        out_shape=[jax.ShapeDtypeStruct((rt.rows, d), F32), jax.ShapeDtypeStruct((rt.rows, hq), BF16),
                   jax.ShapeDtypeStruct((rt.rows, hq), BF16), jax.ShapeDtypeStruct((rt.rows, hq), BF16)],
        grid_spec=grid_spec,
        compiler_params=_cparams("arbitrary"), name="pre_mla",
    )(moe[3], x, *margs, mods, g1.reshape(1, d), w_in, qg, kvg, wq, wqs, wk, wv, vone, ct, st)


def _mla_params(w_in, q_g, kv_g, w_uq, w_ukv):
    d = w_in.shape[0]
    hp, hr = HEAD_PAD, MLA_ROPE // 2
    nq = MLA_NOPE + MLA_ROPE
    kr = w_in[:, MLA_Q_RANK + MLA_KV_RANK:]
    z = jnp.zeros((d, MLA_NOPE), F32)
    zt = jnp.zeros((d, hp - nq), F32)
    kr_a = jnp.concatenate([z, kr, zt], axis=1)
    kr_b = jnp.concatenate([z, -kr[:, hr:], kr[:, :hr], zt], axis=1)
    w_in_p = jnp.concatenate([w_in[:, :MLA_Q_RANK + MLA_KV_RANK], kr_a, kr_b], axis=1).astype(BF16)
    wq = w_uq.reshape(MLA_Q_RANK, MLA_HEADS, nq)
    zq = jnp.zeros((MLA_Q_RANK, MLA_HEADS, hp - nq), F32)
    wq_a = jnp.concatenate([wq, zq], axis=2).reshape(MLA_Q_RANK, MLA_HEADS * hp).astype(BF16)
    wq_b = jnp.concatenate([jnp.zeros_like(wq[:, :, :MLA_NOPE]), -wq[:, :, MLA_NOPE + hr:],
                            wq[:, :, MLA_NOPE:MLA_NOPE + hr], zq], axis=2)
    wq_b = wq_b.reshape(MLA_Q_RANK, MLA_HEADS * hp).astype(BF16)
    wkv = w_ukv.reshape(MLA_KV_RANK, MLA_HEADS, MLA_NOPE + MLA_V)
    wk = jnp.concatenate([wkv[:, :, :MLA_NOPE], jnp.zeros((MLA_KV_RANK, MLA_HEADS, hp - MLA_NOPE), F32)], axis=2)
    wk = wk.reshape(MLA_KV_RANK, MLA_HEADS * hp).astype(BF16)
    wv = wkv[:, :, MLA_NOPE:].reshape(MLA_KV_RANK, MLA_HEADS // 2, 2, MLA_V)
    zv = jnp.zeros((MLA_KV_RANK, MLA_HEADS // 2, hp - MLA_V), F32)
    wv = jnp.concatenate([wv[:, :, 0], zv, zv, wv[:, :, 1]], axis=2).reshape(MLA_KV_RANK, MLA_HEADS * hp)
    lane = jnp.arange(2 * hp) % (2 * hp)
    vone = jnp.tile(jnp.where((lane == ATT_DEN_EVEN) | (lane == hp + ATT_DEN_ODD), 1.0, 0.0), MLA_HEADS // 2)
    return (w_in_p, q_g.reshape(1, -1), kv_g.reshape(1, -1), wq_a, wq_b, wk, wv.astype(BF16),
            vone.reshape(1, -1).astype(F32))


def _attn_kernel(q_ref, k_ref, v_ref, o_ref, *, s_len, l_len):
    t = pl.program_id(2)
    lane = lax.broadcasted_iota(jnp.int32, (ATT_TQ, 2 * MLA_V), 1)

    def attend(nk):
        for pair in range(ATT_HEADS // 2):
            outs = []
            for j, den_lane in ((2 * pair, ATT_DEN_EVEN), (2 * pair + 1, ATT_DEN_ODD)):
                blk = slice(j * HEAD_PAD, (j + 1) * HEAD_PAD)
                s = _dot_nt(q_ref[0, :, blk], k_ref[0, 0:nk, blk])
                p = jnp.exp2(s - jnp.max(s, axis=1, keepdims=True))
                o = _dot(p.astype(BF16), v_ref[0, 0:nk, blk])
                outs.append(o / o[:, den_lane:den_lane + 1])
            o_ref[0, :, pair * 2 * MLA_V:(pair + 1) * 2 * MLA_V] = (
                jnp.where(lane < MLA_V, outs[0], outs[1]).astype(o_ref.dtype))

    ctx_tiles = l_len // ATT_TQ

    @pl.when(t < ctx_tiles)
    def _():
        attend(l_len)

    @pl.when(t >= ctx_tiles)
    def _():
        attend(s_len)


def _attention(q3, k3, v3, l_len):
    b, s, _ = q3.shape
    hq = ATT_HEADS * HEAD_PAD
    hv = ATT_HEADS * MLA_V
    return pl.pallas_call(
        functools.partial(_attn_kernel, s_len=s, l_len=l_len),
        out_shape=jax.ShapeDtypeStruct((b, s, MLA_HEADS * MLA_V), BF16),
        grid=(b, MLA_HEADS // ATT_HEADS, s // ATT_TQ),
        in_specs=[pl.BlockSpec((1, ATT_TQ, hq), lambda bi, hi, ti: (bi, ti, hi)),
                  pl.BlockSpec((1, s, hq), lambda bi, hi, ti: (bi, 0, hi)),
                  pl.BlockSpec((1, s, hq), lambda bi, hi, ti: (bi, 0, hi))],
        out_specs=pl.BlockSpec((1, ATT_TQ, hv), lambda bi, hi, ti: (bi, ti, hi)),
        compiler_params=_cparams("arbitrary", "arbitrary", "arbitrary"), name="attention",
    )(q3, k3, v3)


def _ret_tables(n, l):
    t = np.arange(n, dtype=np.float64)
    inv = ROPE_BASE ** (-np.arange(0, RET_DK, 2, dtype=np.float64) / RET_DK)
    ang = t[:, None] * inv[None, :]
    cos, sin = np.cos(ang), np.sin(ang)
    cos2 = np.concatenate([np.ones((l, RET_DK)), np.concatenate([cos, cos], axis=1)], axis=0)
    sin2 = np.concatenate([np.zeros((l, RET_DK)), np.concatenate([-sin, sin], axis=1)], axis=0)
    return jnp.asarray(cos2, F32), jnp.asarray(sin2, F32)


def _mla_tables(n, l):
    rows = n // GRID_W
    r_pos = np.repeat(np.arange(rows, dtype=np.float64), GRID_W)
    c_pos = np.tile(np.arange(GRID_W, dtype=np.float64), rows)
    ax = MLA_ROPE // 2
    inv = ROPE_BASE ** (-np.arange(0, ax, 2, dtype=np.float64) / ax)
    ang = np.concatenate([r_pos[:, None] * inv[None, :], c_pos[:, None] * inv[None, :]], axis=-1)
    cos, sin = np.cos(ang), np.sin(ang)
    pad = HEAD_PAD - MLA_NOPE - MLA_ROPE
    ct_l = np.concatenate([np.ones((n, MLA_NOPE)), cos, cos, np.zeros((n, pad))], axis=1)
    st_l = np.concatenate([np.zeros((n, MLA_NOPE)), sin, sin, np.zeros((n, pad))], axis=1)
    ct_c = np.concatenate([np.ones((l, MLA_NOPE + MLA_ROPE)), np.zeros((l, pad))], axis=1)
    ct = np.concatenate([ct_c, ct_l], axis=0)
    st = np.concatenate([np.zeros((l, HEAD_PAD)), st_l], axis=0)
    return jnp.asarray(ct, F32), jnp.asarray(st, F32)


def kernel(x, c, ctx, c_ctx, ada_w, ada_b, norm_g, ab_w_in, ab_w_out, ret_decay_logit, lru_conv_w, lru_conv_b, lru_gate_w, lru_gate_b, lru_lambda, mla_w_in, mla_q_norm_g, mla_kv_norm_g, mla_w_uq, mla_w_ukv, mla_w_out, moe_group_w, moe_group_b, moe_expert_w, moe_expert_b, moe_w_gate, moe_w_up, moe_w_down, final_norm_g):
    b, n, d = x.shape
    l = ctx.shape[1]
    s = l + n
    depth = ada_w.shape[0]
    rt = _Rows(b, s, l)

    nrow = (b + 1 + 7) // 8 * 8
    cvec = jnp.concatenate([c, c_ctx[None, :], jnp.zeros((nrow - b - 1, d), F32)], axis=0)
    ada = _ada_all(cvec, ada_w, ada_b)

    def layer_mods(layer):
        lat = ada[layer, :b].reshape(b, 1, 6, d)
        cx = jnp.broadcast_to(ada[layer, b].reshape(1, 1, 6, d), (b, 1, 6, d))
        m = jnp.concatenate([cx, lat], axis=1)
        m = jnp.concatenate([m, jnp.zeros((b, 2, 2, d), F32)], axis=2)
        return m.reshape(2 * b, 8, d)

    cos2, sin2 = _ret_tables(n, l)
    ct, st = _mla_tables(n, l)

    xs = jnp.concatenate([ctx, x], axis=1).reshape(b * s, d)
    out = None
    moe = None
    for layer in range(depth):
        mods = layer_mods(layer)
        i = layer // 2
        if layer % 2 == 0:
            xs, p = _pre_ab(rt, xs, moe, mods, norm_g[layer, 0], ab_w_in[i].astype(BF16))
            p3 = p.reshape(b, s, -1)
            lg = jax.nn.log_sigmoid(ret_decay_logit[i].astype(F32))
            lgv = jnp.broadcast_to(lg.T[:, :, None], (RET_HEADS, 2, LANES))
            lgv = jnp.concatenate([lgv, jnp.zeros((RET_HEADS, 6, LANES), F32)], axis=1)
            ma = _retention(p3, cos2, sin2, lgv, l).reshape(b * s, -1)
            mb = _rglru(p3, *_lru_params(lru_conv_w[i], lru_conv_b[i], lru_gate_w[i], lru_gate_b[i],
                                         lru_lambda[i]), l).reshape(b * s, -1)
            cb = 0
            w_out = ab_w_out[i].astype(BF16)
        else:
            wts = _mla_params(mla_w_in[i], mla_q_norm_g[i], mla_kv_norm_g[i], mla_w_uq[i], mla_w_ukv[i])
            xs, q, k, v = _pre_mla(rt, xs, moe, mods, norm_g[layer, 0], wts, ct, st)
            att = _attention(q.reshape(b, s, -1), k.reshape(b, s, -1), v.reshape(b, s, -1), l)
            ma = mb = att.reshape(b * s, -1)
            cb = 1
            w_out = mla_w_out[i].astype(BF16)
        wr = jnp.concatenate([moe_group_w[layer], moe_expert_w[layer],
                              jnp.zeros((d, LANES - MOE_GROUPS - MOE_EXPERTS), F32)], axis=1)
        wr_hi = wr.astype(BF16)
        wr = jnp.concatenate([wr_hi, (wr - wr_hi.astype(F32)).astype(BF16)], axis=1)
        br = jnp.concatenate([moe_group_b[layer], moe_expert_b[layer],
                              jnp.zeros((LANES - MOE_GROUPS - MOE_EXPERTS,), F32)]).reshape(1, LANES)
        xs, h2, route, cnt = _post(rt, xs, ma, mb, cb, w_out, mods, norm_g[layer, 1], wr, br)
        dest, block_e, n_valid, fill, nb = _moe_plan(route, cnt)
        xsort = _dispatch(h2, dest, fill, nb * MOE_ROWS)
        ys = _experts(xsort, block_e, n_valid, layer, moe_w_gate, moe_w_up, moe_w_down)
        moe = (route, mods, ys, dest)
    out = _final(rt, xs, moe, final_norm_g, n)
    return out.reshape(b, n, d)
```

```python
import functools

import jax
import jax.numpy as jnp
import numpy as np
from jax import lax
from jax.experimental import pallas as pl
from jax.experimental.pallas import tpu as pltpu

F32 = jnp.float32
BF16 = jnp.bfloat16

EPS = 1e-6
ROPE_BASE = 10000.0
GRID_W = 64

RET_HEADS = 4
RET_DK = 128
RET_CHUNK = 128
RET_HPS = 2
LRU_WIDTH = 512
LRU_BLOCK = 64
LRU_C = 8.0
LRU_HALF = 256
LRU_TILE = 128
LRU_SUB = 8

MLA_HEADS = 16
MLA_NOPE = 64
MLA_ROPE = 32
MLA_V = 64
MLA_Q_RANK = 384
MLA_KV_RANK = 256
MLA_SCALE = (MLA_NOPE + MLA_ROPE) ** -0.5
LOG2_E = 1.4426950408889634
HEAD_PAD = 128
MLA_CHUNK = 512

MOE_GROUPS = 4
MOE_PER_GROUP = 8
MOE_EXPERTS = 32
MOE_ROWS = 512

ROW_TILE = 256
ATT_TQ = 256
ATT_HEADS = 4
ATT_DEN_EVEN = 64
ATT_DEN_ODD = 0
LANES = 128
SUBLANES = 8
VMEM_LIMIT = 56 * 1024 * 1024


def _cparams(*sem):
    return pltpu.CompilerParams(dimension_semantics=sem, vmem_limit_bytes=VMEM_LIMIT)


def _rms(x, g):
    return x * lax.rsqrt(jnp.mean(x * x, axis=-1, keepdims=True) + EPS) * g


def _dot(a, b):
    return jnp.dot(a, b, preferred_element_type=F32)


def _dot_nt(a, b):
    return lax.dot_general(a, b, (((1,), (1,)), ((), ())), preferred_element_type=F32)


def _dot_tn(a, b):
    return lax.dot_general(a, b, (((0,), (0,)), ((), ())), preferred_element_type=F32)


def _ada_kernel(s_ref, w_ref, b_ref, o_ref):
    s = jax.nn.silu(s_ref[...])
    o_ref[0] = _dot(s.astype(BF16), w_ref[0].astype(BF16)) + b_ref[0]


def _ada_all(cvec, ada_w, ada_b):
    depth, d, n6 = ada_w.shape
    rows = cvec.shape[0]
    tn = n6 // 4
    return pl.pallas_call(
        _ada_kernel,
        out_shape=jax.ShapeDtypeStruct((depth, rows, n6), F32),
        grid=(depth, n6 // tn),
        in_specs=[pl.BlockSpec((rows, d), lambda l, j: (0, 0)),
                  pl.BlockSpec((1, d, tn), lambda l, j: (l, 0, j)),
                  pl.BlockSpec((1, 1, tn), lambda l, j: (l, 0, j))],
        out_specs=pl.BlockSpec((1, rows, tn), lambda l, j: (l, 0, j)),
        compiler_params=_cparams("arbitrary", "arbitrary"),
        name="adaln",
    )(cvec, ada_w, ada_b.reshape(depth, 1, n6))


class _Rows:
    def __init__(self, b, s, l):
        assert s % ROW_TILE == 0 and l % ROW_TILE == 0
        self.b, self.s, self.l = b, s, l
        self.tpb = s // ROW_TILE
        self.ctx_tiles = l // ROW_TILE
        self.n_tiles = b * self.tpb
        self.rows = b * s

    def mod_idx(self, i):
        return 2 * (i // self.tpb) + jnp.where(i % self.tpb >= self.ctx_tiles, 1, 0)

    def pos_idx(self, i):
        return i % self.tpb


def _modulated(x, g_ref, mod_ref, base):
    h = _rms(x, g_ref[...])
    return h * (1.0 + mod_ref[0, base + 1:base + 2, :]) + mod_ref[0, base:base + 1, :]


def _moe_update(dest_ref, x_ref, r_ref, modp_ref, ys_hbm, ybuf, sem, tile_of, inline):
    i = pl.program_id(0)
    n = pl.num_programs(0)
    slot = i % 2

    def start(tok, j, sl):
        for k in range(2):
            pltpu.make_async_copy(ys_hbm.at[pl.ds(dest_ref[2 * tok + k], 1)], ybuf.at[sl, k, pl.ds(j, 1)],
                                  sem.at[sl]).start()

    def issue_loop(step, sl):
        base = tile_of(step) * ROW_TILE
        lax.fori_loop(0, ROW_TILE, lambda j, c: (start(base + j, j, sl), c)[1], 0, unroll=8)

    def wait(sl):
        for k in range(2):
            pltpu.make_async_copy(ys_hbm.at[pl.ds(0, ROW_TILE)], ybuf.at[sl, k], sem.at[sl]).wait()

    @pl.when(i == 0)
    def _():
        issue_loop(0, 0)

    if not inline:
        @pl.when(i + 1 < n)
        def _():
            issue_loop(i + 1, 1 - slot)

    wait(slot)
    r = r_ref[...]
    y = ybuf[slot, 0] * r[:, 2:3] + ybuf[slot, 1] * r[:, 3:4]
    x = x_ref[...] + modp_ref[0, 5:6, :] * y
    if not inline:
        return x, None
    base = tile_of(jnp.minimum(i + 1, n - 1)) * ROW_TILE
    for j in range(ROW_TILE):
        start(base + j, j, 1 - slot)

    def finish():
        @pl.when(i == n - 1)
        def _():
            wait(1 - slot)
    return x, finish


def _moe_operands(rt, d, moe, tile_of):
    route, modp, ys, _ = moe
    specs = [pl.BlockSpec((ROW_TILE, LANES), lambda i, de: (tile_of(i), 0)),
             pl.BlockSpec((1, 8, d), lambda i, de: (rt.mod_idx(tile_of(i)), 0, 0)),
             pl.BlockSpec(memory_space=pl.ANY)]
    scratch = [pltpu.VMEM((2, 2, ROW_TILE, d), F32), pltpu.SemaphoreType.DMA((2,))]
    return specs, [route, modp, ys], scratch


def _pre_ab_kernel(*refs, has_moe):
    if has_moe:
        dest_ref, x_ref, r_ref, modp_ref, ys_hbm, mod_ref, g_ref, w_ref, xo_ref, p_ref, ybuf, sem = refs
        x, finish = _moe_update(dest_ref, x_ref, r_ref, modp_ref, ys_hbm, ybuf, sem, lambda t: t, True)
        xo_ref[...] = x
    else:
        x_ref, mod_ref, g_ref, w_ref, p_ref = refs
        x = x_ref[...]
    h = _modulated(x, g_ref, mod_ref, 0)
    p_ref[...] = _dot(h.astype(BF16), w_ref[...])
    if has_moe:
        finish()


def _pre_ab(rt, x, moe, mods, g1, w_in):
    d = x.shape[1]
    n_out = w_in.shape[1]
    if moe is None:
        p = pl.pallas_call(
            functools.partial(_pre_ab_kernel, has_moe=False),
            out_shape=jax.ShapeDtypeStruct((rt.rows, n_out), F32),
            grid=(rt.n_tiles,),
            in_specs=[pl.BlockSpec((ROW_TILE, d), lambda i: (i, 0)),
                      pl.BlockSpec((1, 8, d), lambda i: (rt.mod_idx(i), 0, 0)),
                      pl.BlockSpec((1, d), lambda i: (0, 0)),
                      pl.BlockSpec((d, n_out), lambda i: (0, 0))],
            out_specs=pl.BlockSpec((ROW_TILE, n_out), lambda i: (i, 0)),
            compiler_params=_cparams("arbitrary"), name="pre_ab",
        )(x, mods, g1.reshape(1, d), w_in)
        return x, p
    mspecs, margs, scratch = _moe_operands(rt, d, moe, lambda t: t)
    grid_spec = pltpu.PrefetchScalarGridSpec(
        num_scalar_prefetch=1, grid=(rt.n_tiles,),
        in_specs=[pl.BlockSpec((ROW_TILE, d), lambda i, de: (i, 0))] + mspecs + [
            pl.BlockSpec((1, 8, d), lambda i, de: (rt.mod_idx(i), 0, 0)),
            pl.BlockSpec((1, d), lambda i, de: (0, 0)),
            pl.BlockSpec((d, n_out), lambda i, de: (0, 0))],
        out_specs=[pl.BlockSpec((ROW_TILE, d), lambda i, de: (i, 0)),
                   pl.BlockSpec((ROW_TILE, n_out), lambda i, de: (i, 0))],
        scratch_shapes=scratch)
    return pl.pallas_call(
        functools.partial(_pre_ab_kernel, has_moe=True),
        out_shape=[jax.ShapeDtypeStruct((rt.rows, d), F32), jax.ShapeDtypeStruct((rt.rows, n_out), F32)],
        grid_spec=grid_spec,
        compiler_params=_cparams("arbitrary"), name="pre_ab_moe",
    )(moe[3], x, *margs, mods, g1.reshape(1, d), w_in)


def _ret_kernel(q_ref, k_ref, v_ref, g_ref, cos_ref, sin_ref, lg_ref, o_ref, qs, ks, acc, *, s_len, l_len):
    c, dk = RET_CHUNK, RET_DK
    nch, cch = s_len // c, l_len // c
    ii = lax.broadcasted_iota(jnp.int32, (c, c), 0).astype(F32)
    jj = lax.broadcasted_iota(jnp.int32, (c, c), 1).astype(F32)
    diff = ii - jj
    k_scale = RET_DK ** -0.5

    def head_consts(j):
        lgf = lg_ref[j, 0:1, :]
        lgb = lg_ref[j, 1:2, :]
        dmask = (jnp.where(diff > 0, jnp.exp(lgf * jnp.maximum(diff, 0.0)), 0.0)
                 + jnp.where(diff < 0, jnp.exp(lgb * jnp.maximum(-diff, 0.0)), 0.0)
                 + jnp.where(diff == 0, 2.0, 0.0))
        return dict(dmask=dmask,
                    zeta_f=jnp.exp(lgf * (c - 1.0 - ii)), xi_f=jnp.exp(lgf * (ii + 1.0)),
                    zeta_b=jnp.exp(lgb * ii), xi_b=jnp.exp(lgb * (c - ii)),
                    cd_f=jnp.exp(lgf * c), cd_b=jnp.exp(lgb * c))

    hc = [head_consts(j) for j in range(RET_HPS)]

    def fwd(n, sts):
        rows = pl.ds(pl.multiple_of(n * c, c), c)
        cs, sn = cos_ref[rows, :], sin_ref[rows, :]
        out = []
        for j, st in enumerate(sts):
            cols = slice(j * dk, (j + 1) * dk)
            q = q_ref[0, rows, cols]
            k = k_ref[0, rows, cols]
            v = v_ref[0, rows, cols]
            qb = (q * cs + pltpu.roll(q, 64, 1) * sn).astype(BF16)
            kb = ((k * cs + pltpu.roll(k, 64, 1) * sn) * k_scale).astype(BF16)
            qs[rows, cols] = qb
            ks[rows, cols] = kb
            sc = _dot_nt(qb, kb) * hc[j]["dmask"]
            acc[rows, cols] = _dot(sc.astype(BF16), v.astype(BF16)) + _dot(qb, st.astype(BF16)) * hc[j]["xi_f"]
            out.append(hc[j]["cd_f"] * st + _dot_tn(kb, (v * hc[j]["zeta_f"]).astype(BF16)))
        return tuple(out)

    def bwd(n, sts):
        rows = pl.ds(pl.multiple_of(n * c, c), c)
        out = []
        for j, st in enumerate(sts):
            cols = slice(j * dk, (j + 1) * dk)
            qb = qs[rows, cols]
            kb = ks[rows, cols]
            v = v_ref[0, rows, cols]
            y = acc[rows, cols] + _dot(qb, st.astype(BF16)) * hc[j]["xi_b"]
            y = y * lax.rsqrt(jnp.mean(y * y, axis=-1, keepdims=True) + EPS)
            o_ref[0, rows, cols] = (y * jax.nn.silu(g_ref[0, rows, cols])).astype(o_ref.dtype)
            out.append(hc[j]["cd_b"] * st + _dot_tn(kb, (v * hc[j]["zeta_b"]).astype(BF16)))
        return tuple(out)

    zero = tuple(jnp.zeros((c, c), F32) for _ in range(RET_HPS))
    assert nch % 2 == 0 and cch % 2 == 0
    unroll = 4 if (nch - cch) % 4 == 0 else 2
    sts = lax.fori_loop(0, cch, fwd, zero, unroll=2)
    lax.fori_loop(cch, nch, fwd, sts, unroll=unroll)
    sts = lax.fori_loop(0, cch, lambda t, sts: bwd(cch - 1 - t, sts), zero, unroll=2)
    lax.fori_loop(0, nch - cch, lambda t, sts: bwd(nch - 1 - t, sts), sts, unroll=unroll)


def _retention(p3, cos2, sin2, lgv, l_len):
    b, s, _ = p3.shape
    groups = RET_HEADS // RET_HPS
    w = RET_HPS * RET_DK

    def col(off, **kw):
        return pl.BlockSpec((1, s, w), lambda bi, hi: (bi, 0, off + hi), **kw)

    once = dict(pipeline_mode=pl.Buffered(1))
    return pl.pallas_call(
        functools.partial(_ret_kernel, s_len=s, l_len=l_len),
        out_shape=jax.ShapeDtypeStruct((b, s, RET_HEADS * RET_DK), BF16),
        grid=(b, groups),
        in_specs=[col(0), col(groups), col(2 * groups), col(3 * groups, **once),
                  pl.BlockSpec((s, RET_DK), lambda bi, hi: (0, 0), **once),
                  pl.BlockSpec((s, RET_DK), lambda bi, hi: (0, 0), **once),
                  pl.BlockSpec((RET_HPS, 8, LANES), lambda bi, hi: (hi, 0, 0))],
        out_specs=pl.BlockSpec((1, s, w), lambda bi, hi: (bi, 0, hi)),
        scratch_shapes=[pltpu.VMEM((s, w), BF16), pltpu.VMEM((s, w), BF16), pltpu.VMEM((s, w), F32)],
        compiler_params=_cparams("arbitrary", "arbitrary"), name="retention",
    )(p3, p3, p3, p3, cos2, sin2, lgv)


def _tile_scan(a, b, reverse):
    n = a.shape[0]
    rows = lax.broadcasted_iota(jnp.int32, a.shape, 0)
    step = 1
    while step < n:
        shift = n - step if reverse else step
        a_s = pltpu.roll(a, shift, 0)
        b_s = pltpu.roll(b, shift, 0)
        m = (rows < n - step) if reverse else (rows >= step)
        b = jnp.where(m, a * b_s + b, b)
        a = jnp.where(m, a * a_s, a)
        step *= 2
    return a, b


def _scan_rows(a, b, carry, reverse):
    n = a.shape[0]
    pieces = [None] * (n // LRU_SUB)
    for i in (reversed(range(len(pieces))) if reverse else range(len(pieces))):
        rows = slice(i * LRU_SUB, (i + 1) * LRU_SUB)
        a_c, h_loc = _tile_scan(a[rows], b[rows], reverse)
        pieces[i] = h_loc + a_c * carry
        carry = pieces[i][0:1] if reverse else pieces[i][LRU_SUB - 1:LRU_SUB]
    return jnp.concatenate(pieces, axis=0), carry


def _lru_kernel(x_ref, y_ref, cw_ref, wg_ref, gb_ref, sp_ref, o_ref, xpad, hf, ab, bb, *, s_len, l_len):
    tl, w = LRU_TILE, LRU_HALF
    assert s_len % tl == 0 and l_len % tl == 0
    ntl, ctl = s_len // tl, l_len // tl
    xpad[0:8, :] = jnp.zeros((8, w), F32)
    xpad[s_len + 8:s_len + 16, :] = jnp.zeros((8, w), F32)
    xpad[8:s_len + 8, :] = x_ref[0]
    w0, w1, w2, w3, cb = (cw_ref[0, t:t + 1, :] for t in range(5))
    sp_f = sp_ref[0, 0:1, :]
    sp_b = sp_ref[0, 1:2, :]

    def coeff(gr, gi, sp, xc):
        r = jax.nn.sigmoid(gr)
        i = jax.nn.sigmoid(gi)
        log_a = -LRU_C * r * sp
        th = jnp.tanh(log_a)
        return jnp.exp(log_a), jnp.sqrt(-2.0 * th / (1.0 - th)) * (i * xc)

    def fwd(n, carry):
        r0 = pl.multiple_of(n * tl, tl)
        win = xpad[pl.ds(r0, tl + 16), :]
        win = jnp.concatenate([jnp.where(r0 == l_len, 0.0, win[0:8]), win[8:tl + 8],
                               jnp.where(r0 + tl == l_len, 0.0, win[tl + 8:tl + 16])], axis=0)

        def tap(d):
            return pltpu.roll(win, (tl + 16 - d) % (tl + 16), 0)[8:8 + tl]

        xc = tap(-2) * w0
        xc = xc + tap(-1) * w1
        xc = xc + win[8:8 + tl] * w2
        xc = xc + tap(1) * w3
        xc = xc + cb
        gts = _dot(xc.astype(BF16), wg_ref[0]) + gb_ref[0]
        a_f, b_f = coeff(gts[:, 0:w], gts[:, w:2 * w], sp_f, xc)
        a_b, b_b = coeff(gts[:, 2 * w:3 * w], gts[:, 3 * w:4 * w], sp_b, xc)
        rows = pl.ds(r0, tl)
        ab[rows, :] = a_b
        bb[rows, :] = b_b
        hh, carry = _scan_rows(a_f, b_f, carry, False)
        hf[rows, :] = hh
        return carry

    def bwd(n, carry):
        rows = pl.ds(pl.multiple_of(n * tl, tl), tl)
        hh, carry = _scan_rows(ab[rows, :], bb[rows, :], carry, True)
        o_ref[0, rows, :] = ((hf[rows, :] + hh) * jax.nn.gelu(y_ref[0, rows, :])).astype(o_ref.dtype)
        return carry

    zero = jnp.zeros((1, w), F32)
    assert ntl % 2 == 0 and ctl % 2 == 0
    lax.fori_loop(0, ntl, fwd, zero, unroll=2)
    c = lax.fori_loop(0, ctl, lambda t, c: bwd(ctl - 1 - t, c), zero, unroll=2)
    lax.fori_loop(0, ntl - ctl, lambda t, c: bwd(ntl - 1 - t, c), c, unroll=2)


def _rglru(p3, conv_wb, gate_w, gate_b, sp, l_len):
    b, s, _ = p3.shape
    nh = LRU_WIDTH // LRU_HALF
    xoff = (4 * RET_HEADS * RET_DK) // LRU_HALF
    yoff = xoff + nh
    return pl.pallas_call(
        functools.partial(_lru_kernel, s_len=s, l_len=l_len),
        out_shape=jax.ShapeDtypeStruct((b, s, LRU_WIDTH), BF16),
        grid=(b, nh),
        in_specs=[pl.BlockSpec((1, s, LRU_HALF), lambda bi, j: (bi, 0, xoff + j)),
                  pl.BlockSpec((1, s, LRU_HALF), lambda bi, j: (bi, 0, yoff + j)),
                  pl.BlockSpec((1, 8, LRU_HALF), lambda bi, j: (j, 0, 0)),
                  pl.BlockSpec((1, LRU_HALF, 4 * LRU_HALF), lambda bi, j: (j, 0, 0)),
                  pl.BlockSpec((1, 1, 4 * LRU_HALF), lambda bi, j: (j, 0, 0)),
                  pl.BlockSpec((1, 8, LRU_HALF), lambda bi, j: (j, 0, 0))],
        out_specs=pl.BlockSpec((1, s, LRU_HALF), lambda bi, j: (bi, 0, j)),
        scratch_shapes=[pltpu.VMEM((s + 16, LRU_HALF), F32), pltpu.VMEM((s, LRU_HALF), F32),
                        pltpu.VMEM((s, LRU_HALF), F32), pltpu.VMEM((s, LRU_HALF), F32)],
        compiler_params=_cparams("arbitrary", "arbitrary"), name="rglru",
    )(p3, p3, conv_wb, gate_w, gate_b, sp)


def _lru_params(conv_w, conv_b, gate_w, gate_b, lam):
    nh = LRU_WIDTH // LRU_HALF
    bph = LRU_HALF // LRU_BLOCK
    cw = jnp.concatenate([conv_w, conv_b[None, :], jnp.zeros((3, LRU_WIDTH), F32)], axis=0)
    cw = cw.reshape(8, nh, LRU_HALF).transpose(1, 0, 2)
    eye = jnp.eye(bph, dtype=F32)
    gw = gate_w.reshape(2, 2, nh, bph, LRU_BLOCK, LRU_BLOCK)
    dense = jnp.einsum('dgjkio,kl->jkidglo', gw, eye)
    dense = dense.reshape(nh, LRU_HALF, 4 * LRU_HALF).astype(BF16)
    gb = gate_b.reshape(2, 2, nh, LRU_HALF).transpose(2, 0, 1, 3).reshape(nh, 1, 4 * LRU_HALF)
    sp = jax.nn.softplus(-lam.astype(F32)).reshape(2, nh, LRU_HALF).transpose(1, 0, 2)
    sp = jnp.concatenate([sp, jnp.zeros((nh, 6, LRU_HALF), F32)], axis=1)
    return cw, dense, gb, sp


def _post_kernel(x_ref, ma_ref, mb_ref, w_ref, mod_ref, g_ref, wr_ref, br_ref, xo_ref, h_ref, r_ref, cnt_ref):
    m = jnp.concatenate([ma_ref[...], mb_ref[...]], axis=1)
    o = _dot(m, w_ref[...])
    x = x_ref[...] + mod_ref[0, 2:3, :] * o
    xo_ref[...] = x
    h = _modulated(x, g_ref, mod_ref, 3)
    h_ref[...] = h
    h_hi = h.astype(BF16)
    h_lo = (h - h_hi.astype(F32)).astype(BF16)
    part = _dot(h_hi, wr_ref[...])
    lg = part[:, 0:LANES] + part[:, LANES:2 * LANES] + _dot(h_lo, wr_ref[:, 0:LANES]) + br_ref[...]
    lane = lax.broadcasted_iota(jnp.int32, lg.shape, 1)
    lanef = lane.astype(F32)
    ninf = -jnp.inf
    big = float(LANES)
    gl = jnp.where(lane < MOE_GROUPS, lg, ninf)
    gmax = jnp.max(gl, axis=1, keepdims=True)
    g_top = 1.0 / jnp.sum(jnp.exp(gl - gmax), axis=1, keepdims=True)
    g_sel = jnp.min(jnp.where(gl == gmax, lanef, big), axis=1, keepdims=True)
    lo = MOE_GROUPS + MOE_PER_GROUP * g_sel
    el = jnp.where((lanef >= lo) & (lanef < lo + MOE_PER_GROUP), lg, ninf)
    emax = jnp.max(el, axis=1, keepdims=True)
    esum = jnp.sum(jnp.exp(el - emax), axis=1, keepdims=True)
    i1 = jnp.min(jnp.where(el == emax, lanef, big), axis=1, keepdims=True)
    el2 = jnp.where(lanef == i1, ninf, el)
    m2 = jnp.max(el2, axis=1, keepdims=True)
    i2 = jnp.min(jnp.where(el2 == m2, lanef, big), axis=1, keepdims=True)
    p1 = 1.0 / esum
    p2 = jnp.exp(m2 - emax) / esum
    tot = p1 + p2
    w1 = g_top * (p1 / tot)
    w2 = g_top * (p2 / tot)
    hit1 = lanef == i1
    hit2 = lanef == i2
    onehot = jnp.where(hit1, 1.0, 0.0) + jnp.where(hit2, 1.0, 0.0)
    ti = lax.broadcasted_iota(jnp.int32, (ROW_TILE, ROW_TILE), 0)
    tj = lax.broadcasted_iota(jnp.int32, (ROW_TILE, ROW_TILE), 1)
    earlier = jnp.where(tj < ti, 1.0, 0.0).astype(BF16)

    @pl.when(pl.program_id(0) == 0)
    def _():
        cnt_ref[...] = jnp.zeros(cnt_ref.shape, F32)

    before = _dot(earlier, onehot.astype(BF16)) + cnt_ref[0:1, :]
    k1 = jnp.sum(jnp.where(hit1, before, 0.0), axis=1, keepdims=True)
    k2 = jnp.sum(jnp.where(hit2, before, 0.0), axis=1, keepdims=True)
    cnt_ref[0:1, :] = cnt_ref[0:1, :] + jnp.sum(onehot, axis=0, keepdims=True)
    vals = (i1 - MOE_GROUPS, i2 - MOE_GROUPS, w1, w2, k1, k2)
    slab = jnp.zeros(lg.shape, F32)
    for col, v in enumerate(vals):
        slab = jnp.where(lane == col, v, slab)
    r_ref[...] = slab


def _post(rt, x, ma, mb, cb, w_out, mods, g2, wr, br):
    d = x.shape[1]
    hd = d // 2
    return pl.pallas_call(
        _post_kernel,
        out_shape=[jax.ShapeDtypeStruct((rt.rows, d), F32), jax.ShapeDtypeStruct((rt.rows, d), F32),
                   jax.ShapeDtypeStruct((rt.rows, LANES), F32), jax.ShapeDtypeStruct((8, LANES), F32)],
        grid=(rt.n_tiles,),
        in_specs=[pl.BlockSpec((ROW_TILE, d), lambda i: (i, 0)),
                  pl.BlockSpec((ROW_TILE, hd), lambda i: (i, 0)),
                  pl.BlockSpec((ROW_TILE, hd), lambda i: (i, cb)),
                  pl.BlockSpec((d, d), lambda i: (0, 0)),
                  pl.BlockSpec((1, 8, d), lambda i: (rt.mod_idx(i), 0, 0)),
                  pl.BlockSpec((1, d), lambda i: (0, 0)),
                  pl.BlockSpec((d, 2 * LANES), lambda i: (0, 0)),
                  pl.BlockSpec((1, LANES), lambda i: (0, 0))],
        out_specs=[pl.BlockSpec((ROW_TILE, d), lambda i: (i, 0)),
                   pl.BlockSpec((ROW_TILE, d), lambda i: (i, 0)),
                   pl.BlockSpec((ROW_TILE, LANES), lambda i: (i, 0)),
                   pl.BlockSpec((8, LANES), lambda i: (0, 0))],
        compiler_params=_cparams("arbitrary"), name="post",
    )(x, ma, mb, w_out, mods, g2.reshape(1, d), wr, br)


def _moe_plan(route, cnt):
    mb = MOE_ROWS
    t_count = route.shape[0]
    nb = (2 * t_count + MOE_EXPERTS * (mb - 1) + mb - 1) // mb
    counts = cnt[0, MOE_GROUPS:MOE_GROUPS + MOE_EXPERTS].astype(jnp.int32)
    padded = (counts + mb - 1) // mb * mb
    pend = jnp.cumsum(padded)
    pstart = pend - padded
    experts = jnp.arange(MOE_EXPERTS, dtype=jnp.int32)
    e = route[:, 0:2].astype(jnp.int32)
    first = jnp.sum(jnp.where(e[:, :, None] == experts[None, None, :], pstart[None, None, :], 0), axis=-1)
    dest = (first + route[:, 4:6].astype(jnp.int32)).reshape(-1)
    blk0 = jnp.arange(nb, dtype=jnp.int32) * mb
    block_e = jnp.minimum(jnp.sum((blk0[:, None] >= pend[None, :]).astype(jnp.int32), axis=1), MOE_EXPERTS - 1)
    sel = block_e[:, None] == experts[None, :]
    used = blk0 - jnp.sum(jnp.where(sel, pstart[None, :], 0), axis=1)
    n_valid = jnp.clip(jnp.sum(jnp.where(sel, counts[None, :], 0), axis=1) - used, 0, mb).astype(jnp.int32)
    fill = jnp.concatenate([pstart + counts, padded - counts, pend[-1:], nb - pend[-1:] // mb])
    return dest, block_e, n_valid, fill.astype(jnp.int32), nb


def _dispatch_kernel(dest_ref, fill_ref, h_ref, xs_hbm, stage, zbuf, sem, zsems, *, nt):
    i = pl.program_id(0)
    slot = i % 2
    zsem = zsems.at[0]

    def wait_tile(sl):
        for _ in range(2):
            pltpu.make_async_copy(stage.at[sl], xs_hbm.at[pl.ds(0, ROW_TILE)], sem.at[sl]).wait()

    def zero_padding(wait):
        def go(cp):
            cp.wait() if wait else cp.start()

        def one_row(r):
            go(pltpu.make_async_copy(zbuf.at[pl.ds(0, 1)], xs_hbm.at[pl.ds(r, 1)], zsem))

        def per_expert(e, c):
            start = fill_ref[e]
            n = fill_ref[MOE_EXPERTS + e]
            head = jnp.minimum((SUBLANES - start % SUBLANES) % SUBLANES, n)
            mid = pl.multiple_of((n - head) // SUBLANES * SUBLANES, SUBLANES)
            lax.fori_loop(0, head, lambda j, c2: (one_row(start + j), c2)[1], 0)

            @pl.when(mid > 0)
            def _():
                at = pl.multiple_of(start + head, SUBLANES)
                go(pltpu.make_async_copy(zbuf.at[pl.ds(0, mid)], xs_hbm.at[pl.ds(at, mid)], zsem))

            lax.fori_loop(0, n - head - mid, lambda j, c2: (one_row(start + head + mid + j), c2)[1], 0)
            return c
        lax.fori_loop(0, MOE_EXPERTS, per_expert, 0)

        def per_block(j, c):
            at = pl.multiple_of(fill_ref[2 * MOE_EXPERTS] + j * MOE_ROWS, MOE_ROWS)
            go(pltpu.make_async_copy(zbuf, xs_hbm.at[pl.ds(at, MOE_ROWS)], zsem))
            return c
        lax.fori_loop(0, fill_ref[2 * MOE_EXPERTS + 1], per_block, 0)

    @pl.when(i == 0)
    def _():
        zbuf[...] = jnp.zeros(zbuf.shape, zbuf.dtype)
        zero_padding(False)

    @pl.when(i >= 2)
    def _():
        wait_tile(slot)

    stage[slot] = h_ref[...]

    for j in range(ROW_TILE):
        src = stage.at[slot, pl.ds(j, 1)]
        for k in range(2):
            row = dest_ref[2 * (i * ROW_TILE + j) + k]
            pltpu.make_async_copy(src, xs_hbm.at[pl.ds(row, 1)], sem.at[slot]).start()

    @pl.when(i == nt - 1)
    def _():
        wait_tile(slot)
        if nt > 1:
            wait_tile(1 - slot)
        zero_padding(True)


def _dispatch(h, dest, fill, n_rows):
    t_count, d = h.shape
    nt = t_count // ROW_TILE
    grid_spec = pltpu.PrefetchScalarGridSpec(
        num_scalar_prefetch=2, grid=(nt,),
        in_specs=[pl.BlockSpec((ROW_TILE, d), lambda i, de, fi: (i, 0))],
        out_specs=pl.BlockSpec(memory_space=pl.ANY),
        scratch_shapes=[pltpu.VMEM((2, ROW_TILE, d), F32), pltpu.VMEM((MOE_ROWS, d), F32),
                        pltpu.SemaphoreType.DMA((2,)), pltpu.SemaphoreType.DMA((1,))])
    return pl.pallas_call(
        functools.partial(_dispatch_kernel, nt=nt),
        out_shape=jax.ShapeDtypeStruct((n_rows, d), F32),
        grid_spec=grid_spec,
        compiler_params=_cparams("arbitrary"), name="dispatch",
    )(dest, fill, h)


def _expert_kernel(be_ref, nv_ref, x_ref, wg_ref, wu_ref, wd_ref, y_ref, wgb, wub, wdb):
    i = pl.program_id(0)

    @pl.when(nv_ref[i] > 0)
    def _():
        @pl.when((i == 0) | (be_ref[i] != be_ref[jnp.maximum(i - 1, 0)]))
        def _():
            wgb[...] = wg_ref[0, 0].astype(BF16)
            wub[...] = wu_ref[0, 0].astype(BF16)
            wdb[...] = wd_ref[0, 0].astype(BF16)

        x = x_ref[...].astype(BF16)
        a = (jax.nn.silu(_dot(x, wgb[...])) * _dot(x, wub[...])).astype(BF16)
        y_ref[...] = _dot(a, wdb[...])

    @pl.when(nv_ref[i] == 0)
    def _():
        y_ref[...] = jnp.zeros(y_ref.shape, y_ref.dtype)


def _experts(xs, block_e, n_valid, layer, wg, wu, wd):
    n_rows, d = xs.shape
    mb = MOE_ROWS
    hid = wg.shape[3]

    def wspec(shape):
        return pl.BlockSpec(shape, lambda i, be, nv: (layer, be[i], 0, 0))

    grid_spec = pltpu.PrefetchScalarGridSpec(
        num_scalar_prefetch=2, grid=(n_rows // mb,),
        in_specs=[pl.BlockSpec((mb, d), lambda i, be, nv: (i, 0)),
                  wspec((1, 1, d, hid)), wspec((1, 1, d, hid)), wspec((1, 1, hid, d))],
        out_specs=pl.BlockSpec((mb, d), lambda i, be, nv: (i, 0)),
        scratch_shapes=[pltpu.VMEM((d, hid), BF16), pltpu.VMEM((d, hid), BF16), pltpu.VMEM((hid, d), BF16)])
    return pl.pallas_call(
        _expert_kernel,
        out_shape=jax.ShapeDtypeStruct((n_rows, d), F32),
        grid_spec=grid_spec,
        compiler_params=_cparams("arbitrary"), name="experts",
    )(block_e, n_valid, xs, wg, wu, wd)


def _final_kernel(dest_ref, x_ref, r_ref, modp_ref, ys_hbm, g_ref, o_ref, ybuf, sem, *, tile_of):
    x, finish = _moe_update(dest_ref, x_ref, r_ref, modp_ref, ys_hbm, ybuf, sem, tile_of, True)
    o_ref[...] = _rms(x, g_ref[...])
    finish()


def _final(rt, x, moe, final_g, n_len):
    d = x.shape[1]
    lt = n_len // ROW_TILE

    def tile_of(i):
        return (i // lt) * rt.tpb + rt.ctx_tiles + i % lt

    mspecs, margs, scratch = _moe_operands(rt, d, moe, tile_of)
    grid_spec = pltpu.PrefetchScalarGridSpec(
        num_scalar_prefetch=1, grid=(rt.b * lt,),
        in_specs=[pl.BlockSpec((ROW_TILE, d), lambda i, de: (tile_of(i), 0))] + mspecs + [
            pl.BlockSpec((1, d), lambda i, de: (0, 0))],
        out_specs=pl.BlockSpec((ROW_TILE, d), lambda i, de: (i, 0)),
        scratch_shapes=scratch)
    return pl.pallas_call(
        functools.partial(_final_kernel, tile_of=tile_of),
        out_shape=jax.ShapeDtypeStruct((rt.b * n_len, d), F32),
        grid_spec=grid_spec,
        compiler_params=_cparams("arbitrary"), name="final",
    )(moe[3], x, *margs, final_g.reshape(1, d))


def _pre_mla_kernel(dest_ref, x_ref, r_ref, modp_ref, ys_hbm, mod_ref, g_ref, win_ref, qg_ref, kvg_ref,
                    wq_ref, wqs_ref, wk_ref, wv_ref, vone_ref, ct_ref, st_ref,
                    xo_ref, q_ref, k_ref, v_ref, ybuf, sem):
    x, finish = _moe_update(dest_ref, x_ref, r_ref, modp_ref, ys_hbm, ybuf, sem, lambda t: t, True)
    xo_ref[...] = x
    h = _modulated(x, g_ref, mod_ref, 0)
    p = _dot(h.astype(BF16), win_ref[...])
    cq = _rms(p[:, 0:MLA_Q_RANK], qg_ref[...]).astype(BF16)
    ckv = _rms(p[:, MLA_Q_RANK:MLA_Q_RANK + MLA_KV_RANK], kvg_ref[...]).astype(BF16)
    off = MLA_Q_RANK + MLA_KV_RANK
    ct, st = ct_ref[...], st_ref[...]
    k_rope = p[:, off:off + HEAD_PAD] * ct + p[:, off + HEAD_PAD:off + 2 * HEAD_PAD] * st
    qscale = MLA_SCALE * LOG2_E
    for c0 in range(0, MLA_HEADS * HEAD_PAD, MLA_CHUNK):
        cols = slice(c0, c0 + MLA_CHUNK)
        qa = _dot(cq, wq_ref[:, cols])
        qb = _dot(cq, wqs_ref[:, cols])
        kn = _dot(ckv, wk_ref[:, cols])
        v_ref[:, cols] = (_dot(ckv, wv_ref[:, cols]) + vone_ref[:, cols]).astype(v_ref.dtype)
        for h0 in range(0, MLA_CHUNK, HEAD_PAD):
            sl = slice(h0, h0 + HEAD_PAD)
            out = slice(c0 + h0, c0 + h0 + HEAD_PAD)
            q_ref[:, out] = ((qa[:, sl] * ct + qb[:, sl] * st) * qscale).astype(q_ref.dtype)
            k_ref[:, out] = (kn[:, sl] + k_rope).astype(k_ref.dtype)
    finish()


def _pre_mla(rt, x, moe, mods, g1, wts, ct, st):
    d = x.shape[1]
    w_in, qg, kvg, wq, wqs, wk, wv, vone = wts

    def full(a):
        return pl.BlockSpec(a.shape, lambda i, de: (0,) * a.ndim)

    hq = MLA_HEADS * HEAD_PAD
    mspecs, margs, scratch = _moe_operands(rt, d, moe, lambda t: t)
    grid_spec = pltpu.PrefetchScalarGridSpec(
        num_scalar_prefetch=1, grid=(rt.n_tiles,),
        in_specs=[pl.BlockSpec((ROW_TILE, d), lambda i, de: (i, 0))] + mspecs + [
            pl.BlockSpec((1, 8, d), lambda i, de: (rt.mod_idx(i), 0, 0)),
            pl.BlockSpec((1, d), lambda i, de: (0, 0)),
            full(w_in), full(qg), full(kvg), full(wq), full(wqs), full(wk), full(wv), full(vone),
            pl.BlockSpec((ROW_TILE, HEAD_PAD), lambda i, de: (rt.pos_idx(i), 0)),
            pl.BlockSpec((ROW_TILE, HEAD_PAD), lambda i, de: (rt.pos_idx(i), 0))],
        out_specs=[pl.BlockSpec((ROW_TILE, d), lambda i, de: (i, 0)),
                   pl.BlockSpec((ROW_TILE, hq), lambda i, de: (i, 0)),
                   pl.BlockSpec((ROW_TILE, hq), lambda i, de: (i, 0)),
                   pl.BlockSpec((ROW_TILE, hq), lambda i, de: (i, 0))],
        scratch_shapes=scratch)
    return pl.pallas_call(
        _pre_mla_kernel,
        out_shape=[jax.ShapeDtypeStruct((rt.rows, d), F32), jax.ShapeDtypeStruct((rt.rows, hq), BF16),
                   jax.ShapeDtypeStruct((rt.rows, hq), BF16), jax.ShapeDtypeStruct((rt.rows, hq), BF16)],
        grid_spec=grid_spec,
        compiler_params=_cparams("arbitrary"), name="pre_mla",
    )(moe[3], x, *margs, mods, g1.reshape(1, d), w_in, qg, kvg, wq, wqs, wk, wv, vone, ct, st)


def _mla_params(w_in, q_g, kv_g, w_uq, w_ukv):
    d = w_in.shape[0]
    hp, hr = HEAD_PAD, MLA_ROPE // 2
    nq = MLA_NOPE + MLA_ROPE
    kr = w_in[:, MLA_Q_RANK + MLA_KV_RANK:]
    z = jnp.zeros((d, MLA_NOPE), F32)
    zt = jnp.zeros((d, hp - nq), F32)
    kr_a = jnp.concatenate([z, kr, zt], axis=1)
    kr_b = jnp.concatenate([z, -kr[:, hr:], kr[:, :hr], zt], axis=1)
    w_in_p = jnp.concatenate([w_in[:, :MLA_Q_RANK + MLA_KV_RANK], kr_a, kr_b], axis=1).astype(BF16)
    wq = w_uq.reshape(MLA_Q_RANK, MLA_HEADS, nq)
    zq = jnp.zeros((MLA_Q_RANK, MLA_HEADS, hp - nq), F32)
    wq_a = jnp.concatenate([wq, zq], axis=2).reshape(MLA_Q_RANK, MLA_HEADS * hp).astype(BF16)
    wq_b = jnp.concatenate([jnp.zeros_like(wq[:, :, :MLA_NOPE]), -wq[:, :, MLA_NOPE + hr:],
                            wq[:, :, MLA_NOPE:MLA_NOPE + hr], zq], axis=2)
    wq_b = wq_b.reshape(MLA_Q_RANK, MLA_HEADS * hp).astype(BF16)
    wkv = w_ukv.reshape(MLA_KV_RANK, MLA_HEADS, MLA_NOPE + MLA_V)
    wk = jnp.concatenate([wkv[:, :, :MLA_NOPE], jnp.zeros((MLA_KV_RANK, MLA_HEADS, hp - MLA_NOPE), F32)], axis=2)
    wk = wk.reshape(MLA_KV_RANK, MLA_HEADS * hp).astype(BF16)
    wv = wkv[:, :, MLA_NOPE:].reshape(MLA_KV_RANK, MLA_HEADS // 2, 2, MLA_V)
    zv = jnp.zeros((MLA_KV_RANK, MLA_HEADS // 2, hp - MLA_V), F32)
    wv = jnp.concatenate([wv[:, :, 0], zv, zv, wv[:, :, 1]], axis=2).reshape(MLA_KV_RANK, MLA_HEADS * hp)
    lane = jnp.arange(2 * hp) % (2 * hp)
    vone = jnp.tile(jnp.where((lane == ATT_DEN_EVEN) | (lane == hp + ATT_DEN_ODD), 1.0, 0.0), MLA_HEADS // 2)
    return (w_in_p, q_g.reshape(1, -1), kv_g.reshape(1, -1), wq_a, wq_b, wk, wv.astype(BF16),
            vone.reshape(1, -1).astype(F32))


def _attn_kernel(q_ref, k_ref, v_ref, o_ref, *, s_len, l_len):
    t = pl.program_id(2)
    lane = lax.broadcasted_iota(jnp.int32, (ATT_TQ, 2 * MLA_V), 1)

    def attend(nk):
        for pair in range(ATT_HEADS // 2):
            outs = []
            for j, den_lane in ((2 * pair, ATT_DEN_EVEN), (2 * pair + 1, ATT_DEN_ODD)):
                blk = slice(j * HEAD_PAD, (j + 1) * HEAD_PAD)
                s = _dot_nt(q_ref[0, :, blk], k_ref[0, 0:nk, blk])
                p = jnp.exp2(s - jnp.max(s, axis=1, keepdims=True))
                o = _dot(p.astype(BF16), v_ref[0, 0:nk, blk])
                outs.append(o / o[:, den_lane:den_lane + 1])
            o_ref[0, :, pair * 2 * MLA_V:(pair + 1) * 2 * MLA_V] = (
                jnp.where(lane < MLA_V, outs[0], outs[1]).astype(o_ref.dtype))

    ctx_tiles = l_len // ATT_TQ

    @pl.when(t < ctx_tiles)
    def _():
        attend(l_len)

    @pl.when(t >= ctx_tiles)
    def _():
        attend(s_len)


def _attention(q3, k3, v3, l_len):
    b, s, _ = q3.shape
    hq = ATT_HEADS * HEAD_PAD
    hv = ATT_HEADS * MLA_V
    return pl.pallas_call(
        functools.partial(_attn_kernel, s_len=s, l_len=l_len),
        out_shape=jax.ShapeDtypeStruct((b, s, MLA_HEADS * MLA_V), BF16),
        grid=(b, MLA_HEADS // ATT_HEADS, s // ATT_TQ),
        in_specs=[pl.BlockSpec((1, ATT_TQ, hq), lambda bi, hi, ti: (bi, ti, hi)),
                  pl.BlockSpec((1, s, hq), lambda bi, hi, ti: (bi, 0, hi)),
                  pl.BlockSpec((1, s, hq), lambda bi, hi, ti: (bi, 0, hi))],
        out_specs=pl.BlockSpec((1, ATT_TQ, hv), lambda bi, hi, ti: (bi, ti, hi)),
        compiler_params=_cparams("arbitrary", "arbitrary", "arbitrary"), name="attention",
    )(q3, k3, v3)


def _ret_tables(n, l):
    t = np.arange(n, dtype=np.float64)
    inv = ROPE_BASE ** (-np.arange(0, RET_DK, 2, dtype=np.float64) / RET_DK)
    ang = t[:, None] * inv[None, :]
    cos, sin = np.cos(ang), np.sin(ang)
    cos2 = np.concatenate([np.ones((l, RET_DK)), np.concatenate([cos, cos], axis=1)], axis=0)
    sin2 = np.concatenate([np.zeros((l, RET_DK)), np.concatenate([-sin, sin], axis=1)], axis=0)
    return jnp.asarray(cos2, F32), jnp.asarray(sin2, F32)


def _mla_tables(n, l):
    rows = n // GRID_W
    r_pos = np.repeat(np.arange(rows, dtype=np.float64), GRID_W)
    c_pos = np.tile(np.arange(GRID_W, dtype=np.float64), rows)
    ax = MLA_ROPE // 2
    inv = ROPE_BASE ** (-np.arange(0, ax, 2, dtype=np.float64) / ax)
    ang = np.concatenate([r_pos[:, None] * inv[None, :], c_pos[:, None] * inv[None, :]], axis=-1)
    cos, sin = np.cos(ang), np.sin(ang)
    pad = HEAD_PAD - MLA_NOPE - MLA_ROPE
    ct_l = np.concatenate([np.ones((n, MLA_NOPE)), cos, cos, np.zeros((n, pad))], axis=1)
    st_l = np.concatenate([np.zeros((n, MLA_NOPE)), sin, sin, np.zeros((n, pad))], axis=1)
    ct_c = np.concatenate([np.ones((l, MLA_NOPE + MLA_ROPE)), np.zeros((l, pad))], axis=1)
    ct = np.concatenate([ct_c, ct_l], axis=0)
    st = np.concatenate([np.zeros((l, HEAD_PAD)), st_l], axis=0)
    return jnp.asarray(ct, F32), jnp.asarray(st, F32)


def kernel(x, c, ctx, c_ctx, ada_w, ada_b, norm_g, ab_w_in, ab_w_out, ret_decay_logit, lru_conv_w, lru_conv_b, lru_gate_w, lru_gate_b, lru_lambda, mla_w_in, mla_q_norm_g, mla_kv_norm_g, mla_w_uq, mla_w_ukv, mla_w_out, moe_group_w, moe_group_b, moe_expert_w, moe_expert_b, moe_w_gate, moe_w_up, moe_w_down, final_norm_g):
    b, n, d = x.shape
    l = ctx.shape[1]
    s = l + n
    depth = ada_w.shape[0]
    rt = _Rows(b, s, l)

    nrow = (b + 1 + 7) // 8 * 8
    cvec = jnp.concatenate([c, c_ctx[None, :], jnp.zeros((nrow - b - 1, d), F32)], axis=0)
    ada = _ada_all(cvec, ada_w, ada_b)

    def layer_mods(layer):
        lat = ada[layer, :b].reshape(b, 1, 6, d)
        cx = jnp.broadcast_to(ada[layer, b].reshape(1, 1, 6, d), (b, 1, 6, d))
        m = jnp.concatenate([cx, lat], axis=1)
        m = jnp.concatenate([m, jnp.zeros((b, 2, 2, d), F32)], axis=2)
        return m.reshape(2 * b, 8, d)

    cos2, sin2 = _ret_tables(n, l)
    ct, st = _mla_tables(n, l)

    xs = jnp.concatenate([ctx, x], axis=1).reshape(b * s, d)
    out = None
    moe = None
    for layer in range(depth):
        mods = layer_mods(layer)
        i = layer // 2
        if layer % 2 == 0:
            xs, p = _pre_ab(rt, xs, moe, mods, norm_g[layer, 0], ab_w_in[i].astype(BF16))
            p3 = p.reshape(b, s, -1)
            lg = jax.nn.log_sigmoid(ret_decay_logit[i].astype(F32))
            lgv = jnp.broadcast_to(lg.T[:, :, None], (RET_HEADS, 2, LANES))
            lgv = jnp.concatenate([lgv, jnp.zeros((RET_HEADS, 6, LANES), F32)], axis=1)
            ma = _retention(p3, cos2, sin2, lgv, l).reshape(b * s, -1)
            mb = _rglru(p3, *_lru_params(lru_conv_w[i], lru_conv_b[i], lru_gate_w[i], lru_gate_b[i],
                                         lru_lambda[i]), l).reshape(b * s, -1)
            cb = 0
            w_out = ab_w_out[i].astype(BF16)
        else:
            wts = _mla_params(mla_w_in[i], mla_q_norm_g[i], mla_kv_norm_g[i], mla_w_uq[i], mla_w_ukv[i])
            xs, q, k, v = _pre_mla(rt, xs, moe, mods, norm_g[layer, 0], wts, ct, st)
            att = _attention(q.reshape(b, s, -1), k.reshape(b, s, -1), v.reshape(b, s, -1), l)
            ma = mb = att.reshape(b * s, -1)
            cb = 1
            w_out = mla_w_out[i].astype(BF16)
        wr = jnp.concatenate([moe_group_w[layer], moe_expert_w[layer],
                              jnp.zeros((d, LANES - MOE_GROUPS - MOE_EXPERTS), F32)], axis=1)
        wr_hi = wr.astype(BF16)
        wr = jnp.concatenate([wr_hi, (wr - wr_hi.astype(F32)).astype(BF16)], axis=1)
        br = jnp.concatenate([moe_group_b[layer], moe_expert_b[layer],
                              jnp.zeros((LANES - MOE_GROUPS - MOE_EXPERTS,), F32)]).reshape(1, LANES)
        xs, h2, route, cnt = _post(rt, xs, ma, mb, cb, w_out, mods, norm_g[layer, 1], wr, br)
        dest, block_e, n_valid, fill, nb = _moe_plan(route, cnt)
        xsort = _dispatch(h2, dest, fill, nb * MOE_ROWS)
        ys = _experts(xsort, block_e, n_valid, layer, moe_w_gate, moe_w_up, moe_w_down)
        moe = (route, mods, ys, dest)
    out = _final(rt, xs, moe, final_norm_g, n)
    return out.reshape(b, n, d)
```

```python
import functools

import jax
import jax.numpy as jnp
import numpy as np
from jax import lax
from jax.experimental import pallas as pl
from jax.experimental.pallas import tpu as pltpu

F32 = jnp.float32
BF16 = jnp.bfloat16

EPS = 1e-6
ROPE_BASE = 10000.0
GRID_W = 64

RET_HEADS = 4
RET_DK = 128
RET_CHUNK = 128
RET_HPS = 2
LRU_WIDTH = 512
LRU_BLOCK = 64
LRU_C = 8.0
LRU_HALF = 256
LRU_TILE = 128
LRU_SUB = 8

MLA_HEADS = 16
MLA_NOPE = 64
MLA_ROPE = 32
MLA_V = 64
MLA_Q_RANK = 384
MLA_KV_RANK = 256
MLA_SCALE = (MLA_NOPE + MLA_ROPE) ** -0.5
LOG2_E = 1.4426950408889634
HEAD_PAD = 128
MLA_CHUNK = 512

MOE_GROUPS = 4
MOE_PER_GROUP = 8
MOE_EXPERTS = 32
MOE_ROWS = 512

ROW_TILE = 256
ATT_TQ = 256
ATT_HEADS = 4
ATT_DEN_EVEN = 64
ATT_DEN_ODD = 0
LANES = 128
SUBLANES = 8
VMEM_LIMIT = 56 * 1024 * 1024


def _cparams(*sem):
    return pltpu.CompilerParams(dimension_semantics=sem, vmem_limit_bytes=VMEM_LIMIT)


def _rms(x, g):
    return x * lax.rsqrt(jnp.mean(x * x, axis=-1, keepdims=True) + EPS) * g


def _dot(a, b):
    return jnp.dot(a, b, preferred_element_type=F32)


def _dot_nt(a, b):
    return lax.dot_general(a, b, (((1,), (1,)), ((), ())), preferred_element_type=F32)


def _dot_tn(a, b):
    return lax.dot_general(a, b, (((0,), (0,)), ((), ())), preferred_element_type=F32)


def _ada_kernel(s_ref, w_ref, b_ref, o_ref):
    s = jax.nn.silu(s_ref[...])
    o_ref[0] = _dot(s.astype(BF16), w_ref[0].astype(BF16)) + b_ref[0]


def _ada_all(cvec, ada_w, ada_b):
    depth, d, n6 = ada_w.shape
    rows = cvec.shape[0]
    tn = n6 // 4
    return pl.pallas_call(
        _ada_kernel,
        out_shape=jax.ShapeDtypeStruct((depth, rows, n6), F32),
        grid=(depth, n6 // tn),
        in_specs=[pl.BlockSpec((rows, d), lambda l, j: (0, 0)),
                  pl.BlockSpec((1, d, tn), lambda l, j: (l, 0, j)),
                  pl.BlockSpec((1, 1, tn), lambda l, j: (l, 0, j))],
        out_specs=pl.BlockSpec((1, rows, tn), lambda l, j: (l, 0, j)),
        compiler_params=_cparams("arbitrary", "arbitrary"),
        name="adaln",
    )(cvec, ada_w, ada_b.reshape(depth, 1, n6))


class _Rows:
    def __init__(self, b, s, l):
        assert s % ROW_TILE == 0 and l % ROW_TILE == 0
        self.b, self.s, self.l = b, s, l
        self.tpb = s // ROW_TILE
        self.ctx_tiles = l // ROW_TILE
        self.n_tiles = b * self.tpb
        self.rows = b * s

    def mod_idx(self, i):
        return 2 * (i // self.tpb) + jnp.where(i % self.tpb >= self.ctx_tiles, 1, 0)

    def pos_idx(self, i):
        return i % self.tpb


def _modulated(x, g_ref, mod_ref, base):
    h = _rms(x, g_ref[...])
    return h * (1.0 + mod_ref[0, base + 1:base + 2, :]) + mod_ref[0, base:base + 1, :]


def _moe_update(dest_ref, x_ref, r_ref, modp_ref, ys_hbm, ybuf, sem, tile_of, inline, two_queues=False):
    i = pl.program_id(0)
    n = pl.num_programs(0)
    slot = i % 2

    def start(tok, j, sl):
        for k in range(2):
            pltpu.make_async_copy(ys_hbm.at[pl.ds(dest_ref[2 * tok + k], 1)], ybuf.at[sl, k, pl.ds(j, 1)],
                                  sem.at[sl]).start(priority=k if two_queues else 0)

    def issue_loop(step, sl):
        base = tile_of(step) * ROW_TILE
        lax.fori_loop(0, ROW_TILE, lambda j, c: (start(base + j, j, sl), c)[1], 0, unroll=8)

    def wait(sl):
        for k in range(2):
            pltpu.make_async_copy(ys_hbm.at[pl.ds(0, ROW_TILE)], ybuf.at[sl, k], sem.at[sl]).wait()

    @pl.when(i == 0)
    def _():
        issue_loop(0, 0)

    if not inline:
        @pl.when(i + 1 < n)
        def _():
            issue_loop(i + 1, 1 - slot)

    wait(slot)
    r = r_ref[...]
    y = ybuf[slot, 0] * r[:, 2:3] + ybuf[slot, 1] * r[:, 3:4]
    x = x_ref[...] + modp_ref[0, 5:6, :] * y
    if not inline:
        return x, None
    base = tile_of(jnp.minimum(i + 1, n - 1)) * ROW_TILE
    for j in range(ROW_TILE):
        start(base + j, j, 1 - slot)

    def finish():
        @pl.when(i == n - 1)
        def _():
            wait(1 - slot)
    return x, finish


def _moe_operands(rt, d, moe, tile_of):
    route, modp, ys, _ = moe
    specs = [pl.BlockSpec((ROW_TILE, LANES), lambda i, de: (tile_of(i), 0)),
             pl.BlockSpec((1, 8, d), lambda i, de: (rt.mod_idx(tile_of(i)), 0, 0)),
             pl.BlockSpec(memory_space=pl.ANY)]
    scratch = [pltpu.VMEM((2, 2, ROW_TILE, d), F32), pltpu.SemaphoreType.DMA((2,))]
    return specs, [route, modp, ys], scratch


def _pre_ab_kernel(*refs, has_moe):
    if has_moe:
        dest_ref, x_ref, r_ref, modp_ref, ys_hbm, mod_ref, g_ref, w_ref, xo_ref, p_ref, ybuf, sem = refs
        x, finish = _moe_update(dest_ref, x_ref, r_ref, modp_ref, ys_hbm, ybuf, sem, lambda t: t, True)
        xo_ref[...] = x
    else:
        x_ref, mod_ref, g_ref, w_ref, p_ref = refs
        x = x_ref[...]
    h = _modulated(x, g_ref, mod_ref, 0)
    p_ref[...] = _dot(h.astype(BF16), w_ref[...])
    if has_moe:
        finish()


def _pre_ab(rt, x, moe, mods, g1, w_in):
    d = x.shape[1]
    n_out = w_in.shape[1]
    if moe is None:
        p = pl.pallas_call(
            functools.partial(_pre_ab_kernel, has_moe=False),
            out_shape=jax.ShapeDtypeStruct((rt.rows, n_out), F32),
            grid=(rt.n_tiles,),
            in_specs=[pl.BlockSpec((ROW_TILE, d), lambda i: (i, 0)),
                      pl.BlockSpec((1, 8, d), lambda i: (rt.mod_idx(i), 0, 0)),
                      pl.BlockSpec((1, d), lambda i: (0, 0)),
                      pl.BlockSpec((d, n_out), lambda i: (0, 0))],
            out_specs=pl.BlockSpec((ROW_TILE, n_out), lambda i: (i, 0)),
            compiler_params=_cparams("arbitrary"), name="pre_ab",
        )(x, mods, g1.reshape(1, d), w_in)
        return x, p
    mspecs, margs, scratch = _moe_operands(rt, d, moe, lambda t: t)
    grid_spec = pltpu.PrefetchScalarGridSpec(
        num_scalar_prefetch=1, grid=(rt.n_tiles,),
        in_specs=[pl.BlockSpec((ROW_TILE, d), lambda i, de: (i, 0))] + mspecs + [
            pl.BlockSpec((1, 8, d), lambda i, de: (rt.mod_idx(i), 0, 0)),
            pl.BlockSpec((1, d), lambda i, de: (0, 0)),
            pl.BlockSpec((d, n_out), lambda i, de: (0, 0))],
        out_specs=[pl.BlockSpec((ROW_TILE, d), lambda i, de: (i, 0)),
                   pl.BlockSpec((ROW_TILE, n_out), lambda i, de: (i, 0))],
        scratch_shapes=scratch)
    return pl.pallas_call(
        functools.partial(_pre_ab_kernel, has_moe=True),
        out_shape=[jax.ShapeDtypeStruct((rt.rows, d), F32), jax.ShapeDtypeStruct((rt.rows, n_out), F32)],
        grid_spec=grid_spec,
        compiler_params=_cparams("arbitrary"), name="pre_ab_moe",
    )(moe[3], x, *margs, mods, g1.reshape(1, d), w_in)


def _ret_kernel(q_ref, k_ref, v_ref, g_ref, cos_ref, sin_ref, lg_ref, o_ref, qs, ks, acc, *, s_len, l_len):
    c, dk = RET_CHUNK, RET_DK
    nch, cch = s_len // c, l_len // c
    ii = lax.broadcasted_iota(jnp.int32, (c, c), 0).astype(F32)
    jj = lax.broadcasted_iota(jnp.int32, (c, c), 1).astype(F32)
    diff = ii - jj
    k_scale = RET_DK ** -0.5

    def head_consts(j):
        lgf = lg_ref[j, 0:1, :]
        lgb = lg_ref[j, 1:2, :]
        dmask = (jnp.where(diff > 0, jnp.exp(lgf * jnp.maximum(diff, 0.0)), 0.0)
                 + jnp.where(diff < 0, jnp.exp(lgb * jnp.maximum(-diff, 0.0)), 0.0)
                 + jnp.where(diff == 0, 2.0, 0.0))
        return dict(dmask=dmask,
                    zeta_f=jnp.exp(lgf * (c - 1.0 - ii)), xi_f=jnp.exp(lgf * (ii + 1.0)),
                    zeta_b=jnp.exp(lgb * ii), xi_b=jnp.exp(lgb * (c - ii)),
                    cd_f=jnp.exp(lgf * c), cd_b=jnp.exp(lgb * c))

    hc = [head_consts(j) for j in range(RET_HPS)]

    def fwd(n, sts):
        rows = pl.ds(pl.multiple_of(n * c, c), c)
        cs, sn = cos_ref[rows, :], sin_ref[rows, :]
        out = []
        for j, st in enumerate(sts):
            cols = slice(j * dk, (j + 1) * dk)
            q = q_ref[0, rows, cols]
            k = k_ref[0, rows, cols]
            v = v_ref[0, rows, cols]
            qb = (q * cs + pltpu.roll(q, 64, 1) * sn).astype(BF16)
            kb = ((k * cs + pltpu.roll(k, 64, 1) * sn) * k_scale).astype(BF16)
            qs[rows, cols] = qb
            ks[rows, cols] = kb
            sc = _dot_nt(qb, kb) * hc[j]["dmask"]
            acc[rows, cols] = _dot(sc.astype(BF16), v.astype(BF16)) + _dot(qb, st.astype(BF16)) * hc[j]["xi_f"]
            out.append(hc[j]["cd_f"] * st + _dot_tn(kb, (v * hc[j]["zeta_f"]).astype(BF16)))
        return tuple(out)

    def bwd(n, sts):
        rows = pl.ds(pl.multiple_of(n * c, c), c)
        out = []
        for j, st in enumerate(sts):
            cols = slice(j * dk, (j + 1) * dk)
            qb = qs[rows, cols]
            kb = ks[rows, cols]
            v = v_ref[0, rows, cols]
            y = acc[rows, cols] + _dot(qb, st.astype(BF16)) * hc[j]["xi_b"]
            y = y * lax.rsqrt(jnp.mean(y * y, axis=-1, keepdims=True) + EPS)
            o_ref[0, rows, cols] = (y * jax.nn.silu(g_ref[0, rows, cols])).astype(o_ref.dtype)
            out.append(hc[j]["cd_b"] * st + _dot_tn(kb, (v * hc[j]["zeta_b"]).astype(BF16)))
        return tuple(out)

    zero = tuple(jnp.zeros((c, c), F32) for _ in range(RET_HPS))
    assert nch % 2 == 0 and cch % 2 == 0
    unroll = 4 if (nch - cch) % 4 == 0 else 2
    sts = lax.fori_loop(0, cch, fwd, zero, unroll=2)
    lax.fori_loop(cch, nch, fwd, sts, unroll=unroll)
    sts = lax.fori_loop(0, cch, lambda t, sts: bwd(cch - 1 - t, sts), zero, unroll=2)
    lax.fori_loop(0, nch - cch, lambda t, sts: bwd(nch - 1 - t, sts), sts, unroll=unroll)


def _retention(p3, cos2, sin2, lgv, l_len):
    b, s, _ = p3.shape
    groups = RET_HEADS // RET_HPS
    w = RET_HPS * RET_DK

    def col(off, **kw):
        return pl.BlockSpec((1, s, w), lambda bi, hi: (bi, 0, off + hi), **kw)

    once = dict(pipeline_mode=pl.Buffered(1))
    return pl.pallas_call(
        functools.partial(_ret_kernel, s_len=s, l_len=l_len),
        out_shape=jax.ShapeDtypeStruct((b, s, RET_HEADS * RET_DK), BF16),
        grid=(b, groups),
        in_specs=[col(0), col(groups), col(2 * groups), col(3 * groups, **once),
                  pl.BlockSpec((s, RET_DK), lambda bi, hi: (0, 0), **once),
                  pl.BlockSpec((s, RET_DK), lambda bi, hi: (0, 0), **once),
                  pl.BlockSpec((RET_HPS, 8, LANES), lambda bi, hi: (hi, 0, 0))],
        out_specs=pl.BlockSpec((1, s, w), lambda bi, hi: (bi, 0, hi)),
        scratch_shapes=[pltpu.VMEM((s, w), BF16), pltpu.VMEM((s, w), BF16), pltpu.VMEM((s, w), F32)],
        compiler_params=_cparams("arbitrary", "arbitrary"), name="retention",
    )(p3, p3, p3, p3, cos2, sin2, lgv)


def _tile_scan(a, b, reverse):
    n = a.shape[0]
    rows = lax.broadcasted_iota(jnp.int32, a.shape, 0)
    step = 1
    while step < n:
        shift = n - step if reverse else step
        a_s = pltpu.roll(a, shift, 0)
        b_s = pltpu.roll(b, shift, 0)
        m = (rows < n - step) if reverse else (rows >= step)
        b = jnp.where(m, a * b_s + b, b)
        a = jnp.where(m, a * a_s, a)
        step *= 2
    return a, b


def _scan_rows(a, b, carry, reverse):
    n = a.shape[0]
    pieces = [None] * (n // LRU_SUB)
    for i in (reversed(range(len(pieces))) if reverse else range(len(pieces))):
        rows = slice(i * LRU_SUB, (i + 1) * LRU_SUB)
        a_c, h_loc = _tile_scan(a[rows], b[rows], reverse)
        pieces[i] = h_loc + a_c * carry
        carry = pieces[i][0:1] if reverse else pieces[i][LRU_SUB - 1:LRU_SUB]
    return jnp.concatenate(pieces, axis=0), carry


def _lru_kernel(x_ref, y_ref, cw_ref, wg_ref, gb_ref, sp_ref, o_ref, xpad, hf, ab, bb, *, s_len, l_len):
    tl, w = LRU_TILE, LRU_HALF
    assert s_len % tl == 0 and l_len % tl == 0
    ntl, ctl = s_len // tl, l_len // tl
    xpad[0:8, :] = jnp.zeros((8, w), F32)
    xpad[s_len + 8:s_len + 16, :] = jnp.zeros((8, w), F32)
    xpad[8:s_len + 8, :] = x_ref[0]
    w0, w1, w2, w3, cb = (cw_ref[0, t:t + 1, :] for t in range(5))
    sp_f = sp_ref[0, 0:1, :]
    sp_b = sp_ref[0, 1:2, :]

    def coeff(gr, gi, sp, xc):
        r = jax.nn.sigmoid(gr)
        i = jax.nn.sigmoid(gi)
        log_a = -LRU_C * r * sp
        th = jnp.tanh(log_a)
        return jnp.exp(log_a), jnp.sqrt(-2.0 * th / (1.0 - th)) * (i * xc)

    def fwd(n, carry):
        r0 = pl.multiple_of(n * tl, tl)
        win = xpad[pl.ds(r0, tl + 16), :]
        win = jnp.concatenate([jnp.where(r0 == l_len, 0.0, win[0:8]), win[8:tl + 8],
                               jnp.where(r0 + tl == l_len, 0.0, win[tl + 8:tl + 16])], axis=0)

        def tap(d):
            return pltpu.roll(win, (tl + 16 - d) % (tl + 16), 0)[8:8 + tl]

        xc = tap(-2) * w0
        xc = xc + tap(-1) * w1
        xc = xc + win[8:8 + tl] * w2
        xc = xc + tap(1) * w3
        xc = xc + cb
        gts = _dot(xc.astype(BF16), wg_ref[0]) + gb_ref[0]
        a_f, b_f = coeff(gts[:, 0:w], gts[:, w:2 * w], sp_f, xc)
        a_b, b_b = coeff(gts[:, 2 * w:3 * w], gts[:, 3 * w:4 * w], sp_b, xc)
        rows = pl.ds(r0, tl)
        ab[rows, :] = a_b
        bb[rows, :] = b_b
        hh, carry = _scan_rows(a_f, b_f, carry, False)
        hf[rows, :] = hh
        return carry

    def bwd(n, carry):
        rows = pl.ds(pl.multiple_of(n * tl, tl), tl)
        hh, carry = _scan_rows(ab[rows, :], bb[rows, :], carry, True)
        o_ref[0, rows, :] = ((hf[rows, :] + hh) * jax.nn.gelu(y_ref[0, rows, :])).astype(o_ref.dtype)
        return carry

    zero = jnp.zeros((1, w), F32)
    assert ntl % 2 == 0 and ctl % 2 == 0
    lax.fori_loop(0, ntl, fwd, zero, unroll=2)
    c = lax.fori_loop(0, ctl, lambda t, c: bwd(ctl - 1 - t, c), zero, unroll=2)
    lax.fori_loop(0, ntl - ctl, lambda t, c: bwd(ntl - 1 - t, c), c, unroll=2)


def _rglru(p3, conv_wb, gate_w, gate_b, sp, l_len):
    b, s, _ = p3.shape
    nh = LRU_WIDTH // LRU_HALF
    xoff = (4 * RET_HEADS * RET_DK) // LRU_HALF
    yoff = xoff + nh
    return pl.pallas_call(
        functools.partial(_lru_kernel, s_len=s, l_len=l_len),
        out_shape=jax.ShapeDtypeStruct((b, s, LRU_WIDTH), BF16),
        grid=(b, nh),
        in_specs=[pl.BlockSpec((1, s, LRU_HALF), lambda bi, j: (bi, 0, xoff + j)),
                  pl.BlockSpec((1, s, LRU_HALF), lambda bi, j: (bi, 0, yoff + j)),
                  pl.BlockSpec((1, 8, LRU_HALF), lambda bi, j: (j, 0, 0)),
                  pl.BlockSpec((1, LRU_HALF, 4 * LRU_HALF), lambda bi, j: (j, 0, 0)),
                  pl.BlockSpec((1, 1, 4 * LRU_HALF), lambda bi, j: (j, 0, 0)),
                  pl.BlockSpec((1, 8, LRU_HALF), lambda bi, j: (j, 0, 0))],
        out_specs=pl.BlockSpec((1, s, LRU_HALF), lambda bi, j: (bi, 0, j)),
        scratch_shapes=[pltpu.VMEM((s + 16, LRU_HALF), F32), pltpu.VMEM((s, LRU_HALF), F32),
                        pltpu.VMEM((s, LRU_HALF), F32), pltpu.VMEM((s, LRU_HALF), F32)],
        compiler_params=_cparams("arbitrary", "arbitrary"), name="rglru",
    )(p3, p3, conv_wb, gate_w, gate_b, sp)


def _lru_params(conv_w, conv_b, gate_w, gate_b, lam):
    nh = LRU_WIDTH // LRU_HALF
    bph = LRU_HALF // LRU_BLOCK
    cw = jnp.concatenate([conv_w, conv_b[None, :], jnp.zeros((3, LRU_WIDTH), F32)], axis=0)
    cw = cw.reshape(8, nh, LRU_HALF).transpose(1, 0, 2)
    eye = jnp.eye(bph, dtype=F32)
    gw = gate_w.reshape(2, 2, nh, bph, LRU_BLOCK, LRU_BLOCK)
    dense = jnp.einsum('dgjkio,kl->jkidglo', gw, eye)
    dense = dense.reshape(nh, LRU_HALF, 4 * LRU_HALF).astype(BF16)
    gb = gate_b.reshape(2, 2, nh, LRU_HALF).transpose(2, 0, 1, 3).reshape(nh, 1, 4 * LRU_HALF)
    sp = jax.nn.softplus(-lam.astype(F32)).reshape(2, nh, LRU_HALF).transpose(1, 0, 2)
    sp = jnp.concatenate([sp, jnp.zeros((nh, 6, LRU_HALF), F32)], axis=1)
    return cw, dense, gb, sp


def _post_kernel(x_ref, ma_ref, mb_ref, w_ref, mod_ref, g_ref, wr_ref, br_ref, xo_ref, h_ref, r_ref, cnt_ref):
    m = jnp.concatenate([ma_ref[...], mb_ref[...]], axis=1)
    o = _dot(m, w_ref[...])
    x = x_ref[...] + mod_ref[0, 2:3, :] * o
    xo_ref[...] = x
    h = _modulated(x, g_ref, mod_ref, 3)
    h_ref[...] = h
    h_hi = h.astype(BF16)
    h_lo = (h - h_hi.astype(F32)).astype(BF16)
    part = _dot(h_hi, wr_ref[...])
    lg = part[:, 0:LANES] + part[:, LANES:2 * LANES] + _dot(h_lo, wr_ref[:, 0:LANES]) + br_ref[...]
    lane = lax.broadcasted_iota(jnp.int32, lg.shape, 1)
    lanef = lane.astype(F32)
    ninf = -jnp.inf
    big = float(LANES)
    gl = jnp.where(lane < MOE_GROUPS, lg, ninf)
    gmax = jnp.max(gl, axis=1, keepdims=True)
    g_top = 1.0 / jnp.sum(jnp.exp(gl - gmax), axis=1, keepdims=True)
    g_sel = jnp.min(jnp.where(gl == gmax, lanef, big), axis=1, keepdims=True)
    lo = MOE_GROUPS + MOE_PER_GROUP * g_sel
    el = jnp.where((lanef >= lo) & (lanef < lo + MOE_PER_GROUP), lg, ninf)
    emax = jnp.max(el, axis=1, keepdims=True)
    esum = jnp.sum(jnp.exp(el - emax), axis=1, keepdims=True)
    i1 = jnp.min(jnp.where(el == emax, lanef, big), axis=1, keepdims=True)
    el2 = jnp.where(lanef == i1, ninf, el)
    m2 = jnp.max(el2, axis=1, keepdims=True)
    i2 = jnp.min(jnp.where(el2 == m2, lanef, big), axis=1, keepdims=True)
    p1 = 1.0 / esum
    p2 = jnp.exp(m2 - emax) / esum
    tot = p1 + p2
    w1 = g_top * (p1 / tot)
    w2 = g_top * (p2 / tot)
    hit1 = lanef == i1
    hit2 = lanef == i2
    onehot = jnp.where(hit1, 1.0, 0.0) + jnp.where(hit2, 1.0, 0.0)
    ti = lax.broadcasted_iota(jnp.int32, (ROW_TILE, ROW_TILE), 0)
    tj = lax.broadcasted_iota(jnp.int32, (ROW_TILE, ROW_TILE), 1)
    earlier = jnp.where(tj < ti, 1.0, 0.0).astype(BF16)

    @pl.when(pl.program_id(0) == 0)
    def _():
        cnt_ref[...] = jnp.zeros(cnt_ref.shape, F32)

    before = _dot(earlier, onehot.astype(BF16)) + cnt_ref[0:1, :]
    k1 = jnp.sum(jnp.where(hit1, before, 0.0), axis=1, keepdims=True)
    k2 = jnp.sum(jnp.where(hit2, before, 0.0), axis=1, keepdims=True)
    cnt_ref[0:1, :] = cnt_ref[0:1, :] + jnp.sum(onehot, axis=0, keepdims=True)
    vals = (i1 - MOE_GROUPS, i2 - MOE_GROUPS, w1, w2, k1, k2)
    slab = jnp.zeros(lg.shape, F32)
    for col, v in enumerate(vals):
        slab = jnp.where(lane == col, v, slab)
    r_ref[...] = slab


def _post(rt, x, ma, mb, cb, w_out, mods, g2, wr, br):
    d = x.shape[1]
    hd = d // 2
    return pl.pallas_call(
        _post_kernel,
        out_shape=[jax.ShapeDtypeStruct((rt.rows, d), F32), jax.ShapeDtypeStruct((rt.rows, d), F32),
                   jax.ShapeDtypeStruct((rt.rows, LANES), F32), jax.ShapeDtypeStruct((8, LANES), F32)],
        grid=(rt.n_tiles,),
        in_specs=[pl.BlockSpec((ROW_TILE, d), lambda i: (i, 0)),
                  pl.BlockSpec((ROW_TILE, hd), lambda i: (i, 0)),
                  pl.BlockSpec((ROW_TILE, hd), lambda i: (i, cb)),
                  pl.BlockSpec((d, d), lambda i: (0, 0)),
                  pl.BlockSpec((1, 8, d), lambda i: (rt.mod_idx(i), 0, 0)),
                  pl.BlockSpec((1, d), lambda i: (0, 0)),
                  pl.BlockSpec((d, 2 * LANES), lambda i: (0, 0)),
                  pl.BlockSpec((1, LANES), lambda i: (0, 0))],
        out_specs=[pl.BlockSpec((ROW_TILE, d), lambda i: (i, 0)),
                   pl.BlockSpec((ROW_TILE, d), lambda i: (i, 0)),
                   pl.BlockSpec((ROW_TILE, LANES), lambda i: (i, 0)),
                   pl.BlockSpec((8, LANES), lambda i: (0, 0))],
        compiler_params=_cparams("arbitrary"), name="post",
    )(x, ma, mb, w_out, mods, g2.reshape(1, d), wr, br)


def _moe_plan(route, cnt):
    mb = MOE_ROWS
    t_count = route.shape[0]
    nb = (2 * t_count + MOE_EXPERTS * (mb - 1) + mb - 1) // mb
    counts = cnt[0, MOE_GROUPS:MOE_GROUPS + MOE_EXPERTS].astype(jnp.int32)
    padded = (counts + mb - 1) // mb * mb
    pend = jnp.cumsum(padded)
    pstart = pend - padded
    experts = jnp.arange(MOE_EXPERTS, dtype=jnp.int32)
    e = route[:, 0:2].astype(jnp.int32)
    first = jnp.sum(jnp.where(e[:, :, None] == experts[None, None, :], pstart[None, None, :], 0), axis=-1)
    dest = (first + route[:, 4:6].astype(jnp.int32)).reshape(-1)
    blk0 = jnp.arange(nb, dtype=jnp.int32) * mb
    block_e = jnp.minimum(jnp.sum((blk0[:, None] >= pend[None, :]).astype(jnp.int32), axis=1), MOE_EXPERTS - 1)
    sel = block_e[:, None] == experts[None, :]
    used = blk0 - jnp.sum(jnp.where(sel, pstart[None, :], 0), axis=1)
    n_valid = jnp.clip(jnp.sum(jnp.where(sel, counts[None, :], 0), axis=1) - used, 0, mb).astype(jnp.int32)
    fill = jnp.concatenate([pstart + counts, padded - counts, pend[-1:], nb - pend[-1:] // mb])
    return dest, block_e, n_valid, fill.astype(jnp.int32), nb


def _dispatch_kernel(dest_ref, fill_ref, h_ref, xs_hbm, stage, zbuf, sem, zsems, *, nt):
    i = pl.program_id(0)
    slot = i % 2
    zsem = zsems.at[0]

    def wait_tile(sl):
        for _ in range(2):
            pltpu.make_async_copy(stage.at[sl], xs_hbm.at[pl.ds(0, ROW_TILE)], sem.at[sl]).wait()

    def zero_padding(wait):
        def go(cp):
            cp.wait() if wait else cp.start()

        def one_row(r):
            go(pltpu.make_async_copy(zbuf.at[pl.ds(0, 1)], xs_hbm.at[pl.ds(r, 1)], zsem))

        def per_expert(e, c):
            start = fill_ref[e]
            n = fill_ref[MOE_EXPERTS + e]
            head = jnp.minimum((SUBLANES - start % SUBLANES) % SUBLANES, n)
            mid = pl.multiple_of((n - head) // SUBLANES * SUBLANES, SUBLANES)
            lax.fori_loop(0, head, lambda j, c2: (one_row(start + j), c2)[1], 0)

            @pl.when(mid > 0)
            def _():
                at = pl.multiple_of(start + head, SUBLANES)
                go(pltpu.make_async_copy(zbuf.at[pl.ds(0, mid)], xs_hbm.at[pl.ds(at, mid)], zsem))

            lax.fori_loop(0, n - head - mid, lambda j, c2: (one_row(start + head + mid + j), c2)[1], 0)
            return c
        lax.fori_loop(0, MOE_EXPERTS, per_expert, 0)

        def per_block(j, c):
            at = pl.multiple_of(fill_ref[2 * MOE_EXPERTS] + j * MOE_ROWS, MOE_ROWS)
            go(pltpu.make_async_copy(zbuf, xs_hbm.at[pl.ds(at, MOE_ROWS)], zsem))
            return c
        lax.fori_loop(0, fill_ref[2 * MOE_EXPERTS + 1], per_block, 0)

    @pl.when(i == 0)
    def _():
        zbuf[...] = jnp.zeros(zbuf.shape, zbuf.dtype)
        zero_padding(False)

    @pl.when(i >= 2)
    def _():
        wait_tile(slot)

    stage[slot] = h_ref[...]

    for j in range(ROW_TILE):
        src = stage.at[slot, pl.ds(j, 1)]
        for k in range(2):
            row = dest_ref[2 * (i * ROW_TILE + j) + k]
            pltpu.make_async_copy(src, xs_hbm.at[pl.ds(row, 1)], sem.at[slot]).start(priority=k)

    @pl.when(i == nt - 1)
    def _():
        wait_tile(slot)
        if nt > 1:
            wait_tile(1 - slot)
        zero_padding(True)


def _dispatch(h, dest, fill, n_rows):
    t_count, d = h.shape
    nt = t_count // ROW_TILE
    grid_spec = pltpu.PrefetchScalarGridSpec(
        num_scalar_prefetch=2, grid=(nt,),
        in_specs=[pl.BlockSpec((ROW_TILE, d), lambda i, de, fi: (i, 0))],
        out_specs=pl.BlockSpec(memory_space=pl.ANY),
        scratch_shapes=[pltpu.VMEM((2, ROW_TILE, d), F32), pltpu.VMEM((MOE_ROWS, d), F32),
                        pltpu.SemaphoreType.DMA((2,)), pltpu.SemaphoreType.DMA((1,))])
    return pl.pallas_call(
        functools.partial(_dispatch_kernel, nt=nt),
        out_shape=jax.ShapeDtypeStruct((n_rows, d), F32),
        grid_spec=grid_spec,
        compiler_params=_cparams("arbitrary"), name="dispatch",
    )(dest, fill, h)


def _expert_kernel(be_ref, nv_ref, x_ref, wg_ref, wu_ref, wd_ref, y_ref, wgb, wub, wdb):
    i = pl.program_id(0)

    @pl.when(nv_ref[i] > 0)
    def _():
        @pl.when((i == 0) | (be_ref[i] != be_ref[jnp.maximum(i - 1, 0)]))
        def _():
            wgb[...] = wg_ref[0, 0].astype(BF16)
            wub[...] = wu_ref[0, 0].astype(BF16)
            wdb[...] = wd_ref[0, 0].astype(BF16)

        x = x_ref[...].astype(BF16)
        a = (jax.nn.silu(_dot(x, wgb[...])) * _dot(x, wub[...])).astype(BF16)
        y_ref[...] = _dot(a, wdb[...])

    @pl.when(nv_ref[i] == 0)
    def _():
        y_ref[...] = jnp.zeros(y_ref.shape, y_ref.dtype)


def _experts(xs, block_e, n_valid, layer, wg, wu, wd):
    n_rows, d = xs.shape
    mb = MOE_ROWS
    hid = wg.shape[3]

    def wspec(shape):
        return pl.BlockSpec(shape, lambda i, be, nv: (layer, be[i], 0, 0))

    grid_spec = pltpu.PrefetchScalarGridSpec(
        num_scalar_prefetch=2, grid=(n_rows // mb,),
        in_specs=[pl.BlockSpec((mb, d), lambda i, be, nv: (i, 0)),
                  wspec((1, 1, d, hid)), wspec((1, 1, d, hid)), wspec((1, 1, hid, d))],
        out_specs=pl.BlockSpec((mb, d), lambda i, be, nv: (i, 0)),
        scratch_shapes=[pltpu.VMEM((d, hid), BF16), pltpu.VMEM((d, hid), BF16), pltpu.VMEM((hid, d), BF16)])
    return pl.pallas_call(
        _expert_kernel,
        out_shape=jax.ShapeDtypeStruct((n_rows, d), F32),
        grid_spec=grid_spec,
        compiler_params=_cparams("arbitrary"), name="experts",
    )(block_e, n_valid, xs, wg, wu, wd)


def _final_kernel(dest_ref, x_ref, r_ref, modp_ref, ys_hbm, g_ref, o_ref, ybuf, sem, *, tile_of):
    x, finish = _moe_update(dest_ref, x_ref, r_ref, modp_ref, ys_hbm, ybuf, sem, tile_of, True, two_queues=True)
    o_ref[...] = _rms(x, g_ref[...])
    finish()


def _final(rt, x, moe, final_g, n_len):
    d = x.shape[1]
    lt = n_len // ROW_TILE

    def tile_of(i):
        return (i // lt) * rt.tpb + rt.ctx_tiles + i % lt

    mspecs, margs, scratch = _moe_operands(rt, d, moe, tile_of)
    grid_spec = pltpu.PrefetchScalarGridSpec(
        num_scalar_prefetch=1, grid=(rt.b * lt,),
        in_specs=[pl.BlockSpec((ROW_TILE, d), lambda i, de: (tile_of(i), 0))] + mspecs + [
            pl.BlockSpec((1, d), lambda i, de: (0, 0))],
        out_specs=pl.BlockSpec((ROW_TILE, d), lambda i, de: (i, 0)),
        scratch_shapes=scratch)
    return pl.pallas_call(
        functools.partial(_final_kernel, tile_of=tile_of),
        out_shape=jax.ShapeDtypeStruct((rt.b * n_len, d), F32),
        grid_spec=grid_spec,
        compiler_params=_cparams("arbitrary"), name="final",
    )(moe[3], x, *margs, final_g.reshape(1, d))


def _pre_mla_kernel(dest_ref, x_ref, r_ref, modp_ref, ys_hbm, mod_ref, g_ref, win_ref, qg_ref, kvg_ref,
                    wq_ref, wqs_ref, wk_ref, wv_ref, vone_ref, ct_ref, st_ref,
                    xo_ref, q_ref, k_ref, v_ref, ybuf, sem):
    x, finish = _moe_update(dest_ref, x_ref, r_ref, modp_ref, ys_hbm, ybuf, sem, lambda t: t, True)
    xo_ref[...] = x
    h = _modulated(x, g_ref, mod_ref, 0)
    p = _dot(h.astype(BF16), win_ref[...])
    cq = _rms(p[:, 0:MLA_Q_RANK], qg_ref[...]).astype(BF16)
    ckv = _rms(p[:, MLA_Q_RANK:MLA_Q_RANK + MLA_KV_RANK], kvg_ref[...]).astype(BF16)
    off = MLA_Q_RANK + MLA_KV_RANK
    ct, st = ct_ref[...], st_ref[...]
    k_rope = p[:, off:off + HEAD_PAD] * ct + p[:, off + HEAD_PAD:off + 2 * HEAD_PAD] * st
    qscale = MLA_SCALE * LOG2_E
    for c0 in range(0, MLA_HEADS * HEAD_PAD, MLA_CHUNK):
        cols = slice(c0, c0 + MLA_CHUNK)
        qa = _dot(cq, wq_ref[:, cols])
        qb = _dot(cq, wqs_ref[:, cols])
        kn = _dot(ckv, wk_ref[:, cols])
        v_ref[:, cols] = (_dot(ckv, wv_ref[:, cols]) + vone_ref[:, cols]).astype(v_ref.dtype)
        for h0 in range(0, MLA_CHUNK, HEAD_PAD):
            sl = slice(h0, h0 + HEAD_PAD)
            out = slice(c0 + h0, c0 + h0 + HEAD_PAD)
            q_ref[:, out] = ((qa[:, sl] * ct + qb[:, sl] * st) * qscale).astype(q_ref.dtype)
            k_ref[:, out] = (kn[:, sl] + k_rope).astype(k_ref.dtype)
    finish()


def _pre_mla(rt, x, moe, mods, g1, wts, ct, st):
    d = x.shape[1]
    w_in, qg, kvg, wq, wqs, wk, wv, vone = wts

    def full(a):
        return pl.BlockSpec(a.shape, lambda i, de: (0,) * a.ndim)

    hq = MLA_HEADS * HEAD_PAD
    mspecs, margs, scratch = _moe_operands(rt, d, moe, lambda t: t)
    grid_spec = pltpu.PrefetchScalarGridSpec(
        num_scalar_prefetch=1, grid=(rt.n_tiles,),
        in_specs=[pl.BlockSpec((ROW_TILE, d), lambda i, de: (i, 0))] + mspecs + [
            pl.BlockSpec((1, 8, d), lambda i, de: (rt.mod_idx(i), 0, 0)),
            pl.BlockSpec((1, d), lambda i, de: (0, 0)),
            full(w_in), full(qg), full(kvg), full(wq), full(wqs), full(wk), full(wv), full(vone),
            pl.BlockSpec((ROW_TILE, HEAD_PAD), lambda i, de: (rt.pos_idx(i), 0)),
            pl.BlockSpec((ROW_TILE, HEAD_PAD), lambda i, de: (rt.pos_idx(i), 0))],
        out_specs=[pl.BlockSpec((ROW_TILE, d), lambda i, de: (i, 0)),
                   pl.BlockSpec((ROW_TILE, hq), lambda i, de: (i, 0)),
                   pl.BlockSpec((ROW_TILE, hq), lambda i, de: (i, 0)),
                   pl.BlockSpec((ROW_TILE, hq), lambda i, de: (i, 0))],
        scratch_shapes=scratch)
    return pl.pallas_call(
        _pre_mla_kernel,
        out_shape=[jax.ShapeDtypeStruct((rt.rows, d), F32), jax.ShapeDtypeStruct((rt.rows, hq), BF16),
                   jax.ShapeDtypeStruct((rt.rows, hq), BF16), jax.ShapeDtypeStruct((rt.rows, hq), BF16)],
        grid_spec=grid_spec,
        compiler_params=_cparams("arbitrary"), name="pre_mla",
    )(moe[3], x, *margs, mods, g1.reshape(1, d), w_in, qg, kvg, wq, wqs, wk, wv, vone, ct, st)


def _mla_params(w_in, q_g, kv_g, w_uq, w_ukv):
    d = w_in.shape[0]
    hp, hr = HEAD_PAD, MLA_ROPE // 2
    nq = MLA_NOPE + MLA_ROPE
    kr = w_in[:, MLA_Q_RANK + MLA_KV_RANK:]
    z = jnp.zeros((d, MLA_NOPE), F32)
    zt = jnp.zeros((d, hp - nq), F32)
    kr_a = jnp.concatenate([z, kr, zt], axis=1)
    kr_b = jnp.concatenate([z, -kr[:, hr:], kr[:, :hr], zt], axis=1)
    w_in_p = jnp.concatenate([w_in[:, :MLA_Q_RANK + MLA_KV_RANK], kr_a, kr_b], axis=1).astype(BF16)
    wq = w_uq.reshape(MLA_Q_RANK, MLA_HEADS, nq)
    zq = jnp.zeros((MLA_Q_RANK, MLA_HEADS, hp - nq), F32)
    wq_a = jnp.concatenate([wq, zq], axis=2).reshape(MLA_Q_RANK, MLA_HEADS * hp).astype(BF16)
    wq_b = jnp.concatenate([jnp.zeros_like(wq[:, :, :MLA_NOPE]), -wq[:, :, MLA_NOPE + hr:],
                            wq[:, :, MLA_NOPE:MLA_NOPE + hr], zq], axis=2)
    wq_b = wq_b.reshape(MLA_Q_RANK, MLA_HEADS * hp).astype(BF16)
    wkv = w_ukv.reshape(MLA_KV_RANK, MLA_HEADS, MLA_NOPE + MLA_V)
    wk = jnp.concatenate([wkv[:, :, :MLA_NOPE], jnp.zeros((MLA_KV_RANK, MLA_HEADS, hp - MLA_NOPE), F32)], axis=2)
    wk = wk.reshape(MLA_KV_RANK, MLA_HEADS * hp).astype(BF16)
    wv = wkv[:, :, MLA_NOPE:].reshape(MLA_KV_RANK, MLA_HEADS // 2, 2, MLA_V)
    zv = jnp.zeros((MLA_KV_RANK, MLA_HEADS // 2, hp - MLA_V), F32)
    wv = jnp.concatenate([wv[:, :, 0], zv, zv, wv[:, :, 1]], axis=2).reshape(MLA_KV_RANK, MLA_HEADS * hp)
    lane = jnp.arange(2 * hp) % (2 * hp)
    vone = jnp.tile(jnp.where((lane == ATT_DEN_EVEN) | (lane == hp + ATT_DEN_ODD), 1.0, 0.0), MLA_HEADS // 2)
    return (w_in_p, q_g.reshape(1, -1), kv_g.reshape(1, -1), wq_a, wq_b, wk, wv.astype(BF16),
            vone.reshape(1, -1).astype(F32))


def _attn_kernel(q_ref, k_ref, v_ref, o_ref, *, s_len, l_len):
    t = pl.program_id(2)
    lane = lax.broadcasted_iota(jnp.int32, (ATT_TQ, 2 * MLA_V), 1)

    def attend(nk):
        for pair in range(ATT_HEADS // 2):
            outs = []
            for j, den_lane in ((2 * pair, ATT_DEN_EVEN), (2 * pair + 1, ATT_DEN_ODD)):
                blk = slice(j * HEAD_PAD, (j + 1) * HEAD_PAD)
                s = _dot_nt(q_ref[0, :, blk], k_ref[0, 0:nk, blk])
                p = jnp.exp2(s - jnp.max(s, axis=1, keepdims=True))
                o = _dot(p.astype(BF16), v_ref[0, 0:nk, blk])
                outs.append(o / o[:, den_lane:den_lane + 1])
            o_ref[0, :, pair * 2 * MLA_V:(pair + 1) * 2 * MLA_V] = (
                jnp.where(lane < MLA_V, outs[0], outs[1]).astype(o_ref.dtype))

    ctx_tiles = l_len // ATT_TQ

    @pl.when(t < ctx_tiles)
    def _():
        attend(l_len)

    @pl.when(t >= ctx_tiles)
    def _():
        attend(s_len)


def _attention(q3, k3, v3, l_len):
    b, s, _ = q3.shape
    hq = ATT_HEADS * HEAD_PAD
    hv = ATT_HEADS * MLA_V
    return pl.pallas_call(
        functools.partial(_attn_kernel, s_len=s, l_len=l_len),
        out_shape=jax.ShapeDtypeStruct((b, s, MLA_HEADS * MLA_V), BF16),
        grid=(b, MLA_HEADS // ATT_HEADS, s // ATT_TQ),
        in_specs=[pl.BlockSpec((1, ATT_TQ, hq), lambda bi, hi, ti: (bi, ti, hi)),
                  pl.BlockSpec((1, s, hq), lambda bi, hi, ti: (bi, 0, hi)),
                  pl.BlockSpec((1, s, hq), lambda bi, hi, ti: (bi, 0, hi))],
        out_specs=pl.BlockSpec((1, ATT_TQ, hv), lambda bi, hi, ti: (bi, ti, hi)),
        compiler_params=_cparams("arbitrary", "arbitrary", "arbitrary"), name="attention",
    )(q3, k3, v3)


def _ret_tables(n, l):
    t = np.arange(n, dtype=np.float64)
    inv = ROPE_BASE ** (-np.arange(0, RET_DK, 2, dtype=np.float64) / RET_DK)
    ang = t[:, None] * inv[None, :]
    cos, sin = np.cos(ang), np.sin(ang)
    cos2 = np.concatenate([np.ones((l, RET_DK)), np.concatenate([cos, cos], axis=1)], axis=0)
    sin2 = np.concatenate([np.zeros((l, RET_DK)), np.concatenate([-sin, sin], axis=1)], axis=0)
    return jnp.asarray(cos2, F32), jnp.asarray(sin2, F32)


def _mla_tables(n, l):
    rows = n // GRID_W
    r_pos = np.repeat(np.arange(rows, dtype=np.float64), GRID_W)
    c_pos = np.tile(np.arange(GRID_W, dtype=np.float64), rows)
    ax = MLA_ROPE // 2
    inv = ROPE_BASE ** (-np.arange(0, ax, 2, dtype=np.float64) / ax)
    ang = np.concatenate([r_pos[:, None] * inv[None, :], c_pos[:, None] * inv[None, :]], axis=-1)
    cos, sin = np.cos(ang), np.sin(ang)
    pad = HEAD_PAD - MLA_NOPE - MLA_ROPE
    ct_l = np.concatenate([np.ones((n, MLA_NOPE)), cos, cos, np.zeros((n, pad))], axis=1)
    st_l = np.concatenate([np.zeros((n, MLA_NOPE)), sin, sin, np.zeros((n, pad))], axis=1)
    ct_c = np.concatenate([np.ones((l, MLA_NOPE + MLA_ROPE)), np.zeros((l, pad))], axis=1)
    ct = np.concatenate([ct_c, ct_l], axis=0)
    st = np.concatenate([np.zeros((l, HEAD_PAD)), st_l], axis=0)
    return jnp.asarray(ct, F32), jnp.asarray(st, F32)


def kernel(x, c, ctx, c_ctx, ada_w, ada_b, norm_g, ab_w_in, ab_w_out, ret_decay_logit, lru_conv_w, lru_conv_b, lru_gate_w, lru_gate_b, lru_lambda, mla_w_in, mla_q_norm_g, mla_kv_norm_g, mla_w_uq, mla_w_ukv, mla_w_out, moe_group_w, moe_group_b, moe_expert_w, moe_expert_b, moe_w_gate, moe_w_up, moe_w_down, final_norm_g):
    b, n, d = x.shape
    l = ctx.shape[1]
    s = l + n
    depth = ada_w.shape[0]
    rt = _Rows(b, s, l)

    nrow = (b + 1 + 7) // 8 * 8
    cvec = jnp.concatenate([c, c_ctx[None, :], jnp.zeros((nrow - b - 1, d), F32)], axis=0)
    ada = _ada_all(cvec, ada_w, ada_b)

    def layer_mods(layer):
        lat = ada[layer, :b].reshape(b, 1, 6, d)
        cx = jnp.broadcast_to(ada[layer, b].reshape(1, 1, 6, d), (b, 1, 6, d))
        m = jnp.concatenate([cx, lat], axis=1)
        m = jnp.concatenate([m, jnp.zeros((b, 2, 2, d), F32)], axis=2)
        return m.reshape(2 * b, 8, d)

    cos2, sin2 = _ret_tables(n, l)
    ct, st = _mla_tables(n, l)

    xs = jnp.concatenate([ctx, x], axis=1).reshape(b * s, d)
    out = None
    moe = None
    for layer in range(depth):
        mods = layer_mods(layer)
        i = layer // 2
        if layer % 2 == 0:
            xs, p = _pre_ab(rt, xs, moe, mods, norm_g[layer, 0], ab_w_in[i].astype(BF16))
            p3 = p.reshape(b, s, -1)
            lg = jax.nn.log_sigmoid(ret_decay_logit[i].astype(F32))
            lgv = jnp.broadcast_to(lg.T[:, :, None], (RET_HEADS, 2, LANES))
            lgv = jnp.concatenate([lgv, jnp.zeros((RET_HEADS, 6, LANES), F32)], axis=1)
            ma = _retention(p3, cos2, sin2, lgv, l).reshape(b * s, -1)
            mb = _rglru(p3, *_lru_params(lru_conv_w[i], lru_conv_b[i], lru_gate_w[i], lru_gate_b[i],
                                         lru_lambda[i]), l).reshape(b * s, -1)
            cb = 0
            w_out = ab_w_out[i].astype(BF16)
        else:
            wts = _mla_params(mla_w_in[i], mla_q_norm_g[i], mla_kv_norm_g[i], mla_w_uq[i], mla_w_ukv[i])
            xs, q, k, v = _pre_mla(rt, xs, moe, mods, norm_g[layer, 0], wts, ct, st)
            att = _attention(q.reshape(b, s, -1), k.reshape(b, s, -1), v.reshape(b, s, -1), l)
            ma = mb = att.reshape(b * s, -1)
            cb = 1
            w_out = mla_w_out[i].astype(BF16)
        wr = jnp.concatenate([moe_group_w[layer], moe_expert_w[layer],
                              jnp.zeros((d, LANES - MOE_GROUPS - MOE_EXPERTS), F32)], axis=1)
        wr_hi = wr.astype(BF16)
        wr = jnp.concatenate([wr_hi, (wr - wr_hi.astype(F32)).astype(BF16)], axis=1)
        br = jnp.concatenate([moe_group_b[layer], moe_expert_b[layer],
                              jnp.zeros((LANES - MOE_GROUPS - MOE_EXPERTS,), F32)]).reshape(1, LANES)
        xs, h2, route, cnt = _post(rt, xs, ma, mb, cb, w_out, mods, norm_g[layer, 1], wr, br)
        dest, block_e, n_valid, fill, nb = _moe_plan(route, cnt)
        xsort = _dispatch(h2, dest, fill, nb * MOE_ROWS)
        ys = _experts(xsort, block_e, n_valid, layer, moe_w_gate, moe_w_up, moe_w_down)
        moe = (route, mods, ys, dest)
    out = _final(rt, xs, moe, final_norm_g, n)
    return out.reshape(b, n, d)
```
